```python
import math
import jax, jax.numpy as jnp
from jax import lax
import numpy as np

D_MODEL = 1024
BATCH = 8
SEQ = 2048
DEPTH = 4
DEC_BATCH = 128
DEC_SEQ = 4
PAST_LEN = 16384
PAGE_SIZE = 128

N_MIXERS = 2
N_RET = (DEPTH + 1) // 2
N_HGRN = DEPTH // 2
RET_HEADS = 4
RET_DK = D_MODEL // RET_HEADS
RET_DV = 2 * RET_DK
RET_QK = RET_HEADS * RET_DK
RET_V = RET_HEADS * RET_DV
RET_CHUNK = 128
ROPE_BASE = 10000.0
HG_EXPAND = 128
HG_HEADS = D_MODEL // HG_EXPAND
HG_DK = HG_EXPAND
HG_DV = D_MODEL // HG_HEADS
HG_CHUNK = 16
D_FF = 2816
EPS = 1e-6

kernel_name = "retnet_hgrn2_macaron_hybrid_step"


def rmsnorm(x, gain):
    xf = x.astype(jnp.float32)
    y = xf * lax.rsqrt(jnp.mean(xf * xf, axis=-1, keepdims=True) + EPS)
    return (y * gain.astype(jnp.float32)).astype(x.dtype)


def head_rmsnorm(o, gain):
    y = o * lax.rsqrt(jnp.mean(o * o, axis=-1, keepdims=True) + EPS)
    return y * gain.astype(jnp.float32)[None, :, None, :]


def swiglu_ffn(x, w_up, w_down):
    a, b = jnp.split(x @ w_up, 2, axis=-1)
    return (jax.nn.silu(a) * b) @ w_down


def split_heads(a, n_heads):
    B, T, _ = a.shape
    return a.reshape(B, T, n_heads, -1).transpose(0, 2, 1, 3).astype(jnp.float32)


def merge_heads(o):
    B, H, T, d = o.shape
    return o.transpose(0, 2, 1, 3).reshape(B, T, H * d)


def rotary(x, pos):
    half = x.shape[-1] // 2
    inv_freq = ROPE_BASE ** (-jnp.arange(half, dtype=jnp.float32) / half)
    ang = pos[:, None] * inv_freq[None, :]
    cos, sin = jnp.cos(ang), jnp.sin(ang)
    x1, x2 = x[..., :half], x[..., half:]
    return jnp.concatenate([x1 * cos - x2 * sin, x1 * sin + x2 * cos], axis=-1)


def to_chunks(a, C):
    B, H, T, d = a.shape
    return jnp.moveaxis(a.reshape(B, H, T // C, C, d), 2, 0)


def from_chunks(a):
    n, B, H, C, d = a.shape
    return jnp.moveaxis(a, 0, 2).reshape(B, H, n * C, d)


def retention_chunkwise(q, k, v, S0, log_gamma):
    T = q.shape[2]
    C = math.gcd(T, RET_CHUNK)
    idx = jnp.arange(C, dtype=jnp.float32)
    diff = idx[:, None] - idx[None, :]
    lg = log_gamma[:, None, None]
    decay_mat = jnp.where(diff[None] >= 0, jnp.exp(diff[None] * lg), 0.0)
    q_decay = jnp.exp((idx[None, :] + 1.0) * log_gamma[:, None])
    k_decay = jnp.exp((C - 1.0 - idx[None, :]) * log_gamma[:, None])
    chunk_decay = jnp.exp(C * log_gamma)

    def step(S, inp):
        qc, kc, vc = inp
        scores = jnp.einsum('bhtd,bhsd->bhts', qc, kc) * decay_mat[None]
        o = (jnp.einsum('bhts,bhse->bhte', scores, vc)
             + jnp.einsum('bhtd,bhde->bhte', qc, S) * q_decay[None, :, :, None])
        S = (S * chunk_decay[None, :, None, None]
             + jnp.einsum('bhsd,bhse->bhde', kc * k_decay[None, :, :, None], vc))
        return S, o

    S, o = lax.scan(step, S0, (to_chunks(q, C), to_chunks(k, C), to_chunks(v, C)))
    return from_chunks(o), S


def hgrn2_chunkwise(q, k, v, g, S0):
    T = q.shape[2]
    C = math.gcd(T, HG_CHUNK)
    causal = jnp.tril(jnp.ones((C, C), dtype=bool))

    def step(S, inp):
        qc, kc, vc, gc = inp
        b = jnp.cumsum(gc, axis=2)
        rel = jnp.where(causal[None, None, :, :, None],
                        b[:, :, :, None, :] - b[:, :, None, :, :], -jnp.inf)
        scores = jnp.einsum('bhtd,bhsd,bhtsd->bhts', qc, kc, jnp.exp(rel))
        o = (jnp.einsum('bhts,bhse->bhte', scores, vc)
             + jnp.einsum('bhtd,bhde->bhte', qc * jnp.exp(b), S))
        b_last = b[:, :, -1:, :]
        S = (jnp.exp(b_last)[:, :, 0, :, None] * S
             + jnp.einsum('bhsd,bhse->bhde', kc * jnp.exp(b_last - b), vc))
        return S, o

    S, o = lax.scan(step, S0, (to_chunks(q, C), to_chunks(k, C), to_chunks(v, C), to_chunks(g, C)))
    return from_chunks(o), S


def retention_mixer(x, pos, S0, w_in, gn, w_out):
    h = x @ w_in
    q, k, v, gate = jnp.split(h, [RET_QK, 2 * RET_QK, 2 * RET_QK + RET_V], axis=-1)
    q = rotary(split_heads(q, RET_HEADS), pos)
    k = rotary(split_heads(k, RET_HEADS), pos) * (RET_DK ** -0.5)
    v = split_heads(v, RET_HEADS)
    log_gamma = jnp.log(1.0 - jnp.power(2.0, -5.0 - jnp.arange(RET_HEADS, dtype=jnp.float32)))
    o, S = retention_chunkwise(q, k, v, S0.astype(jnp.float32), log_gamma)
    o = merge_heads(head_rmsnorm(o, gn)).astype(x.dtype)
    return (jax.nn.silu(gate) * o) @ w_out, S


def hgrn2_mixer(x, S0, lb, w_in, gn, w_out):
    h = x @ w_in
    q, z, i, gate = jnp.split(h, 4, axis=-1)
    q = jax.nn.silu(split_heads(q, HG_HEADS))
    z = split_heads(z, HG_HEADS)
    v = split_heads(i, HG_HEADS)
    lbh = lb.astype(jnp.float32).reshape(HG_HEADS, HG_DK)[None, :, None, :]
    g = jnp.logaddexp(jnp.log(lbh), jnp.log1p(-lbh) + jax.nn.log_sigmoid(z))
    k = (1.0 - lbh) * jax.nn.sigmoid(-z)
    o, S = hgrn2_chunkwise(q, k, v, g, S0.astype(jnp.float32))
    o = merge_heads(head_rmsnorm(o, gn)).astype(x.dtype)
    return (jax.nn.silu(gate) * o) @ w_out, S


def trunk(x, pos, ret_state, hg_state, norm_gain, ffn_w_up, ffn_w_down,
          ret_w_in, ret_norm, ret_w_out, hg_w_in, hg_lb_logits, hg_norm, hg_w_out, final_norm):
    lb_all = jnp.cumsum(jax.nn.softmax(hg_lb_logits.astype(jnp.float32), axis=0), axis=0)
    lb_all = lb_all - lb_all[:1]
    new_ret, new_hg = [], []
    for layer in range(DEPTH):
        x = x + 0.5 * swiglu_ffn(rmsnorm(x, norm_gain[layer, 0]), ffn_w_up[layer, 0], ffn_w_down[layer, 0])
        h = rmsnorm(x, norm_gain[layer, 1])
        j = layer // N_MIXERS
        if layer % N_MIXERS == 0:
            y, S = retention_mixer(h, pos, ret_state[j], ret_w_in[j], ret_norm[j], ret_w_out[j])
            new_ret.append(S.astype(ret_state.dtype))
        else:
            y, S = hgrn2_mixer(h, hg_state[j], lb_all[j], hg_w_in[j], hg_norm[j], hg_w_out[j])
            new_hg.append(S.astype(hg_state.dtype))
        x = x + y
        x = x + 0.5 * swiglu_ffn(rmsnorm(x, norm_gain[layer, 2]), ffn_w_up[layer, 1], ffn_w_down[layer, 1])
    return rmsnorm(x, final_norm), jnp.stack(new_ret), jnp.stack(new_hg)


def setup_inputs(seed: int = 0) -> dict:
    key = jax.random.key(seed)
    ks = jax.random.split(key, 16)
    f32 = jnp.float32
    nrm = lambda k, shape, s: jax.random.normal(k, shape, f32) * s
    return {
        "x_prompt": nrm(ks[0], (BATCH, SEQ, D_MODEL), 1.0),
        "x_sample": nrm(ks[1], (DEC_BATCH, DEC_SEQ, D_MODEL), 1.0),
        "state_ret": nrm(ks[2], (N_RET, DEC_BATCH, RET_HEADS, RET_DK, RET_DV), 0.05),
        "state_hgrn": nrm(ks[3], (N_HGRN, DEC_BATCH, HG_HEADS, HG_DK, HG_DV), 0.5),
        "norm_gain": 1.0 + nrm(ks[4], (DEPTH, 3, D_MODEL), 0.02),
        "ffn_w_up": nrm(ks[5], (DEPTH, 2, D_MODEL, 2 * D_FF), D_MODEL ** -0.5),
        "ffn_w_down": nrm(ks[6], (DEPTH, 2, D_FF, D_MODEL), D_FF ** -0.5),
        "ret_w_in": nrm(ks[7], (N_RET, D_MODEL, 2 * RET_QK + 2 * RET_V), D_MODEL ** -0.5),
        "ret_norm": 1.0 + nrm(ks[8], (N_RET, RET_HEADS, RET_DV), 0.02),
        "ret_w_out": nrm(ks[9], (N_RET, RET_V, D_MODEL), RET_V ** -0.5),
        "hg_w_in": nrm(ks[10], (N_HGRN, D_MODEL, 4 * D_MODEL), D_MODEL ** -0.5),
        "hg_lb_logits": nrm(ks[11], (N_HGRN, HG_HEADS * HG_DK), 0.1),
        "hg_norm": 1.0 + nrm(ks[12], (N_HGRN, HG_HEADS, HG_DV), 0.02),
        "hg_w_out": nrm(ks[13], (N_HGRN, D_MODEL, D_MODEL), D_MODEL ** -0.5),
        "final_norm": 1.0 + nrm(ks[14], (D_MODEL,), 0.02),
    }


def reference(x_prompt, x_sample, state_ret, state_hgrn, norm_gain, ffn_w_up, ffn_w_down,
              ret_w_in, ret_norm, ret_w_out, hg_w_in, hg_lb_logits, hg_norm, hg_w_out, final_norm):
    pos_prompt = jnp.arange(SEQ, dtype=jnp.float32)
    pos_sample = PAST_LEN + jnp.arange(DEC_SEQ, dtype=jnp.float32)
    ret0 = jnp.zeros((N_RET, BATCH, RET_HEADS, RET_DK, RET_DV), state_ret.dtype)
    hg0 = jnp.zeros((N_HGRN, BATCH, HG_HEADS, HG_DK, HG_DV), state_hgrn.dtype)
    y_prompt, ret_prompt, hg_prompt = trunk(
        x_prompt, pos_prompt, ret0, hg0, norm_gain, ffn_w_up, ffn_w_down,
        ret_w_in, ret_norm, ret_w_out, hg_w_in, hg_lb_logits, hg_norm, hg_w_out, final_norm)
    y_sample, ret_sample, hg_sample = trunk(
        x_sample, pos_sample, state_ret, state_hgrn, norm_gain, ffn_w_up, ffn_w_down,
        ret_w_in, ret_norm, ret_w_out, hg_w_in, hg_lb_logits, hg_norm, hg_w_out, final_norm)
    return (y_prompt, y_sample, ret_prompt, ret_sample, hg_prompt, hg_sample)
```

```python
import functools
import math

import jax
import jax.numpy as jnp
import numpy as np
from jax import lax
from jax.experimental import pallas as pl
from jax.experimental.pallas import tpu as pltpu

F32 = jnp.float32
BF16 = jnp.bfloat16

EPS = 1e-6
ROPE_BASE = 10000.0
RET_HEADS = 4
RET_CHUNK = 128
HG_HEADS = 8
HG_CHUNK = 128
RET_SAMPLE_SEQS = 8
LANE = 128
VMEM_LIMIT = 56 * 1024 * 1024


def _cparams(n_axes):
    return pltpu.CompilerParams(
        dimension_semantics=("arbitrary",) * n_axes,
        vmem_limit_bytes=VMEM_LIMIT)


def _resident(shape):
    return pl.BlockSpec(shape, lambda *_: (0,) * len(shape),
                        pipeline_mode=pl.Buffered(1))


def _dot(a, b):
    return jnp.dot(a, b, preferred_element_type=F32)


def _dot_nt(a, b):
    return lax.dot_general(a, b, (((1,), (1,)), ((), ())),
                           preferred_element_type=F32)


def _dot_tn(a, b):
    return lax.dot_general(a, b, (((0,), (0,)), ((), ())),
                           preferred_element_type=F32)


def _rmsnorm_bf16(x, gain):
    ms = jnp.mean(x * x, axis=-1, keepdims=True)
    return (x * lax.rsqrt(ms + EPS) * gain).astype(BF16)


def _silu(x):
    return x * jax.nn.sigmoid(x)


def _ffn_kernel(x_ref, g_ref, wu_ref, wd_ref, o_ref, h_ref, *, d_ff, fc):
    x = x_ref[...]
    xn = _rmsnorm_bf16(x, g_ref[...])
    for c in range(d_ff // fc):
        a = _dot(xn, wu_ref[:, c * fc:(c + 1) * fc])
        b = _dot(xn, wu_ref[:, d_ff + c * fc:d_ff + (c + 1) * fc])
        h_ref[:, c * fc:(c + 1) * fc] = (_silu(a) * b).astype(BF16)
    o_ref[...] = x + 0.5 * _dot(h_ref[...], wd_ref[...])


def _ffn(x, gain, w_up, w_down, tm):
    n, d = x.shape
    d_ff = w_down.shape[0]
    fc = 256
    assert d_ff % fc == 0 and n % tm == 0
    return pl.pallas_call(
        functools.partial(_ffn_kernel, d_ff=d_ff, fc=fc),
        out_shape=jax.ShapeDtypeStruct((n, d), F32),
        grid=(n // tm,),
        in_specs=[pl.BlockSpec((tm, d), lambda i: (i, 0)),
                  _resident((1, d)),
                  _resident((d, 2 * d_ff)),
                  _resident((d_ff, d))],
        out_specs=pl.BlockSpec((tm, d), lambda i: (i, 0)),
        scratch_shapes=[pltpu.VMEM((tm, d_ff), BF16)],
        compiler_params=_cparams(1),
        name="ffn",
    )(x, gain.reshape(1, d), w_up, w_down)


def _out_proj_kernel(x_ref, a_ref, w_ref, o_ref):
    o_ref[...] = x_ref[...] + _dot(a_ref[...], w_ref[...])


def _out_proj(x, a, w, tm):
    n, d = x.shape
    k = a.shape[1]
    return pl.pallas_call(
        _out_proj_kernel,
        out_shape=jax.ShapeDtypeStruct((n, d), F32),
        grid=(n // tm,),
        in_specs=[pl.BlockSpec((tm, d), lambda i: (i, 0)),
                  pl.BlockSpec((tm, k), lambda i: (i, 0)),
                  _resident((k, d))],
        out_specs=pl.BlockSpec((tm, d), lambda i: (i, 0)),
        compiler_params=_cparams(1),
        name="out_proj",
    )(x, a, w)


def _final_norm_kernel(x_ref, g_ref, o_ref):
    x = x_ref[...]
    ms = jnp.mean(x * x, axis=-1, keepdims=True)
    o_ref[...] = x * lax.rsqrt(ms + EPS) * g_ref[...]


def _final_norm(x, gain, tm):
    n, d = x.shape
    return pl.pallas_call(
        _final_norm_kernel,
        out_shape=jax.ShapeDtypeStruct((n, d), F32),
        grid=(n // tm,),
        in_specs=[pl.BlockSpec((tm, d), lambda i: (i, 0)), _resident((1, d))],
        out_specs=pl.BlockSpec((tm, d), lambda i: (i, 0)),
        compiler_params=_cparams(1),
        name="final_norm",
    )(x, gain.reshape(1, d))


def _ret_proj_kernel(x_ref, g_ref, w_ref, cos_ref, sin_ref,
                     q_ref, k_ref, v_ref, sg_ref, *, dk, dv, heads):
    xn = _rmsnorm_bf16(x_ref[...], g_ref[...])
    cos = cos_ref[...]
    sin = sin_ref[...]
    half = dk // 2
    qk = heads * dk
    k_scale = dk ** -0.5
    for h in range(heads):
        for base, out_ref, scale in ((0, q_ref, None), (qk, k_ref, k_scale)):
            t = _dot(xn, w_ref[:, base + h * dk:base + (h + 1) * dk])
            x1, x2 = t[:, :half], t[:, half:]
            r1 = x1 * cos - x2 * sin
            r2 = x1 * sin + x2 * cos
            if scale is not None:
                r1, r2 = r1 * scale, r2 * scale
            out_ref[:, h * dk:h * dk + half] = r1.astype(BF16)
            out_ref[:, h * dk + half:(h + 1) * dk] = r2.astype(BF16)
    for h in range(heads):
        lo = 2 * qk + h * dv
        v_ref[:, h * dv:(h + 1) * dv] = _dot(xn, w_ref[:, lo:lo + dv]).astype(BF16)
        lo = 2 * qk + heads * dv + h * dv
        gate = _dot(xn, w_ref[:, lo:lo + dv])
        sg_ref[:, h * dv:(h + 1) * dv] = _silu(gate).astype(BF16)


def _ret_proj(x, gain, w_in, cos_tab, sin_tab, tm, n_prompt_tiles, tiles_per_seq):
    n, d = x.shape
    heads = RET_HEADS
    dk = d // heads
    dv = 2 * dk
    half = dk // 2

    def tab_map(i):
        return (jnp.where(i < n_prompt_tiles, i % tiles_per_seq, tiles_per_seq), 0)

    row = lambda w: pl.BlockSpec((tm, w), lambda i: (i, 0))
    return pl.pallas_call(
        functools.partial(_ret_proj_kernel, dk=dk, dv=dv, heads=heads),
        out_shape=(jax.ShapeDtypeStruct((n, heads * dk), BF16),
                   jax.ShapeDtypeStruct((n, heads * dk), BF16),
                   jax.ShapeDtypeStruct((n, heads * dv), BF16),
                   jax.ShapeDtypeStruct((n, heads * dv), BF16)),
        grid=(n // tm,),
        in_specs=[row(d), _resident((1, d)), _resident(w_in.shape),
                  pl.BlockSpec((tm, half), tab_map),
                  pl.BlockSpec((tm, half), tab_map)],
        out_specs=(row(heads * dk), row(heads * dk), row(heads * dv), row(heads * dv)),
        compiler_params=_cparams(1),
        name="ret_proj",
    )(x, gain.reshape(1, d), w_in, cos_tab, sin_tab)


def _head_norm_gate(o, gn, sg):
    ms = jnp.mean(o * o, axis=-1, keepdims=True)
    return (o * lax.rsqrt(ms + EPS) * gn * sg.astype(F32)).astype(BF16)


def _ret_core_prompt_kernel(q_ref, k_ref, v_ref, sg_ref, gn_ref, dm_ref, qd_ref, kd_ref,
                            cd_ref, og_ref, s_ref, *, n_chunks, chunk):
    h = pl.program_id(1)
    s_ref[0, 0] = jnp.zeros(s_ref.shape[2:], F32)
    dm = dm_ref[0]
    qd = qd_ref[0]
    kd = kd_ref[0]
    cd = cd_ref[h]
    gn = gn_ref[0]

    def body(c, carry):
        rows = pl.ds(pl.multiple_of(c * chunk, chunk), chunk)
        qc = q_ref[rows, :]
        kc = k_ref[rows, :]
        vc = v_ref[rows, :]
        s_old = s_ref[0, 0]
        scores = _dot_nt(qc, kc) * dm
        o = _dot(scores.astype(BF16), vc) + _dot(qc, s_old.astype(BF16)) * qd
        kdec = (kc.astype(F32) * kd).astype(BF16)
        s_ref[0, 0] = s_old * cd + _dot_tn(kdec, vc)
        og_ref[rows, :] = _head_norm_gate(o, gn, sg_ref[rows, :])
        return carry

    lax.fori_loop(0, n_chunks, body, 0)


def _ret_decay_tables(chunk):
    heads = RET_HEADS
    log_gamma = jnp.log(1.0 - jnp.power(2.0, -5.0 - jnp.arange(heads, dtype=F32)))
    idx = jnp.arange(chunk, dtype=F32)
    diff = idx[:, None] - idx[None, :]
    lg = log_gamma[:, None, None]
    decay_mat = jnp.where(diff[None] >= 0, jnp.exp(diff[None] * lg), 0.0)
    q_decay = jnp.exp((idx[None, :] + 1.0) * log_gamma[:, None])
    k_decay = jnp.exp((chunk - 1.0 - idx[None, :]) * log_gamma[:, None])
    chunk_decay = jnp.exp(chunk * log_gamma)
    return decay_mat, q_decay, k_decay, chunk_decay


def _ret_core_prompt(q, k, v, sg, gn, n_seqs, seq_len):
    n = q.shape[0]
    heads = RET_HEADS
    dk = q.shape[1] // heads
    dv = v.shape[1] // heads
    chunk = math.gcd(seq_len, RET_CHUNK)
    dm, qd, kd, cd = _ret_decay_tables(chunk)
    per_head = lambda w: pl.BlockSpec((seq_len, w), lambda b, h: (b, h))
    tab = lambda shape: pl.BlockSpec((1,) + shape, lambda b, h: (h, 0, 0))
    return pl.pallas_call(
        functools.partial(_ret_core_prompt_kernel, n_chunks=seq_len // chunk, chunk=chunk),
        out_shape=(jax.ShapeDtypeStruct((n, heads * dv), BF16),
                   jax.ShapeDtypeStruct((n_seqs, heads, dk, dv), F32)),
        grid=(n_seqs, heads),
        in_specs=[per_head(dk), per_head(dk), per_head(dv), per_head(dv),
                  tab((1, dv)), tab((chunk, chunk)), tab((chunk, 1)), tab((chunk, 1)),
                  pl.BlockSpec(memory_space=pltpu.SMEM)],
        out_specs=(per_head(dv),
                   pl.BlockSpec((1, 1, dk, dv), lambda b, h: (b, h, 0, 0))),
        compiler_params=_cparams(2),
        name="ret_core_prompt",
    )(q, k, v, sg, gn.reshape(heads, 1, dv), dm, qd[:, :, None], kd[:, :, None], cd)


def _ret_core_sample_kernel(q_ref, k_ref, v_ref, sg_ref, gn_ref, dm_ref, qd_ref, kd_ref,
                            cd_ref, s0_ref, og_in_ref, og_ref, s_ref, *, seqs, dec_len):
    del og_in_ref
    h = pl.program_id(1)
    rows = seqs * dec_len
    qb = q_ref[...]
    vb = v_ref[...]
    kdec = k_ref[...].astype(F32) * kd_ref[0]
    cd = cd_ref[h]
    row_seq = lax.broadcasted_iota(jnp.int32, (rows, 1), 0) // dec_len
    scores = _dot_nt(qb, k_ref[...]) * dm_ref[0]
    o = _dot(scores.astype(BF16), vb)
    inter = jnp.zeros(o.shape, F32)
    for i in range(seqs):
        s_old = s0_ref[i, 0]
        mine = row_seq == i
        inter = jnp.where(mine, _dot(qb, s_old.astype(BF16)), inter)
        k_i = jnp.where(mine, kdec, 0.0).astype(BF16)
        s_ref[i, 0] = s_old * cd + _dot_tn(k_i, vb)
    o = o + inter * qd_ref[0]
    og_ref[...] = _head_norm_gate(o, gn_ref[0], sg_ref[...])


def _ret_core_sample(q, k, v, sg, gn, s0, og, n_prompt_rows, dec_len):
    heads = RET_HEADS
    n_seqs = s0.shape[0]
    dk, dv = s0.shape[2], s0.shape[3]
    seqs = RET_SAMPLE_SEQS
    rows = seqs * dec_len
    assert n_seqs % seqs == 0 and n_prompt_rows % rows == 0
    chunk = math.gcd(dec_len, RET_CHUNK)
    assert chunk == dec_len
    dm, qd, kd, cd = _ret_decay_tables(chunk)
    seq_id = np.arange(rows) // dec_len
    same = jnp.asarray(seq_id[:, None] == seq_id[None, :])
    dm_blk = jnp.where(same[None], jnp.tile(dm, (1, seqs, seqs)), 0.0)
    qd_blk = jnp.tile(qd, (1, seqs))[:, :, None]
    kd_blk = jnp.tile(kd, (1, seqs))[:, :, None]
    off = n_prompt_rows // rows
    per_head = lambda w: pl.BlockSpec((rows, w), lambda i, h: (off + i, h))
    tab = lambda shape: pl.BlockSpec((1,) + shape, lambda i, h: (h, 0, 0))
    state = pl.BlockSpec((seqs, 1, dk, dv), lambda i, h: (i, h, 0, 0))
    og_new, s_new = pl.pallas_call(
        functools.partial(_ret_core_sample_kernel, seqs=seqs, dec_len=dec_len),
        out_shape=(jax.ShapeDtypeStruct(og.shape, og.dtype),
                   jax.ShapeDtypeStruct(s0.shape, F32)),
        grid=(n_seqs // seqs, heads),
        in_specs=[per_head(dk), per_head(dk), per_head(dv), per_head(dv),
                  tab((1, dv)), tab((rows, rows)), tab((rows, 1)), tab((rows, 1)),
                  pl.BlockSpec(memory_space=pltpu.SMEM), state,
                  pl.BlockSpec(memory_space=pl.ANY)],
        out_specs=(per_head(dv), state),
        input_output_aliases={10: 0},
        compiler_params=_cparams(2),
        name="ret_core_sample",
    )(q, k, v, sg, gn.reshape(heads, 1, dv), dm_blk, qd_blk, kd_blk, cd, s0, og)
    return og_new, s_new


def _hg_proj_kernel(x_ref, g_ref, w_ref, lbl_ref, q_ref, k_ref, gl_ref, v_ref, sg_ref,
                    *, layer, heads):
    d = x_ref.shape[1]
    dh = d // heads
    xn = _rmsnorm_bf16(x_ref[...], g_ref[...])
    logits = lbl_ref[...]
    e = jnp.exp(logits - jnp.max(logits, axis=0, keepdims=True))
    sm = e / jnp.sum(e, axis=0, keepdims=True)
    cum = sm[0:1]
    for i in range(1, layer + 1):
        cum = cum + sm[i:i + 1]
    lb = cum - sm[0:1]
    log_lb = jnp.log(lb)
    log_1m_lb = jnp.log1p(-lb)
    for h in range(heads):
        cols = slice(h * dh, (h + 1) * dh)
        q = _dot(xn, w_ref[:, h * dh:(h + 1) * dh])
        q_ref[:, cols] = _silu(q).astype(BF16)
        z = _dot(xn, w_ref[:, d + h * dh:d + (h + 1) * dh])
        ez = jnp.exp(-jnp.abs(z))
        log_sig = jnp.minimum(z, 0.0) - jnp.log1p(ez)
        a = log_lb[:, cols]
        b = log_1m_lb[:, cols] + log_sig
        delta = a - b
        g = jnp.where(delta != delta, a + b,
                      jnp.maximum(a, b) + jnp.log1p(jnp.exp(-jnp.abs(delta))))
        gl_ref[:, cols] = g
        sig_neg = jnp.where(z >= 0.0, ez, 1.0) / (1.0 + ez)
        k_ref[:, cols] = ((1.0 - lb[:, cols]) * sig_neg).astype(BF16)
        v = _dot(xn, w_ref[:, 2 * d + h * dh:2 * d + (h + 1) * dh])
        v_ref[:, cols] = v.astype(BF16)
        gate = _dot(xn, w_ref[:, 3 * d + h * dh:3 * d + (h + 1) * dh])
        sg_ref[:, cols] = _silu(gate).astype(BF16)


def _hg_proj(x, gain, w_in, lb_logits, layer, tm):
    n, d = x.shape
    row = lambda dt: jax.ShapeDtypeStruct((n, d), dt)
    spec = pl.BlockSpec((tm, d), lambda i: (i, 0))
    return pl.pallas_call(
        functools.partial(_hg_proj_kernel, layer=layer, heads=HG_HEADS),
        out_shape=(row(BF16), row(BF16), row(F32), row(BF16), row(BF16)),
        grid=(n // tm,),
        in_specs=[spec, _resident((1, d)), _resident(w_in.shape),
                  _resident(lb_logits.shape)],
        out_specs=(spec, spec, spec, spec, spec),
        compiler_params=_cparams(1),
        name="hg_proj",
    )(x, gain.reshape(1, d), w_in, lb_logits)


def _hg_tables(chunk, block):
    n_levels = int(math.log2(block))
    assert 2 ** n_levels == block and chunk % block == 0
    t = np.arange(chunk)[:, None]
    j = np.arange(chunk)[None, :]
    same_run = (t // block) == (j // block)
    sums = [same_run & (j <= t), same_run & (j > t)]
    masks = [t == j]
    for l in range(n_levels):
        m = 2 ** l
        mid = (t // (2 * m)) * (2 * m) + m
        upper = t >= mid
        sums.append(np.where(upper, (j >= mid) & (j <= t), (j > t) & (j < mid)))
        same_group = (t // (2 * m)) == (j // (2 * m))
        masks.append(same_group & upper & (j < mid))
    sums = jnp.asarray(np.concatenate(sums, axis=0).astype(np.float32), dtype=BF16)
    masks = jnp.asarray(np.stack(masks).astype(np.float32))
    return sums, masks, n_levels


def _hg_chunk(q, k, v, g, sums_ref, masks_ref, n_levels):
    c = q.shape[0]
    g_hi = g.astype(BF16)
    g_lo = (g - g_hi.astype(F32)).astype(BF16)
    nd = _dot(sums_ref[...], jnp.concatenate([g_hi, g_lo], axis=1))
    nd = nd[:, :LANE] + nd[:, LANE:]
    qf = q.astype(F32)
    kf = k.astype(F32)
    scores = _dot_nt(q, k) * masks_ref[0]
    for l in range(n_levels):
        e = jnp.exp(nd[(2 + l) * c:(3 + l) * c])
        s = _dot_nt((qf * e).astype(BF16), (kf * e).astype(BF16))
        scores = scores + s * masks_ref[1 + l]
    o_intra = _dot(scores.astype(BF16), v)
    eb = jnp.exp(nd[:c])
    q_dec = (qf * eb).astype(BF16)
    k_dec = (kf * jnp.exp(nd[c:2 * c])).astype(BF16)
    return o_intra, q_dec, k_dec, eb


def _hg_core_prompt_kernel(q_ref, k_ref, v_ref, g_ref, sg_ref, gn_ref, sums_ref, masks_ref,
                           og_ref, s_ref, st_ref, *, n_chunks, chunk, n_levels):
    st_ref[...] = jnp.zeros(st_ref.shape, F32)
    gn = gn_ref[0]

    def body(c, carry):
        rows = pl.ds(pl.multiple_of(c * chunk, chunk), chunk)
        vc = v_ref[rows, :]
        o_intra, q_dec, k_dec, eb = _hg_chunk(
            q_ref[rows, :], k_ref[rows, :], vc, g_ref[rows, :], sums_ref, masks_ref, n_levels)
        st_old = st_ref[...]
        o = o_intra + _dot_nt(q_dec, st_old.astype(BF16))
        st_ref[...] = st_old * eb[chunk - 1:chunk, :] + _dot_tn(vc, k_dec)
        og_ref[rows, :] = _head_norm_gate(o, gn, sg_ref[rows, :])
        return carry

    lax.fori_loop(0, n_chunks, body, 0)
    s_ref[0, 0] = st_ref[...].T


def _hg_core_prompt(q, k, v, g, sg, gn, n_seqs, seq_len):
    n, d = q.shape
    heads = HG_HEADS
    dh = d // heads
    chunk = math.gcd(seq_len, HG_CHUNK)
    sums, masks, n_levels = _hg_tables(chunk, chunk)
    per_head = pl.BlockSpec((seq_len, dh), lambda b, h: (b, h))
    return pl.pallas_call(
        functools.partial(_hg_core_prompt_kernel, n_chunks=seq_len // chunk, chunk=chunk,
                          n_levels=n_levels),
        out_shape=(jax.ShapeDtypeStruct((n, d), BF16),
                   jax.ShapeDtypeStruct((n_seqs, heads, dh, dh), F32)),
        grid=(n_seqs, heads),
        in_specs=[per_head, per_head, per_head, per_head, per_head,
                  pl.BlockSpec((1, 1, dh), lambda b, h: (h, 0, 0)),
                  _resident(sums.shape), _resident(masks.shape)],
        out_specs=(per_head, pl.BlockSpec((1, 1, dh, dh), lambda b, h: (b, h, 0, 0))),
        scratch_shapes=[pltpu.VMEM((dh, dh), F32)],
        compiler_params=_cparams(2),
        name="hg_core_prompt",
    )(q, k, v, g, sg, gn.reshape(heads, 1, dh), sums, masks)


def _hg_core_sample_kernel(q_ref, k_ref, v_ref, g_ref, sg_ref, gn_ref, sums_ref, masks_ref,
                           s0_ref, og_in_ref, og_ref, s_ref, *, seqs, dec_len, n_levels):
    del og_in_ref
    rows = seqs * dec_len
    vb = v_ref[...]
    o, q_dec, k_dec, eb = _hg_chunk(
        q_ref[...], k_ref[...], vb, g_ref[...], sums_ref, masks_ref, n_levels)
    eb_t = eb.T
    k_dec_t = k_dec.astype(F32).T
    row_seq = lax.broadcasted_iota(jnp.int32, (rows, 1), 0) // dec_len
    col_seq = lax.broadcasted_iota(jnp.int32, (1, rows), 1) // dec_len
    inter = jnp.zeros(o.shape, F32)
    for i in range(seqs):
        s_old = s0_ref[i, 0]
        inter = jnp.where(row_seq == i, _dot(q_dec, s_old.astype(BF16)), inter)
        k_i = jnp.where(col_seq == i, k_dec_t, 0.0).astype(BF16)
        last = (i + 1) * dec_len - 1
        s_ref[i, 0] = s_old * eb_t[:, last:last + 1] + _dot(k_i, vb)
    og_ref[...] = _head_norm_gate(o + inter, gn_ref[0], sg_ref[...])


def _hg_core_sample(q, k, v, g, sg, gn, s0, og, n_prompt_rows, dec_len):
    heads = HG_HEADS
    n_seqs = s0.shape[0]
    dh = s0.shape[2]
    rows = math.gcd(n_seqs * dec_len, HG_CHUNK)
    seqs = rows // dec_len
    assert n_prompt_rows % rows == 0 and rows % dec_len == 0
    sums, masks, n_levels = _hg_tables(rows, dec_len)
    off = n_prompt_rows // rows
    per_head = pl.BlockSpec((rows, dh), lambda i, h: (off + i, h))
    state = pl.BlockSpec((seqs, 1, dh, dh), lambda i, h: (i, h, 0, 0))
    og_new, s_new = pl.pallas_call(
        functools.partial(_hg_core_sample_kernel, seqs=seqs, dec_len=dec_len,
                          n_levels=n_levels),
        out_shape=(jax.ShapeDtypeStruct(og.shape, og.dtype),
                   jax.ShapeDtypeStruct(s0.shape, F32)),
        grid=(n_seqs // seqs, heads),
        in_specs=[per_head, per_head, per_head, per_head, per_head,
                  pl.BlockSpec((1, 1, dh), lambda i, h: (h, 0, 0)),
                  _resident(sums.shape), _resident(masks.shape), state,
                  pl.BlockSpec(memory_space=pl.ANY)],
        out_specs=(per_head, state),
        input_output_aliases={9: 0},
        compiler_params=_cparams(2),
        name="hg_core_sample",
    )(q, k, v, g, sg, gn.reshape(heads, 1, dh), sums, masks, s0, og)
    return og_new, s_new


def _rope_tables(seq_len, past_len, dec_len, tm, half):
    inv_freq = ROPE_BASE ** (-jnp.arange(half, dtype=F32) / half)
    pos_sample = past_len + jnp.arange(dec_len, dtype=F32)
    pos = jnp.concatenate([jnp.arange(seq_len, dtype=F32),
                           jnp.tile(pos_sample, tm // dec_len)])
    ang = pos[:, None] * inv_freq[None, :]
    return jnp.cos(ang), jnp.sin(ang)


def _forward(x_prompt, x_sample, state_ret, state_hgrn, norm_gain, ffn_w_up, ffn_w_down,
             ret_w_in, ret_norm, ret_w_out, hg_w_in, hg_lb_logits, hg_norm, hg_w_out,
             final_norm, *, past_len, tm):
    n_seqs, seq_len, d = x_prompt.shape
    dec_seqs, dec_len, _ = x_sample.shape
    n_prompt = n_seqs * seq_len
    n_sample = dec_seqs * dec_len
    assert seq_len % tm == 0 and n_sample % tm == 0 and tm % dec_len == 0
    depth = norm_gain.shape[0]

    x = jnp.concatenate([x_prompt.reshape(n_prompt, d), x_sample.reshape(n_sample, d)])
    cos_tab, sin_tab = _rope_tables(seq_len, past_len, dec_len, tm, d // RET_HEADS // 2)
    bf = lambda w: w.astype(BF16)

    ret_prompt, ret_sample, hg_prompt, hg_sample = [], [], [], []
    for layer in range(depth):
        x = _ffn(x, norm_gain[layer, 0], bf(ffn_w_up[layer, 0]), bf(ffn_w_down[layer, 0]), tm)
        j = layer // 2
        if layer % 2 == 0:
            q, k, v, sg = _ret_proj(x, norm_gain[layer, 1], bf(ret_w_in[j]), cos_tab, sin_tab,
                                    tm, n_prompt // tm, seq_len // tm)
            og, s_p = _ret_core_prompt(q, k, v, sg, ret_norm[j], n_seqs, seq_len)
            og, s_s = _ret_core_sample(q, k, v, sg, ret_norm[j], state_ret[j], og,
                                       n_prompt, dec_len)
            ret_prompt.append(s_p)
            ret_sample.append(s_s)
            x = _out_proj(x, og, bf(ret_w_out[j]), tm)
        else:
            q, k, g, v, sg = _hg_proj(x, norm_gain[layer, 1], bf(hg_w_in[j]), hg_lb_logits, j, tm)
            og, s_p = _hg_core_prompt(q, k, v, g, sg, hg_norm[j], n_seqs, seq_len)
            og, s_s = _hg_core_sample(q, k, v, g, sg, hg_norm[j], state_hgrn[j], og,
                                      n_prompt, dec_len)
            hg_prompt.append(s_p)
            hg_sample.append(s_s)
            x = _out_proj(x, og, bf(hg_w_out[j]), tm)
        x = _ffn(x, norm_gain[layer, 2], bf(ffn_w_up[layer, 1]), bf(ffn_w_down[layer, 1]), tm)
    y = _final_norm(x, final_norm, tm)
    return (y[:n_prompt].reshape(n_seqs, seq_len, d),
            y[n_prompt:].reshape(dec_seqs, dec_len, d),
            jnp.stack(ret_prompt), jnp.stack(ret_sample),
            jnp.stack(hg_prompt), jnp.stack(hg_sample))


def kernel(x_prompt, x_sample, state_ret, state_hgrn, norm_gain, ffn_w_up, ffn_w_down,
           ret_w_in, ret_norm, ret_w_out, hg_w_in, hg_lb_logits, hg_norm, hg_w_out, final_norm):
    return _forward(x_prompt, x_sample, state_ret, state_hgrn, norm_gain, ffn_w_up,
                    ffn_w_down, ret_w_in, ret_norm, ret_w_out, hg_w_in, hg_lb_logits,
                    hg_norm, hg_w_out, final_norm, past_len=16384, tm=512)
```

```python
import functools
import math

import jax
import jax.numpy as jnp
import numpy as np
from jax import lax
from jax.experimental import pallas as pl
from jax.experimental.pallas import tpu as pltpu

F32 = jnp.float32
BF16 = jnp.bfloat16

EPS = 1e-6
ROPE_BASE = 10000.0
PAST_LEN = 16384
RET_HEADS = 4
RET_CHUNK = 128
RET_HEADS_PER_STEP = 2
RET_SAMPLE_SEQS = 8
HG_HEADS = 8
HG_CHUNK = 128
HG_HEADS_PER_STEP = 4
TOKEN_TILE = 1024
MIX_FFN_TILE = 512
FFN_COLS = 256
LANE = 128
SUBLANE = 8
VMEM_LIMIT = 56 * 1024 * 1024
LOG2E = 1.4426950408889634


def _cparams(n_axes):
    return pltpu.CompilerParams(
        dimension_semantics=("arbitrary",) * n_axes,
        vmem_limit_bytes=VMEM_LIMIT)


def _resident(shape, index=None):
    index = (0,) * len(shape) if index is None else index
    return pl.BlockSpec(shape, lambda *_: index, pipeline_mode=pl.Buffered(1))


def _dot(a, b):
    return jnp.dot(a, b, preferred_element_type=F32)


def _dot_nt(a, b):
    return lax.dot_general(a, b, (((1,), (1,)), ((), ())),
                           preferred_element_type=F32)


def _dot_tn(a, b):
    return lax.dot_general(a, b, (((0,), (0,)), ((), ())),
                           preferred_element_type=F32)


def _rmsnorm(x, gain):
    ms = jnp.mean(x * x, axis=-1, keepdims=True)
    return x * lax.rsqrt(ms + EPS) * gain


def _silu(x):
    return x * jax.nn.sigmoid(x)


def _token_specs(tm, ts, width):
    return (pl.BlockSpec((tm, width), lambda i: (i, 0)),
            pl.BlockSpec((ts, width), lambda i: (i, 0)))


def _token_tiles(xp, xs, tile=None):
    n_prompt, n_sample = xp.shape[0], xs.shape[0]
    tm = math.gcd(n_prompt, TOKEN_TILE if tile is None else tile)
    steps = n_prompt // tm
    ts = n_sample // steps
    assert ts * steps == n_sample and ts % 16 == 0
    return tm, ts, steps


def _load_normed(xp_ref, xs_ref, g_ref, xn_ref):
    tm = xp_ref.shape[0]
    xn_ref[:tm, :] = _rmsnorm(xp_ref[...], g_ref[...]).astype(BF16)
    xn_ref[tm:, :] = _rmsnorm(xs_ref[...], g_ref[...]).astype(BF16)


def _ffn_body(xp, xs, g_ref, wu_ref, wd_ref, op_ref, os_ref, xn_ref, h_ref):
    tm = xp.shape[0]
    d_ff = wd_ref.shape[0]
    xn_ref[:tm, :] = _rmsnorm(xp, g_ref[...]).astype(BF16)
    xn_ref[tm:, :] = _rmsnorm(xs, g_ref[...]).astype(BF16)
    xn = xn_ref[...]
    for c in range(d_ff // FFN_COLS):
        lo = c * FFN_COLS
        a = _dot(xn, wu_ref[:, lo:lo + FFN_COLS])
        b = _dot(xn, wu_ref[:, d_ff + lo:d_ff + lo + FFN_COLS])
        h_ref[:, lo:lo + FFN_COLS] = (_silu(a) * b).astype(BF16)
    y = _dot(h_ref[...], wd_ref[...])
    op_ref[...] = xp + 0.5 * y[:tm]
    os_ref[...] = xs + 0.5 * y[tm:]


def _ffn_kernel(xp_ref, xs_ref, g_ref, wu_ref, wd_ref, op_ref, os_ref, xn_ref, h_ref):
    _ffn_body(xp_ref[...], xs_ref[...], g_ref, wu_ref, wd_ref, op_ref, os_ref, xn_ref, h_ref)


def _mix_ffn_kernel(xp_ref, xs_ref, ap_ref, as_ref, wo_ref, g_ref, wu_ref, wd_ref,
                    op_ref, os_ref, xn_ref, h_ref):
    xp = xp_ref[...] + _dot(ap_ref[...], wo_ref[...])
    xs = xs_ref[...] + _dot(as_ref[...], wo_ref[...])
    _ffn_body(xp, xs, g_ref, wu_ref, wd_ref, op_ref, os_ref, xn_ref, h_ref)


def _ffn(xp, xs, gains, w_up, w_down, layer, which, mix=None):
    d = xp.shape[1]
    d_ff = w_down.shape[2]
    tm, ts, steps = _token_tiles(xp, xs, None if mix is None else MIX_FFN_TILE)
    assert d_ff % FFN_COLS == 0
    xspec = _token_specs(tm, ts, d)
    gain = _resident((None, None, 1, d), (layer, 2 * which, 0, 0))
    weights = [_resident((None, None, d, 2 * d_ff), (layer, which, 0, 0)),
               _resident((None, None, d_ff, d), (layer, which, 0, 0))]
    gains4 = gains.reshape(gains.shape[0], gains.shape[1], 1, d)
    if mix is None:
        body, in_specs, args = _ffn_kernel, [*xspec, gain, *weights], (xp, xs, gains4, w_up, w_down)
    else:
        ap, a_s, w_out, j = mix
        k = ap.shape[1]
        body = _mix_ffn_kernel
        in_specs = [*xspec, *_token_specs(tm, ts, k), _resident((None, k, d), (j, 0, 0)),
                    gain, *weights]
        args = (xp, xs, ap, a_s, w_out, gains4, w_up, w_down)
    return pl.pallas_call(
        body,
        out_shape=(jax.ShapeDtypeStruct(xp.shape, F32), jax.ShapeDtypeStruct(xs.shape, F32)),
        grid=(steps,),
        in_specs=in_specs,
        out_specs=xspec,
        scratch_shapes=[pltpu.VMEM((tm + ts, d), BF16), pltpu.VMEM((tm + ts, d_ff), BF16)],
        compiler_params=_cparams(1),
        name="ffn" if mix is None else "mix_ffn",
    )(*args)


def _final_norm_kernel(xp_ref, xs_ref, g_ref, op_ref, os_ref):
    op_ref[...] = _rmsnorm(xp_ref[...], g_ref[...])
    os_ref[...] = _rmsnorm(xs_ref[...], g_ref[...])


def _final_norm(xp, xs, gain):
    d = xp.shape[1]
    tm, ts, steps = _token_tiles(xp, xs)
    xspec = _token_specs(tm, ts, d)
    return pl.pallas_call(
        _final_norm_kernel,
        out_shape=(jax.ShapeDtypeStruct(xp.shape, F32), jax.ShapeDtypeStruct(xs.shape, F32)),
        grid=(steps,),
        in_specs=[*xspec, _resident((1, d))],
        out_specs=xspec,
        compiler_params=_cparams(1),
        name="final_norm",
    )(xp, xs, gain.reshape(1, d))


def _ret_proj_kernel(xp_ref, xs_ref, g_ref, w_ref, cp_ref, sp_ref, cs_ref, ss_ref,
                     qp_ref, qs_ref, kp_ref, ks_ref, vp_ref, vs_ref, gp_ref, gs_ref, xn_ref,
                     *, dk, dv, heads):
    tm = xp_ref.shape[0]
    _load_normed(xp_ref, xs_ref, g_ref, xn_ref)
    xn = xn_ref[...]
    cos = jnp.concatenate([cp_ref[...], cs_ref[...]], axis=0)
    sin = jnp.concatenate([sp_ref[...], ss_ref[...]], axis=0)
    half = dk // 2
    qk = heads * dk
    k_scale = dk ** -0.5

    def put(p_ref, s_ref, lo, val):
        val = val.astype(BF16)
        p_ref[:, lo:lo + val.shape[1]] = val[:tm]
        s_ref[:, lo:lo + val.shape[1]] = val[tm:]

    for h in range(heads):
        for base, p_ref, s_ref, scale in ((0, qp_ref, qs_ref, None), (qk, kp_ref, ks_ref, k_scale)):
            t = _dot(xn, w_ref[:, base + h * dk:base + (h + 1) * dk])
            x1, x2 = t[:, :half], t[:, half:]
            r1 = x1 * cos - x2 * sin
            r2 = x1 * sin + x2 * cos
            if scale is not None:
                r1, r2 = r1 * scale, r2 * scale
            put(p_ref, s_ref, h * dk, r1)
            put(p_ref, s_ref, h * dk + half, r2)
    for h in range(heads):
        lo = 2 * qk + h * dv
        put(vp_ref, vs_ref, h * dv, _dot(xn, w_ref[:, lo:lo + dv]))
        lo = 2 * qk + heads * dv + h * dv
        put(gp_ref, gs_ref, h * dv, _silu(_dot(xn, w_ref[:, lo:lo + dv])))


def _rope_tables(pos, half):
    inv_freq = ROPE_BASE ** (-jnp.arange(half, dtype=F32) / half)
    ang = pos[:, None] * inv_freq[None, :]
    return jnp.cos(ang), jnp.sin(ang)


def _ret_proj(xp, xs, gains, w_in, layer, j, seq_len, dec_len):
    d = xp.shape[1]
    heads = RET_HEADS
    dk = d // heads
    dv = 2 * dk
    half = dk // 2
    tm, ts, steps = _token_tiles(xp, xs)
    assert seq_len % tm == 0 and ts % dec_len == 0
    tiles_per_seq = seq_len // tm
    cos_p, sin_p = _rope_tables(jnp.arange(seq_len, dtype=F32), half)
    cos_s, sin_s = _rope_tables(
        jnp.tile(PAST_LEN + jnp.arange(dec_len, dtype=F32), ts // dec_len), half)
    tab_p = pl.BlockSpec((tm, half), lambda i: (i % tiles_per_seq, 0))
    widths = (heads * dk, heads * dk, heads * dv, heads * dv)
    out_shape, out_specs = [], []
    for w in widths:
        out_shape += [jax.ShapeDtypeStruct((xp.shape[0], w), BF16),
                      jax.ShapeDtypeStruct((xs.shape[0], w), BF16)]
        out_specs += list(_token_specs(tm, ts, w))
    return pl.pallas_call(
        functools.partial(_ret_proj_kernel, dk=dk, dv=dv, heads=heads),
        out_shape=tuple(out_shape),
        grid=(steps,),
        in_specs=[*_token_specs(tm, ts, d),
                  _resident((None, None, 1, d), (layer, 1, 0, 0)),
                  _resident((None,) + w_in.shape[1:], (j, 0, 0)),
                  tab_p, tab_p, _resident((ts, half)), _resident((ts, half))],
        out_specs=tuple(out_specs),
        scratch_shapes=[pltpu.VMEM((tm + ts, d), BF16)],
        compiler_params=_cparams(1),
        name="ret_proj",
    )(xp, xs, gains.reshape(gains.shape[0], gains.shape[1], 1, d), w_in,
      cos_p, sin_p, cos_s, sin_s)


def _head_norm_gate(o, gn, sg):
    ms = jnp.mean(o * o, axis=-1, keepdims=True)
    return (o * lax.rsqrt(ms + EPS) * gn * sg.astype(F32)).astype(BF16)


def _ret_core_prompt_kernel(q_ref, k_ref, v_ref, sg_ref, gn_ref, dm_ref, qd_ref, kd_ref,
                            cd_ref, *rest, n_chunks, chunk, hps, dk, dv):
    og_ref, s_ref = rest[-2:]
    h0 = pl.program_id(1) * hps
    for hh in range(hps):
        s_ref[0, hh] = jnp.zeros((dk, dv), F32)

    def body(c, carry):
        rows = pl.ds(pl.multiple_of(c * chunk, chunk), chunk)
        for hh in range(hps):
            qc = q_ref[rows, hh * dk:(hh + 1) * dk]
            kc = k_ref[rows, hh * dk:(hh + 1) * dk]
            vc = v_ref[rows, hh * dv:(hh + 1) * dv]
            s_old = s_ref[0, hh]
            scores = _dot_nt(qc, kc) * dm_ref[hh]
            o = _dot(scores.astype(BF16), vc) + _dot(qc, s_old.astype(BF16)) * qd_ref[hh]
            kdec = (kc.astype(F32) * kd_ref[hh]).astype(BF16)
            s_ref[0, hh] = s_old * cd_ref[h0 + hh] + _dot_tn(kdec, vc)
            og_ref[rows, hh * dv:(hh + 1) * dv] = _head_norm_gate(
                o, gn_ref[hh], sg_ref[rows, hh * dv:(hh + 1) * dv])
        return carry

    lax.fori_loop(0, n_chunks, body, 0, unroll=2)


def _ret_decay_tables(chunk):
    heads = RET_HEADS
    log_gamma = jnp.log(1.0 - jnp.power(2.0, -5.0 - jnp.arange(heads, dtype=F32)))
    idx = jnp.arange(chunk, dtype=F32)
    diff = idx[:, None] - idx[None, :]
    lg = log_gamma[:, None, None]
    decay_mat = jnp.where(diff[None] >= 0, jnp.exp(diff[None] * lg), 0.0)
    q_decay = jnp.exp((idx[None, :] + 1.0) * log_gamma[:, None])
    k_decay = jnp.exp((chunk - 1.0 - idx[None, :]) * log_gamma[:, None])
    chunk_decay = jnp.exp(chunk * log_gamma)
    return decay_mat, q_decay, k_decay, chunk_decay


def _state_out(prev, shape, j, block, index_map):
    spec = pl.BlockSpec((None,) + block, lambda *ids: (j,) + index_map(*ids))
    extra_in = [] if prev is None else [prev]
    extra_spec = [] if prev is None else [pl.BlockSpec(memory_space=pl.ANY)]
    return spec, jax.ShapeDtypeStruct(shape, F32), extra_in, extra_spec


def _ret_core_prompt(q, k, v, sg, gn, j, n_layers, prev_state, n_seqs, seq_len):
    n = q.shape[0]
    heads = RET_HEADS
    hps = RET_HEADS_PER_STEP
    dk = q.shape[1] // heads
    dv = v.shape[1] // heads
    chunk = math.gcd(seq_len, RET_CHUNK)
    dm, qd, kd, cd = _ret_decay_tables(chunk)
    per_step = lambda w: pl.BlockSpec((seq_len, hps * w), lambda b, h: (b, h))
    tab = lambda shape: pl.BlockSpec((hps,) + shape, lambda b, h: (h, 0, 0))
    s_spec, s_shape, extra_in, extra_spec = _state_out(
        prev_state, (n_layers, n_seqs, heads, dk, dv), j, (1, hps, dk, dv),
        lambda b, h: (b, h, 0, 0))
    n_in = 9 + len(extra_in)
    return pl.pallas_call(
        functools.partial(_ret_core_prompt_kernel, n_chunks=seq_len // chunk, chunk=chunk,
                          hps=hps, dk=dk, dv=dv),
        out_shape=(jax.ShapeDtypeStruct((n, heads * dv), BF16), s_shape),
        grid=(n_seqs, heads // hps),
        in_specs=[per_step(dk), per_step(dk), per_step(dv), per_step(dv),
                  pl.BlockSpec((None, hps, 1, dv), lambda b, h: (j, h, 0, 0)),
                  tab((chunk, chunk)), tab((chunk, 1)), tab((chunk, 1)),
                  pl.BlockSpec(memory_space=pltpu.SMEM)] + extra_spec,
        out_specs=(per_step(dv), s_spec),
        input_output_aliases={n_in - 1: 1} if extra_in else {},
        compiler_params=_cparams(2),
        name="ret_core_prompt",
    )(q, k, v, sg, gn.reshape(gn.shape[0], heads, 1, dv), dm, qd[:, :, None], kd[:, :, None],
      cd, *extra_in)


def _ret_core_sample_kernel(q_ref, k_ref, v_ref, sg_ref, gn_ref, dm_ref, qd_ref, kd_ref,
                            cd_ref, s0_ref, *rest, seqs, dec_len):
    og_ref, s_ref = rest[-2:]
    h = pl.program_id(1)
    rows = seqs * dec_len
    qb = q_ref[...]
    vb = v_ref[...]
    kdec = k_ref[...].astype(F32) * kd_ref[0]
    cd = cd_ref[h]
    row_seq = lax.broadcasted_iota(jnp.int32, (rows, 1), 0) // dec_len
    scores = _dot_nt(qb, k_ref[...]) * dm_ref[0]
    o = _dot(scores.astype(BF16), vb)
    inter = jnp.zeros(o.shape, F32)
    for i in range(seqs):
        s_old = s0_ref[i, 0]
        mine = row_seq == i
        inter = jnp.where(mine, _dot(qb, s_old.astype(BF16)), inter)
        k_i = jnp.where(mine, kdec, 0.0).astype(BF16)
        s_ref[i, 0] = s_old * cd + _dot_tn(k_i, vb)
    o = o + inter * qd_ref[0]
    og_ref[...] = _head_norm_gate(o, gn_ref[0], sg_ref[...])


def _ret_core_sample(q, k, v, sg, gn, s0_all, j, prev_state, dec_len):
    heads = RET_HEADS
    n_layers, n_seqs = s0_all.shape[:2]
    dk, dv = s0_all.shape[3], s0_all.shape[4]
    seqs = RET_SAMPLE_SEQS
    rows = seqs * dec_len
    assert n_seqs % seqs == 0
    chunk = math.gcd(dec_len, RET_CHUNK)
    assert chunk == dec_len
    dm, qd, kd, cd = _ret_decay_tables(chunk)
    seq_id = np.arange(rows) // dec_len
    same = jnp.asarray(seq_id[:, None] == seq_id[None, :])
    dm_blk = jnp.where(same[None], jnp.tile(dm, (1, seqs, seqs)), 0.0)
    qd_blk = jnp.tile(qd, (1, seqs))[:, :, None]
    kd_blk = jnp.tile(kd, (1, seqs))[:, :, None]
    per_head = lambda w: pl.BlockSpec((rows, w), lambda i, h: (i, h))
    tab = lambda shape: pl.BlockSpec((1,) + shape, lambda i, h: (h, 0, 0))
    state_in = pl.BlockSpec((None, seqs, 1, dk, dv), lambda i, h: (j, i, h, 0, 0))
    s_spec, s_shape, extra_in, extra_spec = _state_out(
        prev_state, s0_all.shape, j, (seqs, 1, dk, dv), lambda i, h: (i, h, 0, 0))
    n_in = 10 + len(extra_in)
    return pl.pallas_call(
        functools.partial(_ret_core_sample_kernel, seqs=seqs, dec_len=dec_len),
        out_shape=(jax.ShapeDtypeStruct((q.shape[0], heads * dv), BF16), s_shape),
        grid=(n_seqs // seqs, heads),
        in_specs=[per_head(dk), per_head(dk), per_head(dv), per_head(dv),
                  pl.BlockSpec((None, 1, 1, dv), lambda i, h: (j, h, 0, 0)),
                  tab((rows, rows)), tab((rows, 1)), tab((rows, 1)),
                  pl.BlockSpec(memory_space=pltpu.SMEM), state_in] + extra_spec,
        out_specs=(per_head(dv), s_spec),
        input_output_aliases={n_in - 1: 1} if extra_in else {},
        compiler_params=_cparams(2),
        name="ret_core_sample",
    )(q, k, v, sg, gn.reshape(gn.shape[0], heads, 1, dv), dm_blk, qd_blk, kd_blk, cd,
      s0_all, *extra_in)


def _hg_proj_kernel(xp_ref, xs_ref, g_ref, w_ref, lbl_ref,
                    qp_ref, qs_ref, kp_ref, ks_ref, gp_ref, gs_ref, vp_ref, vs_ref,
                    sp_ref, ss_ref, xn_ref, *, layer, cols):
    tm, d = xp_ref.shape
    _load_normed(xp_ref, xs_ref, g_ref, xn_ref)
    xn = xn_ref[...]
    logits = lbl_ref[...]
    e = jnp.exp(logits - jnp.max(logits, axis=0, keepdims=True))
    sm = e / jnp.sum(e, axis=0, keepdims=True)
    cum = sm[0:1]
    for i in range(1, layer + 1):
        cum = cum + sm[i:i + 1]
    lb = cum - sm[0:1]
    log_lb = jnp.log(lb)
    log_1m_lb = jnp.log1p(-lb)

    def put(p_ref, s_ref, lo, val):
        p_ref[:, lo:lo + cols] = val[:tm].astype(p_ref.dtype)
        s_ref[:, lo:lo + cols] = val[tm:].astype(s_ref.dtype)

    for c in range(d // cols):
        lo = c * cols
        put(qp_ref, qs_ref, lo, _silu(_dot(xn, w_ref[:, lo:lo + cols])))
        z = _dot(xn, w_ref[:, d + lo:d + lo + cols])
        ez = jnp.exp(-jnp.abs(z))
        log_sig = jnp.minimum(z, 0.0) - jnp.log1p(ez)
        a = log_lb[:, lo:lo + cols]
        b = log_1m_lb[:, lo:lo + cols] + log_sig
        delta = a - b
        g = jnp.where(delta != delta, a + b,
                      jnp.maximum(a, b) + jnp.log1p(jnp.exp(-jnp.abs(delta))))
        put(gp_ref, gs_ref, lo, g)
        sig_neg = jnp.where(z >= 0.0, ez, 1.0) / (1.0 + ez)
        put(kp_ref, ks_ref, lo, (1.0 - lb[:, lo:lo + cols]) * sig_neg)
        put(vp_ref, vs_ref, lo, _dot(xn, w_ref[:, 2 * d + lo:2 * d + lo + cols]))
        put(sp_ref, ss_ref, lo, _silu(_dot(xn, w_ref[:, 3 * d + lo:3 * d + lo + cols])))


def _hg_proj(xp, xs, gains, w_in, lb_logits, layer, j):
    d = xp.shape[1]
    tm, ts, steps = _token_tiles(xp, xs)
    out_shape, out_specs = [], []
    for dt in (BF16, BF16, F32, BF16, BF16):
        out_shape += [jax.ShapeDtypeStruct(xp.shape, dt), jax.ShapeDtypeStruct(xs.shape, dt)]
        out_specs += list(_token_specs(tm, ts, d))
    return pl.pallas_call(
        functools.partial(_hg_proj_kernel, layer=j, cols=2 * LANE),
        out_shape=tuple(out_shape),
        grid=(steps,),
        in_specs=[*_token_specs(tm, ts, d),
                  _resident((None, None, 1, d), (layer, 1, 0, 0)),
                  _resident((None,) + w_in.shape[1:], (j, 0, 0)),
                  _resident(lb_logits.shape)],
        out_specs=tuple(out_specs),
        scratch_shapes=[pltpu.VMEM((tm + ts, d), BF16)],
        compiler_params=_cparams(1),
        name="hg_proj",
    )(xp, xs, gains.reshape(gains.shape[0], gains.shape[1], 1, d), w_in, lb_logits)


def _hg_tables(chunk, block):
    n_levels = int(math.log2(block))
    assert 2 ** n_levels == block and chunk % block == 0
    t = np.arange(chunk)[:, None]
    s = np.arange(chunk)[None, :]
    same_run = (t // block) == (s // block)
    sums = np.concatenate([same_run & (s <= t), same_run & (s > t)], axis=0)
    masks = [t == s]
    for l in range(n_levels):
        m = 2 ** l
        mid = (t // (2 * m)) * (2 * m) + m
        masks.append(((t // (2 * m)) == (s // (2 * m))) & (t >= mid) & (s < mid))
    sums = jnp.asarray(sums.astype(np.float32), dtype=BF16)
    masks = jnp.asarray(np.stack(masks).astype(np.float32))
    return sums, masks, n_levels


def _level_exponents(b, g2, b_ref, block):
    c = b.shape[0]
    row = lax.broadcasted_iota(jnp.int32, (c, 1), 0)
    out = []
    n_levels = int(math.log2(block))
    for l in range(n_levels):
        m = 2 ** l
        if m == 1:
            out.append(jnp.where(row % 2 == 1, g2, 0.0))
        elif m == 2:
            nxt = pltpu.roll(g2, c - 1, 0)
            prv = pltpu.roll(g2, 1, 0)
            r4 = row % 4
            out.append(jnp.where(r4 == 0, nxt,
                                 jnp.where(r4 == 2, g2,
                                           jnp.where(r4 == 3, g2 + prv, 0.0))))
        elif 2 * m == SUBLANE:
            pieces = []
            for lo in range(0, c, SUBLANE):
                diff = b[lo:lo + SUBLANE] - b_ref[pl.ds(lo + m - 1, 1), :]
                pieces.append(jnp.where(row[:SUBLANE] >= m, diff, -diff))
            out.append(jnp.concatenate(pieces, axis=0))
        else:
            pieces = []
            for lo in range(0, c, 2 * m):
                ref_row = b_ref[pl.ds(lo + m - 1, 1), :]
                pieces.append(ref_row - b[lo:lo + m])
                pieces.append(b[lo + m:lo + 2 * m] - ref_row)
            out.append(jnp.concatenate(pieces, axis=0))
    return out


def _hg_chunk(q, k, v, g, sums_ref, masks_ref, b_ref, block):
    c = q.shape[0]
    g2 = g * LOG2E
    g_hi = g2.astype(BF16)
    g_lo = (g2 - g_hi.astype(F32)).astype(BF16)
    g_split = jnp.concatenate([g_hi, g_lo], axis=1)
    if block == c:
        b = _dot(sums_ref[:c, :], g_split)
        b = b[:, :LANE] + b[:, LANE:]
        b_rest = b[c - 1:c, :] - b
    else:
        both = _dot(sums_ref[...], g_split)
        both = both[:, :LANE] + both[:, LANE:]
        b, b_rest = both[:c], both[c:]
    b_ref[...] = b
    qf = q.astype(F32)
    kf = k.astype(F32)
    scores = _dot_nt(q, k) * masks_ref[0]
    for l, nd in enumerate(_level_exponents(b, g2, b_ref, block)):
        e = jnp.exp2(nd)
        s = _dot_nt((qf * e).astype(BF16), (kf * e).astype(BF16))
        scores = scores + s * masks_ref[1 + l]
    o_intra = _dot(scores.astype(BF16), v)
    eb = jnp.exp2(b)
    q_dec = (qf * eb).astype(BF16)
    k_dec = (kf * jnp.exp2(b_rest)).astype(BF16)
    return o_intra, q_dec, k_dec, eb


def _hg_core_prompt_kernel(q_ref, k_ref, v_ref, g_ref, sg_ref, gn_ref, sums_ref, masks_ref,
                           *rest, n_chunks, chunk, hps, dh):
    og_ref, s_ref, st_ref, b_ref = rest[-4:]
    st_ref[...] = jnp.zeros(st_ref.shape, F32)

    def body(c, carry):
        rows = pl.ds(pl.multiple_of(c * chunk, chunk), chunk)
        for hh in range(hps):
            cols = slice(hh * dh, (hh + 1) * dh)
            vc = v_ref[rows, cols]
            o_intra, q_dec, k_dec, eb = _hg_chunk(
                q_ref[rows, cols], k_ref[rows, cols], vc, g_ref[rows, cols],
                sums_ref, masks_ref, b_ref.at[hh], chunk)
            st_old = st_ref[hh]
            o = o_intra + _dot_nt(q_dec, st_old.astype(BF16))
            st_ref[hh] = st_old * eb[chunk - 1:chunk, :] + _dot_tn(vc, k_dec)
            og_ref[rows, cols] = _head_norm_gate(o, gn_ref[hh], sg_ref[rows, cols])
        return carry

    lax.fori_loop(0, n_chunks, body, 0)
    for hh in range(hps):
        s_ref[0, hh] = st_ref[hh].T


def _hg_core_prompt(q, k, v, g, sg, gn, j, n_layers, prev_state, n_seqs, seq_len):
    n, d = q.shape
    heads = HG_HEADS
    hps = HG_HEADS_PER_STEP
    dh = d // heads
    chunk = math.gcd(seq_len, HG_CHUNK)
    sums, masks, _ = _hg_tables(chunk, chunk)
    per_step = pl.BlockSpec((seq_len, hps * dh), lambda b, h: (b, h))
    s_spec, s_shape, extra_in, extra_spec = _state_out(
        prev_state, (n_layers, n_seqs, heads, dh, dh), j, (1, hps, dh, dh),
        lambda b, h: (b, h, 0, 0))
    n_in = 8 + len(extra_in)
    return pl.pallas_call(
        functools.partial(_hg_core_prompt_kernel, n_chunks=seq_len // chunk, chunk=chunk,
                          hps=hps, dh=dh),
        out_shape=(jax.ShapeDtypeStruct((n, d), BF16), s_shape),
        grid=(n_seqs, heads // hps),
        in_specs=[per_step, per_step, per_step, per_step, per_step,
                  pl.BlockSpec((None, hps, 1, dh), lambda b, h: (j, h, 0, 0)),
                  _resident(sums.shape), _resident(masks.shape)] + extra_spec,
        out_specs=(per_step, s_spec),
        input_output_aliases={n_in - 1: 1} if extra_in else {},
        scratch_shapes=[pltpu.VMEM((hps, dh, dh), F32), pltpu.VMEM((hps, chunk, dh), F32)],
        compiler_params=_cparams(2),
        name="hg_core_prompt",
    )(q, k, v, g, sg, gn.reshape(gn.shape[0], heads, 1, dh), sums, masks, *extra_in)


def _hg_core_sample_kernel(q_ref, k_ref, v_ref, g_ref, sg_ref, gn_ref, sums_ref, masks_ref,
                           s0_ref, *rest, seqs, dec_len):
    og_ref, s_ref, b_ref = rest[-3:]
    rows = seqs * dec_len
    vb = v_ref[...]
    o, q_dec, k_dec, eb = _hg_chunk(
        q_ref[...], k_ref[...], vb, g_ref[...], sums_ref, masks_ref, b_ref, dec_len)
    eb_t = eb.T
    k_dec_t = k_dec.astype(F32).T
    row_seq = lax.broadcasted_iota(jnp.int32, (rows, 1), 0) // dec_len
    col_seq = lax.broadcasted_iota(jnp.int32, (1, rows), 1) // dec_len
    inter = jnp.zeros(o.shape, F32)
    for i in range(seqs):
        s_old = s0_ref[i, 0]
        inter = jnp.where(row_seq == i, _dot(q_dec, s_old.astype(BF16)), inter)
        k_i = jnp.where(col_seq == i, k_dec_t, 0.0).astype(BF16)
        last = (i + 1) * dec_len - 1
        s_ref[i, 0] = s_old * eb_t[:, last:last + 1] + _dot(k_i, vb)
    og_ref[...] = _head_norm_gate(o + inter, gn_ref[0], sg_ref[...])


def _hg_core_sample(q, k, v, g, sg, gn, s0_all, j, prev_state, dec_len):
    heads = HG_HEADS
    n_layers, n_seqs = s0_all.shape[:2]
    dh = s0_all.shape[3]
    rows = math.gcd(n_seqs * dec_len, HG_CHUNK)
    seqs = rows // dec_len
    assert rows % dec_len == 0
    sums, masks, _ = _hg_tables(rows, dec_len)
    per_head = pl.BlockSpec((rows, dh), lambda i, h: (i, h))
    state_in = pl.BlockSpec((None, seqs, 1, dh, dh), lambda i, h: (j, i, h, 0, 0))
    s_spec, s_shape, extra_in, extra_spec = _state_out(
        prev_state, s0_all.shape, j, (seqs, 1, dh, dh), lambda i, h: (i, h, 0, 0))
    n_in = 9 + len(extra_in)
    return pl.pallas_call(
        functools.partial(_hg_core_sample_kernel, seqs=seqs, dec_len=dec_len),
        out_shape=(jax.ShapeDtypeStruct(q.shape, BF16), s_shape),
        grid=(n_seqs // seqs, heads),
        in_specs=[per_head, per_head, per_head, per_head, per_head,
                  pl.BlockSpec((None, 1, 1, dh), lambda i, h: (j, h, 0, 0)),
                  _resident(sums.shape), _resident(masks.shape), state_in] + extra_spec,
        out_specs=(per_head, s_spec),
        input_output_aliases={n_in - 1: 1} if extra_in else {},
        scratch_shapes=[pltpu.VMEM((rows, dh), F32)],
        compiler_params=_cparams(2),
        name="hg_core_sample",
    )(q, k, v, g, sg, gn.reshape(gn.shape[0], heads, 1, dh), sums, masks, s0_all, *extra_in)


def kernel(x_prompt, x_sample, state_ret, state_hgrn, norm_gain, ffn_w_up, ffn_w_down,
           ret_w_in, ret_norm, ret_w_out, hg_w_in, hg_lb_logits, hg_norm, hg_w_out, final_norm):
    n_seqs, seq_len, d = x_prompt.shape
    dec_seqs, dec_len, _ = x_sample.shape
    depth = norm_gain.shape[0]
    n_ret, n_hg = state_ret.shape[0], state_hgrn.shape[0]

    xp = x_prompt.reshape(n_seqs * seq_len, d)
    xs = x_sample.reshape(dec_seqs * dec_len, d)
    w_up, w_down = ffn_w_up.astype(BF16), ffn_w_down.astype(BF16)
    ret_in, ret_out = ret_w_in.astype(BF16), ret_w_out.astype(BF16)
    hg_in, hg_out = hg_w_in.astype(BF16), hg_w_out.astype(BF16)

    ret_p = ret_s = hg_p = hg_s = None
    mix = None
    for layer in range(depth):
        xp, xs = _ffn(xp, xs, norm_gain, w_up, w_down, layer, 0, mix)
        j = layer // 2
        if layer % 2 == 0:
            qp, qs, kp, ks, vp, vs, gp, gs = _ret_proj(
                xp, xs, norm_gain, ret_in, layer, j, seq_len, dec_len)
            ap, ret_p = _ret_core_prompt(qp, kp, vp, gp, ret_norm, j, n_ret, ret_p,
                                         n_seqs, seq_len)
            a_s, ret_s = _ret_core_sample(qs, ks, vs, gs, ret_norm, state_ret, j, ret_s, dec_len)
            mix = (ap, a_s, ret_out, j)
        else:
            qp, qs, kp, ks, gp, gs, vp, vs, sp, ss = _hg_proj(
                xp, xs, norm_gain, hg_in, hg_lb_logits, layer, j)
            ap, hg_p = _hg_core_prompt(qp, kp, vp, gp, sp, hg_norm, j, n_hg, hg_p,
                                       n_seqs, seq_len)
            a_s, hg_s = _hg_core_sample(qs, ks, vs, gs, ss, hg_norm, state_hgrn, j, hg_s,
                                        dec_len)
            mix = (ap, a_s, hg_out, j)
        xp, xs = _ffn(xp, xs, norm_gain, w_up, w_down, layer, 1, mix)
        mix = None
    yp, ys = _final_norm(xp, xs, final_norm)
    return (yp.reshape(n_seqs, seq_len, d), ys.reshape(dec_seqs, dec_len, d),
            ret_p, ret_s, hg_p, hg_s)
```

```python
import functools
import math

import jax
import jax.numpy as jnp
import numpy as np
from jax import lax
from jax.experimental import pallas as pl
from jax.experimental.pallas import tpu as pltpu

F32 = jnp.float32
BF16 = jnp.bfloat16

EPS = 1e-6
ROPE_BASE = 10000.0
PAST_LEN = 16384
RET_HEADS = 4
RET_CHUNK = 128
RET_HEADS_PER_STEP = 2
RET_CHUNKS_PER_ITER = 2
RET_SAMPLE_SEQS = 8
HG_HEADS = 8
HG_CHUNK = 128
HG_HEADS_PER_STEP = 4
HG_FAST_BLOCK = 32
HG_FAST_LIMIT = 120.0
TOKEN_TILE = 1024
MIX_FFN_TILE = 512
FFN_COLS = 256
LANE = 128
SUBLANE = 8
VMEM_LIMIT = 56 * 1024 * 1024
LOG2E = 1.4426950408889634


def _cparams(n_axes):
    return pltpu.CompilerParams(
        dimension_semantics=("arbitrary",) * n_axes,
        vmem_limit_bytes=VMEM_LIMIT)


def _resident(shape, index=None):
    index = (0,) * len(shape) if index is None else index
    return pl.BlockSpec(shape, lambda *_: index, pipeline_mode=pl.Buffered(1))


def _dot(a, b):
    return jnp.dot(a, b, preferred_element_type=F32)


def _dot_nt(a, b):
    return lax.dot_general(a, b, (((1,), (1,)), ((), ())),
                           preferred_element_type=F32)


def _dot_tn(a, b):
    return lax.dot_general(a, b, (((0,), (0,)), ((), ())),
                           preferred_element_type=F32)


def _rmsnorm(x, gain):
    ms = jnp.mean(x * x, axis=-1, keepdims=True)
    return x * lax.rsqrt(ms + EPS) * gain


def _exp_neg(x):
    return jnp.exp2(x * (-LOG2E))


def _silu(x):
    return x * (1.0 / (1.0 + _exp_neg(x)))


def _run_staged(gens):
    results = [None] * len(gens)
    live = list(range(len(gens)))
    while live:
        still = []
        for i in live:
            try:
                next(gens[i])
                still.append(i)
            except StopIteration as stop:
                results[i] = stop.value
        live = still
    return results


def _token_specs(tm, ts, width):
    return (pl.BlockSpec((tm, width), lambda i: (i, 0)),
            pl.BlockSpec((ts, width), lambda i: (i, 0)))


def _token_tiles(xp, xs, tile=None):
    n_prompt, n_sample = xp.shape[0], xs.shape[0]
    tm = math.gcd(n_prompt, TOKEN_TILE if tile is None else tile)
    steps = n_prompt // tm
    ts = n_sample // steps
    assert ts * steps == n_sample and ts % 16 == 0
    return tm, ts, steps


def _load_normed(xp_ref, xs_ref, g_ref, xn_ref):
    tm = xp_ref.shape[0]
    xn_ref[:tm, :] = _rmsnorm(xp_ref[...], g_ref[...]).astype(BF16)
    xn_ref[tm:, :] = _rmsnorm(xs_ref[...], g_ref[...]).astype(BF16)


def _ffn_body(xp, xs, g_ref, wu_ref, wd_ref, op_ref, os_ref, xn_ref, h_ref, fg_ref=None):
    tm = xp.shape[0]
    d_ff = wd_ref.shape[0]
    xn_ref[:tm, :] = _rmsnorm(xp, g_ref[...]).astype(BF16)
    xn_ref[tm:, :] = _rmsnorm(xs, g_ref[...]).astype(BF16)
    xn = xn_ref[...]
    for c in range(d_ff // FFN_COLS):
        lo = c * FFN_COLS
        a = _dot(xn, wu_ref[:, lo:lo + FFN_COLS])
        b = _dot(xn, wu_ref[:, d_ff + lo:d_ff + lo + FFN_COLS])
        h_ref[:, lo:lo + FFN_COLS] = (_silu(a) * b).astype(BF16)
    y = _dot(h_ref[...], wd_ref[...])
    yp = xp + 0.5 * y[:tm]
    ys = xs + 0.5 * y[tm:]
    if fg_ref is not None:
        yp, ys = _rmsnorm(yp, fg_ref[...]), _rmsnorm(ys, fg_ref[...])
    op_ref[...] = yp
    os_ref[...] = ys


def _ffn_kernel(xp_ref, xs_ref, g_ref, wu_ref, wd_ref, op_ref, os_ref, xn_ref, h_ref):
    _ffn_body(xp_ref[...], xs_ref[...], g_ref, wu_ref, wd_ref, op_ref, os_ref, xn_ref, h_ref)


def _mix_ffn_kernel(xp_ref, xs_ref, ap_ref, as_ref, wo_ref, g_ref, wu_ref, wd_ref, *rest):
    fg_ref = rest[0] if len(rest) == 5 else None
    xp = xp_ref[...] + _dot(ap_ref[...], wo_ref[...])
    xs = xs_ref[...] + _dot(as_ref[...], wo_ref[...])
    _ffn_body(xp, xs, g_ref, wu_ref, wd_ref, *rest[-4:], fg_ref=fg_ref)


def _ffn(xp, xs, gains, w_up, w_down, layer, which, mix=None, final_gain=None):
    d = xp.shape[1]
    d_ff = w_down.shape[2]
    tm, ts, steps = _token_tiles(xp, xs, None if mix is None else MIX_FFN_TILE)
    assert d_ff % FFN_COLS == 0
    xspec = _token_specs(tm, ts, d)
    gain = _resident((None, None, 1, d), (layer, 2 * which, 0, 0))
    weights = [_resident((None, None, d, 2 * d_ff), (layer, which, 0, 0)),
               _resident((None, None, d_ff, d), (layer, which, 0, 0))]
    gains4 = gains.reshape(gains.shape[0], gains.shape[1], 1, d)
    if mix is None:
        body, in_specs, args = _ffn_kernel, [*xspec, gain, *weights], (xp, xs, gains4, w_up, w_down)
    else:
        ap, a_s, w_out, j = mix
        k = ap.shape[1]
        body = _mix_ffn_kernel
        in_specs = [*xspec, *_token_specs(tm, ts, k), _resident((None, k, d), (j, 0, 0)),
                    gain, *weights]
        args = (xp, xs, ap, a_s, w_out, gains4, w_up, w_down)
        if final_gain is not None:
            in_specs.append(_resident((1, d)))
            args += (final_gain.reshape(1, d),)
    return pl.pallas_call(
        body,
        out_shape=(jax.ShapeDtypeStruct(xp.shape, F32), jax.ShapeDtypeStruct(xs.shape, F32)),
        grid=(steps,),
        in_specs=in_specs,
        out_specs=xspec,
        scratch_shapes=[pltpu.VMEM((tm + ts, d), BF16), pltpu.VMEM((tm + ts, d_ff), BF16)],
        compiler_params=_cparams(1),
        name="ffn" if mix is None else "mix_ffn",
    )(*args)


def _ret_proj_kernel(xp_ref, xs_ref, g_ref, w_ref, cp_ref, sp_ref, cs_ref, ss_ref,
                     qp_ref, qs_ref, kp_ref, ks_ref, vp_ref, vs_ref, gp_ref, gs_ref, xn_ref,
                     *, dk, dv, heads):
    tm = xp_ref.shape[0]
    _load_normed(xp_ref, xs_ref, g_ref, xn_ref)
    xn = xn_ref[...]
    cos = jnp.concatenate([cp_ref[...], cs_ref[...]], axis=0)
    sin = jnp.concatenate([sp_ref[...], ss_ref[...]], axis=0)
    half = dk // 2
    qk = heads * dk
    k_scale = dk ** -0.5

    def put(p_ref, s_ref, lo, val):
        val = val.astype(BF16)
        p_ref[:, lo:lo + val.shape[1]] = val[:tm]
        s_ref[:, lo:lo + val.shape[1]] = val[tm:]

    for h in range(heads):
        for base, p_ref, s_ref, scale in ((0, qp_ref, qs_ref, None), (qk, kp_ref, ks_ref, k_scale)):
            t = _dot(xn, w_ref[:, base + h * dk:base + (h + 1) * dk])
            x1, x2 = t[:, :half], t[:, half:]
            r1 = x1 * cos - x2 * sin
            r2 = x1 * sin + x2 * cos
            if scale is not None:
                r1, r2 = r1 * scale, r2 * scale
            put(p_ref, s_ref, h * dk, r1)
            put(p_ref, s_ref, h * dk + half, r2)
    for h in range(heads):
        lo = 2 * qk + h * dv
        put(vp_ref, vs_ref, h * dv, _dot(xn, w_ref[:, lo:lo + dv]))
        lo = 2 * qk + heads * dv + h * dv
        put(gp_ref, gs_ref, h * dv, _silu(_dot(xn, w_ref[:, lo:lo + dv])))


def _rope_tables(pos, half):
    inv_freq = ROPE_BASE ** (-jnp.arange(half, dtype=F32) / half)
    ang = pos[:, None] * inv_freq[None, :]
    return jnp.cos(ang), jnp.sin(ang)


def _ret_proj(xp, xs, gains, w_in, layer, j, seq_len, dec_len):
    d = xp.shape[1]
    heads = RET_HEADS
    dk = d // heads
    dv = 2 * dk
    half = dk // 2
    tm, ts, steps = _token_tiles(xp, xs)
    assert seq_len % tm == 0 and ts % dec_len == 0
    tiles_per_seq = seq_len // tm
    cos_p, sin_p = _rope_tables(jnp.arange(seq_len, dtype=F32), half)
    cos_s, sin_s = _rope_tables(
        jnp.tile(PAST_LEN + jnp.arange(dec_len, dtype=F32), ts // dec_len), half)
    tab_p = pl.BlockSpec((tm, half), lambda i: (i % tiles_per_seq, 0))
    widths = (heads * dk, heads * dk, heads * dv, heads * dv)
    out_shape, out_specs = [], []
    for w in widths:
        out_shape += [jax.ShapeDtypeStruct((xp.shape[0], w), BF16),
                      jax.ShapeDtypeStruct((xs.shape[0], w), BF16)]
        out_specs += list(_token_specs(tm, ts, w))
    return pl.pallas_call(
        functools.partial(_ret_proj_kernel, dk=dk, dv=dv, heads=heads),
        out_shape=tuple(out_shape),
        grid=(steps,),
        in_specs=[*_token_specs(tm, ts, d),
                  _resident((None, None, 1, d), (layer, 1, 0, 0)),
                  _resident((None,) + w_in.shape[1:], (j, 0, 0)),
                  tab_p, tab_p, _resident((ts, half)), _resident((ts, half))],
        out_specs=tuple(out_specs),
        scratch_shapes=[pltpu.VMEM((tm + ts, d), BF16)],
        compiler_params=_cparams(1),
        name="ret_proj",
    )(xp, xs, gains.reshape(gains.shape[0], gains.shape[1], 1, d), w_in,
      cos_p, sin_p, cos_s, sin_s)


def _head_norm_gate(o, gn, sg):
    ms = jnp.mean(o * o, axis=-1, keepdims=True)
    return (o * lax.rsqrt(ms + EPS) * gn * sg.astype(F32)).astype(BF16)


def _ret_core_prompt_kernel(q_ref, k_ref, v_ref, sg_ref, gn_ref, dm_ref, qd_ref, kd_ref,
                            cd_ref, *rest, n_chunks, chunk, hps, dk, dv):
    og_ref, s_ref = rest[-2:]
    h0 = pl.program_id(1) * hps
    for hh in range(hps):
        s_ref[0, hh] = jnp.zeros((dk, dv), F32)

    cpi = RET_CHUNKS_PER_ITER
    assert n_chunks % cpi == 0

    def head_chain(hh, rows):
        qs = [q_ref[r, hh * dk:(hh + 1) * dk] for r in rows]
        ks = [k_ref[r, hh * dk:(hh + 1) * dk] for r in rows]
        vs = [v_ref[r, hh * dv:(hh + 1) * dv] for r in rows]
        sgs = [sg_ref[r, hh * dv:(hh + 1) * dv] for r in rows]
        s = s_ref[0, hh]
        cd = cd_ref[h0 + hh]
        scores = [_dot_nt(qc, kc) for qc, kc in zip(qs, ks)]
        updates = [_dot_tn((kc.astype(F32) * kd_ref[hh]).astype(BF16), vc)
                   for kc, vc in zip(ks, vs)]
        yield
        intra = [_dot((sc * dm_ref[hh]).astype(BF16), vc) for sc, vc in zip(scores, vs)]
        inter = []
        for qc, upd in zip(qs, updates):
            inter.append(_dot(qc, s.astype(BF16)))
            s = s * cd + upd
        yield
        return s, [_head_norm_gate(oi + oc * qd_ref[hh], gn_ref[hh], sgc)
                   for oi, oc, sgc in zip(intra, inter, sgs)]

    def body(it, carry):
        rows = [pl.ds(pl.multiple_of((it * cpi + cc) * chunk, chunk), chunk)
                for cc in range(cpi)]
        results = _run_staged([head_chain(hh, rows) for hh in range(hps)])
        for hh, (s_new, ogs) in enumerate(results):
            s_ref[0, hh] = s_new
            for r, og in zip(rows, ogs):
                og_ref[r, hh * dv:(hh + 1) * dv] = og
        return carry

    lax.fori_loop(0, n_chunks // cpi, body, 0)


def _ret_decay_tables(chunk):
    heads = RET_HEADS
    log_gamma = jnp.log(1.0 - jnp.power(2.0, -5.0 - jnp.arange(heads, dtype=F32)))
    idx = jnp.arange(chunk, dtype=F32)
    diff = idx[:, None] - idx[None, :]
    lg = log_gamma[:, None, None]
    decay_mat = jnp.where(diff[None] >= 0, jnp.exp(diff[None] * lg), 0.0)
    q_decay = jnp.exp((idx[None, :] + 1.0) * log_gamma[:, None])
    k_decay = jnp.exp((chunk - 1.0 - idx[None, :]) * log_gamma[:, None])
    chunk_decay = jnp.exp(chunk * log_gamma)
    return decay_mat, q_decay, k_decay, chunk_decay


def _state_out(prev, shape, j, block, index_map):
    spec = pl.BlockSpec((None,) + block, lambda *ids: (j,) + index_map(*ids))
    extra_in = [] if prev is None else [prev]
    extra_spec = [] if prev is None else [pl.BlockSpec(memory_space=pl.ANY)]
    return spec, jax.ShapeDtypeStruct(shape, F32), extra_in, extra_spec


def _ret_core_prompt(q, k, v, sg, gn, j, n_layers, prev_state, n_seqs, seq_len):
    n = q.shape[0]
    heads = RET_HEADS
    hps = RET_HEADS_PER_STEP
    dk = q.shape[1] // heads
    dv = v.shape[1] // heads
    chunk = math.gcd(seq_len, RET_CHUNK)
    dm, qd, kd, cd = _ret_decay_tables(chunk)
    per_step = lambda w: pl.BlockSpec((seq_len, hps * w), lambda b, h: (b, h))
    tab = lambda shape: pl.BlockSpec((hps,) + shape, lambda b, h: (h, 0, 0))
    s_spec, s_shape, extra_in, extra_spec = _state_out(
        prev_state, (n_layers, n_seqs, heads, dk, dv), j, (1, hps, dk, dv),
        lambda b, h: (b, h, 0, 0))
    n_in = 9 + len(extra_in)
    return pl.pallas_call(
        functools.partial(_ret_core_prompt_kernel, n_chunks=seq_len // chunk, chunk=chunk,
                          hps=hps, dk=dk, dv=dv),
        out_shape=(jax.ShapeDtypeStruct((n, heads * dv), BF16), s_shape),
        grid=(n_seqs, heads // hps),
        in_specs=[per_step(dk), per_step(dk), per_step(dv), per_step(dv),
                  pl.BlockSpec((None, hps, 1, dv), lambda b, h: (j, h, 0, 0)),
                  tab((chunk, chunk)), tab((chunk, 1)), tab((chunk, 1)),
                  pl.BlockSpec(memory_space=pltpu.SMEM)] + extra_spec,
        out_specs=(per_step(dv), s_spec),
        input_output_aliases={n_in - 1: 1} if extra_in else {},
        compiler_params=_cparams(2),
        name="ret_core_prompt",
    )(q, k, v, sg, gn.reshape(gn.shape[0], heads, 1, dv), dm, qd[:, :, None], kd[:, :, None],
      cd, *extra_in)


def _ret_core_sample_kernel(q_ref, k_ref, v_ref, sg_ref, gn_ref, dm_ref, qd_ref, kd_ref,
                            cd_ref, s0_ref, *rest, seqs, dec_len):
    og_ref, s_ref = rest[-2:]
    h = pl.program_id(1)
    rows = seqs * dec_len
    qb = q_ref[...]
    vb = v_ref[...]
    kdec = k_ref[...].astype(F32) * kd_ref[0]
    cd = cd_ref[h]
    row_seq = lax.broadcasted_iota(jnp.int32, (rows, 1), 0) // dec_len
    scores = _dot_nt(qb, k_ref[...]) * dm_ref[0]
    o = _dot(scores.astype(BF16), vb)
    inter = jnp.zeros(o.shape, F32)
    for i in range(seqs):
        s_old = s0_ref[i, 0]
        mine = row_seq == i
        inter = jnp.where(mine, _dot(qb, s_old.astype(BF16)), inter)
        k_i = jnp.where(mine, kdec, 0.0).astype(BF16)
        s_ref[i, 0] = s_old * cd + _dot_tn(k_i, vb)
    o = o + inter * qd_ref[0]
    og_ref[...] = _head_norm_gate(o, gn_ref[0], sg_ref[...])


def _ret_core_sample(q, k, v, sg, gn, s0_all, j, prev_state, dec_len):
    heads = RET_HEADS
    n_layers, n_seqs = s0_all.shape[:2]
    dk, dv = s0_all.shape[3], s0_all.shape[4]
    seqs = RET_SAMPLE_SEQS
    rows = seqs * dec_len
    assert n_seqs % seqs == 0
    chunk = math.gcd(dec_len, RET_CHUNK)
    assert chunk == dec_len
    dm, qd, kd, cd = _ret_decay_tables(chunk)
    seq_id = np.arange(rows) // dec_len
    same = jnp.asarray(seq_id[:, None] == seq_id[None, :])
    dm_blk = jnp.where(same[None], jnp.tile(dm, (1, seqs, seqs)), 0.0)
    qd_blk = jnp.tile(qd, (1, seqs))[:, :, None]
    kd_blk = jnp.tile(kd, (1, seqs))[:, :, None]
    per_head = lambda w: pl.BlockSpec((rows, w), lambda i, h: (i, h))
    tab = lambda shape: pl.BlockSpec((1,) + shape, lambda i, h: (h, 0, 0))
    state_in = pl.BlockSpec((None, seqs, 1, dk, dv), lambda i, h: (j, i, h, 0, 0))
    s_spec, s_shape, extra_in, extra_spec = _state_out(
        prev_state, s0_all.shape, j, (seqs, 1, dk, dv), lambda i, h: (i, h, 0, 0))
    n_in = 10 + len(extra_in)
    return pl.pallas_call(
        functools.partial(_ret_core_sample_kernel, seqs=seqs, dec_len=dec_len),
        out_shape=(jax.ShapeDtypeStruct((q.shape[0], heads * dv), BF16), s_shape),
        grid=(n_seqs // seqs, heads),
        in_specs=[per_head(dk), per_head(dk), per_head(dv), per_head(dv),
                  pl.BlockSpec((None, 1, 1, dv), lambda i, h: (j, h, 0, 0)),
                  tab((rows, rows)), tab((rows, 1)), tab((rows, 1)),
                  pl.BlockSpec(memory_space=pltpu.SMEM), state_in] + extra_spec,
        out_specs=(per_head(dv), s_spec),
        input_output_aliases={n_in - 1: 1} if extra_in else {},
        compiler_params=_cparams(2),
        name="ret_core_sample",
    )(q, k, v, sg, gn.reshape(gn.shape[0], heads, 1, dv), dm_blk, qd_blk, kd_blk, cd,
      s0_all, *extra_in)


def _hg_proj_kernel(xp_ref, xs_ref, g_ref, w_ref, lbl_ref,
                    qp_ref, qs_ref, kp_ref, ks_ref, gp_ref, gs_ref, vp_ref, vs_ref,
                    sp_ref, ss_ref, xn_ref, *, layer, cols):
    tm, d = xp_ref.shape
    _load_normed(xp_ref, xs_ref, g_ref, xn_ref)
    xn = xn_ref[...]
    logits = lbl_ref[...]
    e = jnp.exp(logits - jnp.max(logits, axis=0, keepdims=True))
    sm = e / jnp.sum(e, axis=0, keepdims=True)
    cum = sm[0:1]
    for i in range(1, layer + 1):
        cum = cum + sm[i:i + 1]
    lb = cum - sm[0:1]
    log_lb = jnp.log(lb)
    log_1m_lb = jnp.log1p(-lb)

    def put(p_ref, s_ref, lo, val):
        p_ref[:, lo:lo + cols] = val[:tm].astype(p_ref.dtype)
        s_ref[:, lo:lo + cols] = val[tm:].astype(s_ref.dtype)

    for c in range(d // cols):
        lo = c * cols
        put(qp_ref, qs_ref, lo, _silu(_dot(xn, w_ref[:, lo:lo + cols])))
        z = _dot(xn, w_ref[:, d + lo:d + lo + cols])
        ez = _exp_neg(jnp.abs(z))
        ez1 = 1.0 + ez
        log_sig = jnp.minimum(z, 0.0) - jnp.log(ez1)
        a = log_lb[:, lo:lo + cols]
        b = log_1m_lb[:, lo:lo + cols] + log_sig
        delta = a - b
        g = jnp.where(delta != delta, a + b,
                      jnp.maximum(a, b) + jnp.log(1.0 + _exp_neg(jnp.abs(delta))))
        put(gp_ref, gs_ref, lo, g)
        sig_neg = jnp.where(z >= 0.0, ez, 1.0) * (1.0 / ez1)
        put(kp_ref, ks_ref, lo, (1.0 - lb[:, lo:lo + cols]) * sig_neg)
        put(vp_ref, vs_ref, lo, _dot(xn, w_ref[:, 2 * d + lo:2 * d + lo + cols]))
        put(sp_ref, ss_ref, lo, _silu(_dot(xn, w_ref[:, 3 * d + lo:3 * d + lo + cols])))


def _hg_proj(xp, xs, gains, w_in, lb_logits, layer, j):
    d = xp.shape[1]
    tm, ts, steps = _token_tiles(xp, xs)
    out_shape, out_specs = [], []
    for dt in (BF16, BF16, F32, BF16, BF16):
        out_shape += [jax.ShapeDtypeStruct(xp.shape, dt), jax.ShapeDtypeStruct(xs.shape, dt)]
        out_specs += list(_token_specs(tm, ts, d))
    return pl.pallas_call(
        functools.partial(_hg_proj_kernel, layer=j, cols=2 * LANE),
        out_shape=tuple(out_shape),
        grid=(steps,),
        in_specs=[*_token_specs(tm, ts, d),
                  _resident((None, None, 1, d), (layer, 1, 0, 0)),
                  _resident((None,) + w_in.shape[1:], (j, 0, 0)),
                  _resident(lb_logits.shape)],
        out_specs=tuple(out_specs),
        scratch_shapes=[pltpu.VMEM((tm + ts, d), BF16)],
        compiler_params=_cparams(1),
        name="hg_proj",
    )(xp, xs, gains.reshape(gains.shape[0], gains.shape[1], 1, d), w_in, lb_logits)


def _hg_tables(chunk, block):
    n_levels = int(math.log2(block))
    assert 2 ** n_levels == block and chunk % block == 0 and chunk == LANE
    t = np.arange(chunk)[:, None]
    s = np.arange(chunk)[None, :]
    same_run = (t // block) == (s // block)
    sums = np.concatenate([same_run & (s <= t), same_run & (s > t)], axis=0)
    masks = [t == s]
    for l in range(n_levels):
        m = 2 ** l
        mid = (t // (2 * m)) * (2 * m) + m
        masks.append(((t // (2 * m)) == (s // (2 * m))) & (t >= mid) & (s < mid))
    r = np.arange(chunk)
    rows = np.stack([r % 2 == 1, r % 4 == 0, r % 4 >= 2, r % 4 == 3,
                     np.where(r % 8 >= 4, 1.0, -1.0)]).astype(np.float32)
    rows = np.broadcast_to(rows[:, :, None], rows.shape + (LANE,))
    block_causal = ((t // HG_FAST_BLOCK) == (s // HG_FAST_BLOCK)) & (s <= t)
    rows = np.concatenate([rows, block_causal[None].astype(np.float32)])
    sums = jnp.asarray(sums.astype(np.float32), dtype=BF16)
    masks = jnp.asarray(np.stack(masks).astype(np.float32), dtype=BF16)
    return sums, masks, jnp.asarray(rows)


def _level_exponents(b, g2, b_ref, rows_ref, block):
    c = b.shape[0]
    out = []
    n_levels = int(math.log2(block))
    for l in range(n_levels):
        m = 2 ** l
        if m == 1:
            out.append(g2 * rows_ref[0])
        elif m == 2:
            nxt = pltpu.roll(g2, c - 1, 0)
            prv = pltpu.roll(g2, 1, 0)
            out.append(nxt * rows_ref[1] + g2 * rows_ref[2] + prv * rows_ref[3])
        elif 2 * m == SUBLANE:
            pieces = [b[lo:lo + SUBLANE] - b_ref[pl.ds(lo + m - 1, 1), :]
                      for lo in range(0, c, SUBLANE)]
            out.append(jnp.concatenate(pieces, axis=0) * rows_ref[4])
        else:
            pieces = []
            for lo in range(0, c, 2 * m):
                ref_row = b_ref[pl.ds(lo + m - 1, 1), :]
                pieces.append(ref_row - b[lo:lo + m])
                pieces.append(b[lo + m:lo + 2 * m] - ref_row)
            out.append(jnp.concatenate(pieces, axis=0))
    return out


def _hg_chunk(q, k, v, g, sums_ref, masks_ref, rows_ref, b_ref, block):
    c = q.shape[0]
    g2 = g * LOG2E
    g_hi = g2.astype(BF16)
    g_lo = (g2 - g_hi.astype(F32)).astype(BF16)
    g_split = jnp.concatenate([g_hi, g_lo], axis=1)
    both = _dot(sums_ref[:c, :] if block == c else sums_ref[...], g_split)
    yield
    both = both[:, :LANE] + both[:, LANE:]
    if block == c:
        b = both
        b_rest = b[c - 1:c, :] - b
    else:
        b, b_rest = both[:c], both[c:]
    b_ref[...] = b
    level_scores = [_dot_nt(q, k)]
    for nd in _level_exponents(b, g2, b_ref, rows_ref, block):
        e = jnp.exp2(nd).astype(BF16)
        level_scores.append(_dot_nt(q * e, k * e))
    eb = jnp.exp2(b)
    q_dec = q * eb.astype(BF16)
    k_dec = k * jnp.exp2(b_rest).astype(BF16)
    yield
    scores = level_scores[0].astype(BF16) * masks_ref[0]
    for l, s in enumerate(level_scores[1:]):
        scores = scores + s.astype(BF16) * masks_ref[1 + l]
    o_intra = _dot(scores, v)
    return o_intra, q_dec, k_dec, eb


def _hg_chunk_fast(q, k, v, g, sums_ref, rows_ref):
    c = q.shape[0]
    blk = HG_FAST_BLOCK
    g2 = g * LOG2E
    g_hi = g2.astype(BF16)
    g_lo = (g2 - g_hi.astype(F32)).astype(BF16)
    b = _dot(sums_ref[:c, :], jnp.concatenate([g_hi, g_lo], axis=1))
    yield
    b = b[:, :LANE] + b[:, LANE:]
    qf = q.astype(F32)
    kf = k.astype(F32)
    q_blocks, k_blocks, cross = [], [], [jnp.zeros((blk, c), F32)]
    for i in range(c // blk):
        rows = slice(i * blk, (i + 1) * blk)
        if i == 0:
            d = b[rows]
        else:
            r = b[i * blk - 1:i * blk, :]
            d = b[rows] - r
        q_i = (qf[rows] * jnp.exp2(d)).astype(BF16)
        q_blocks.append(q_i)
        k_blocks.append((kf[rows] * jnp.exp2(-d)).astype(BF16))
        if i > 0:
            k_before = (kf[:i * blk] * jnp.exp2(r - b[:i * blk])).astype(BF16)
            k_before = jnp.concatenate(
                [k_before, jnp.zeros((c - i * blk, LANE), BF16)], axis=0)
            cross.append(_dot_nt(q_i, k_before))
    within = _dot_nt(jnp.concatenate(q_blocks, axis=0), jnp.concatenate(k_blocks, axis=0))
    eb = jnp.exp2(b)
    q_dec = (qf * eb).astype(BF16)
    k_dec = (kf * jnp.exp2(b[c - 1:c, :] - b)).astype(BF16)
    yield
    scores = within * rows_ref[5] + jnp.concatenate(cross, axis=0)
    o_intra = _dot(scores.astype(BF16), v)
    return o_intra, q_dec, k_dec, eb


def _hg_core_prompt_kernel(q_ref, k_ref, v_ref, g_ref, sg_ref, gn_ref, sums_ref, masks_ref,
                           rows_ref, *rest, n_chunks, chunk, hps, dh):
    og_ref, s_ref, st_ref, b_ref = rest[-4:]
    st_ref[...] = jnp.zeros(st_ref.shape, F32)

    def head_chain(fast, hh, qc, kc, vc, gc, sgc, st_old):
        if fast:
            intra = _hg_chunk_fast(qc, kc, vc, gc, sums_ref, rows_ref)
        else:
            intra = _hg_chunk(qc, kc, vc, gc, sums_ref, masks_ref, rows_ref, b_ref.at[hh], chunk)
        o_intra, q_dec, k_dec, eb = yield from intra
        o_inter = _dot_nt(q_dec, st_old.astype(BF16))
        update = _dot_tn(vc, k_dec)
        yield
        st_new = st_old * eb[chunk - 1:chunk, :] + update
        return st_new, _head_norm_gate(o_intra + o_inter, gn_ref[hh], sgc)

    def chunk_loop(fast):
        def body(c, carry):
            rows = pl.ds(pl.multiple_of(c * chunk, chunk), chunk)
            heads = [slice(hh * dh, (hh + 1) * dh) for hh in range(hps)]
            chains = [head_chain(fast, hh, q_ref[rows, cols], k_ref[rows, cols],
                                 v_ref[rows, cols], g_ref[rows, cols], sg_ref[rows, cols],
                                 st_ref[hh])
                      for hh, cols in enumerate(heads)]
            for hh, (st_new, og) in enumerate(_run_staged(chains)):
                st_ref[hh] = st_new
                og_ref[rows, heads[hh]] = og
            return carry

        lax.fori_loop(0, n_chunks, body, 0)

    g_abs = jnp.abs(g_ref[...])
    block_sums = jnp.sum(g_abs.reshape(-1, HG_FAST_BLOCK, g_abs.shape[1]), axis=1)
    fast_ok = jnp.max(block_sums) * LOG2E < HG_FAST_LIMIT
    pl.when(fast_ok)(lambda: chunk_loop(True))
    pl.when(jnp.logical_not(fast_ok))(lambda: chunk_loop(False))
    for hh in range(hps):
        s_ref[0, hh] = st_ref[hh].T


def _hg_core_prompt(q, k, v, g, sg, gn, j, n_layers, prev_state, n_seqs, seq_len):
    n, d = q.shape
    heads = HG_HEADS
    hps = HG_HEADS_PER_STEP
    dh = d // heads
    chunk = math.gcd(seq_len, HG_CHUNK)
    sums, masks, row_tabs = _hg_tables(chunk, chunk)
    per_step = pl.BlockSpec((seq_len, hps * dh), lambda b, h: (b, h))
    s_spec, s_shape, extra_in, extra_spec = _state_out(
        prev_state, (n_layers, n_seqs, heads, dh, dh), j, (1, hps, dh, dh),
        lambda b, h: (b, h, 0, 0))
    n_in = 9 + len(extra_in)
    return pl.pallas_call(
        functools.partial(_hg_core_prompt_kernel, n_chunks=seq_len // chunk, chunk=chunk,
                          hps=hps, dh=dh),
        out_shape=(jax.ShapeDtypeStruct((n, d), BF16), s_shape),
        grid=(n_seqs, heads // hps),
        in_specs=[per_step, per_step, per_step, per_step, per_step,
                  pl.BlockSpec((None, hps, 1, dh), lambda b, h: (j, h, 0, 0)),
                  _resident(sums.shape), _resident(masks.shape),
                  _resident(row_tabs.shape)] + extra_spec,
        out_specs=(per_step, s_spec),
        input_output_aliases={n_in - 1: 1} if extra_in else {},
        scratch_shapes=[pltpu.VMEM((hps, dh, dh), F32), pltpu.VMEM((hps, chunk, dh), F32)],
        compiler_params=_cparams(2),
        name="hg_core_prompt",
    )(q, k, v, g, sg, gn.reshape(gn.shape[0], heads, 1, dh), sums, masks, row_tabs, *extra_in)


def _hg_core_sample_kernel(q_ref, k_ref, v_ref, g_ref, sg_ref, gn_ref, sums_ref, masks_ref,
                           rows_ref, s0_ref, *rest, seqs, dec_len):
    og_ref, s_ref, b_ref = rest[-3:]
    rows = seqs * dec_len
    vb = v_ref[...]
    (o, q_dec, k_dec, eb), = _run_staged([_hg_chunk(
        q_ref[...], k_ref[...], vb, g_ref[...], sums_ref, masks_ref, rows_ref, b_ref, dec_len)])
    eb_t = eb.T
    k_dec_t = k_dec.astype(F32).T
    row_seq = lax.broadcasted_iota(jnp.int32, (rows, 1), 0) // dec_len
    col_seq = lax.broadcasted_iota(jnp.int32, (1, rows), 1) // dec_len
    inter = jnp.zeros(o.shape, F32)
    for i in range(seqs):
        s_old = s0_ref[i, 0]
        inter = jnp.where(row_seq == i, _dot(q_dec, s_old.astype(BF16)), inter)
        k_i = jnp.where(col_seq == i, k_dec_t, 0.0).astype(BF16)
        last = (i + 1) * dec_len - 1
        s_ref[i, 0] = s_old * eb_t[:, last:last + 1] + _dot(k_i, vb)
    og_ref[...] = _head_norm_gate(o + inter, gn_ref[0], sg_ref[...])


def _hg_core_sample(q, k, v, g, sg, gn, s0_all, j, prev_state, dec_len):
    heads = HG_HEADS
    n_layers, n_seqs = s0_all.shape[:2]
    dh = s0_all.shape[3]
    rows = math.gcd(n_seqs * dec_len, HG_CHUNK)
    seqs = rows // dec_len
    assert rows % dec_len == 0
    sums, masks, row_tabs = _hg_tables(rows, dec_len)
    per_head = pl.BlockSpec((rows, dh), lambda i, h: (i, h))
    state_in = pl.BlockSpec((None, seqs, 1, dh, dh), lambda i, h: (j, i, h, 0, 0))
    s_spec, s_shape, extra_in, extra_spec = _state_out(
        prev_state, s0_all.shape, j, (seqs, 1, dh, dh), lambda i, h: (i, h, 0, 0))
    n_in = 10 + len(extra_in)
    return pl.pallas_call(
        functools.partial(_hg_core_sample_kernel, seqs=seqs, dec_len=dec_len),
        out_shape=(jax.ShapeDtypeStruct(q.shape, BF16), s_shape),
        grid=(n_seqs // seqs, heads),
        in_specs=[per_head, per_head, per_head, per_head, per_head,
                  pl.BlockSpec((None, 1, 1, dh), lambda i, h: (j, h, 0, 0)),
                  _resident(sums.shape), _resident(masks.shape), _resident(row_tabs.shape),
                  state_in] + extra_spec,
        out_specs=(per_head, s_spec),
        input_output_aliases={n_in - 1: 1} if extra_in else {},
        scratch_shapes=[pltpu.VMEM((rows, dh), F32)],
        compiler_params=_cparams(2),
        name="hg_core_sample",
    )(q, k, v, g, sg, gn.reshape(gn.shape[0], heads, 1, dh), sums, masks, row_tabs, s0_all,
      *extra_in)


def kernel(x_prompt, x_sample, state_ret, state_hgrn, norm_gain, ffn_w_up, ffn_w_down,
           ret_w_in, ret_norm, ret_w_out, hg_w_in, hg_lb_logits, hg_norm, hg_w_out, final_norm):
    n_seqs, seq_len, d = x_prompt.shape
    dec_seqs, dec_len, _ = x_sample.shape
    depth = norm_gain.shape[0]
    n_ret, n_hg = state_ret.shape[0], state_hgrn.shape[0]

    xp = x_prompt.reshape(n_seqs * seq_len, d)
    xs = x_sample.reshape(dec_seqs * dec_len, d)
    w_up, w_down = ffn_w_up.astype(BF16), ffn_w_down.astype(BF16)
    ret_in, ret_out = ret_w_in.astype(BF16), ret_w_out.astype(BF16)
    hg_in, hg_out = hg_w_in.astype(BF16), hg_w_out.astype(BF16)

    ret_p = ret_s = hg_p = hg_s = None
    for layer in range(depth):
        xp, xs = _ffn(xp, xs, norm_gain, w_up, w_down, layer, 0)
        j = layer // 2
        if layer % 2 == 0:
            qp, qs, kp, ks, vp, vs, gp, gs = _ret_proj(
                xp, xs, norm_gain, ret_in, layer, j, seq_len, dec_len)
            ap, ret_p = _ret_core_prompt(qp, kp, vp, gp, ret_norm, j, n_ret, ret_p,
                                         n_seqs, seq_len)
            a_s, ret_s = _ret_core_sample(qs, ks, vs, gs, ret_norm, state_ret, j, ret_s, dec_len)
            mix = (ap, a_s, ret_out, j)
        else:
            qp, qs, kp, ks, gp, gs, vp, vs, sp, ss = _hg_proj(
                xp, xs, norm_gain, hg_in, hg_lb_logits, layer, j)
            ap, hg_p = _hg_core_prompt(qp, kp, vp, gp, sp, hg_norm, j, n_hg, hg_p,
                                       n_seqs, seq_len)
            a_s, hg_s = _hg_core_sample(qs, ks, vs, gs, ss, hg_norm, state_hgrn, j, hg_s,
                                        dec_len)
            mix = (ap, a_s, hg_out, j)
        xp, xs = _ffn(xp, xs, norm_gain, w_up, w_down, layer, 1, mix,
                      final_norm if layer == depth - 1 else None)
    return (xp.reshape(n_seqs, seq_len, d), xs.reshape(dec_seqs, dec_len, d),
            ret_p, ret_s, hg_p, hg_s)
```

```python
import functools
import math

import jax
import jax.numpy as jnp
import numpy as np
from jax import lax
from jax.experimental import pallas as pl
from jax.experimental.pallas import tpu as pltpu

F32 = jnp.float32
BF16 = jnp.bfloat16

EPS = 1e-6
ROPE_BASE = 10000.0
PAST_LEN = 16384
RET_HEADS = 4
RET_CHUNK = 128
RET_HEADS_PER_STEP = 2
RET_CHUNKS_PER_ITER = 4
RET_SAMPLE_SEQS = 8
HG_HEADS = 8
HG_CHUNK = 128
HG_HEADS_PER_STEP = 4
HG_CHUNKS_PER_ITER = 4
HG_FAST_BLOCK = 32
HG_FAST_LIMIT = 120.0
TOKEN_TILE = 1024
MIX_FFN_TILE = 512
FFN_COLS = 256
LANE = 128
SUBLANE = 8
VMEM_LIMIT = 56 * 1024 * 1024
LOG2E = 1.4426950408889634


def _cparams(n_axes):
    return pltpu.CompilerParams(
        dimension_semantics=("arbitrary",) * n_axes,
        vmem_limit_bytes=VMEM_LIMIT)


def _resident(shape, index=None):
    index = (0,) * len(shape) if index is None else index
    return pl.BlockSpec(shape, lambda *_: index, pipeline_mode=pl.Buffered(1))


def _dot(a, b):
    return jnp.dot(a, b, preferred_element_type=F32)


def _dot_nt(a, b):
    return lax.dot_general(a, b, (((1,), (1,)), ((), ())),
                           preferred_element_type=F32)


def _dot_tn(a, b):
    return lax.dot_general(a, b, (((0,), (0,)), ((), ())),
                           preferred_element_type=F32)


def _rmsnorm(x, gain):
    ms = jnp.mean(x * x, axis=-1, keepdims=True)
    return x * lax.rsqrt(ms + EPS) * gain


def _exp_neg(x):
    return jnp.exp2(x * (-LOG2E))


def _silu(x):
    return x * (1.0 / (1.0 + _exp_neg(x)))


def _run_staged(gens):
    results = [None] * len(gens)
    live = list(range(len(gens)))
    while live:
        still = []
        for i in live:
            try:
                next(gens[i])
                still.append(i)
            except StopIteration as stop:
                results[i] = stop.value
        live = still
    return results


def _token_specs(tm, ts, width):
    return (pl.BlockSpec((tm, width), lambda i: (i, 0)),
            pl.BlockSpec((ts, width), lambda i: (i, 0)))


def _token_tiles(xp, xs, tile=None):
    n_prompt, n_sample = xp.shape[0], xs.shape[0]
    tm = math.gcd(n_prompt, TOKEN_TILE if tile is None else tile)
    steps = n_prompt // tm
    ts = n_sample // steps
    assert ts * steps == n_sample and ts % 16 == 0
    return tm, ts, steps


def _load_normed(xp_ref, xs_ref, g_ref, xn_ref):
    tm = xp_ref.shape[0]
    xn_ref[:tm, :] = _rmsnorm(xp_ref[...], g_ref[...]).astype(BF16)
    xn_ref[tm:, :] = _rmsnorm(xs_ref[...], g_ref[...]).astype(BF16)


def _ffn_body(xp, xs, g_ref, wu_ref, wd_ref, op_ref, os_ref, xn_ref, h_ref, fg_ref=None):
    tm = xp.shape[0]
    d_ff = wd_ref.shape[0]
    xn_ref[:tm, :] = _rmsnorm(xp, g_ref[...]).astype(BF16)
    xn_ref[tm:, :] = _rmsnorm(xs, g_ref[...]).astype(BF16)
    xn = xn_ref[...]
    for c in range(d_ff // FFN_COLS):
        lo = c * FFN_COLS
        a = _dot(xn, wu_ref[:, lo:lo + FFN_COLS])
        b = _dot(xn, wu_ref[:, d_ff + lo:d_ff + lo + FFN_COLS])
        h_ref[:, lo:lo + FFN_COLS] = (_silu(a) * b).astype(BF16)
    y = _dot(h_ref[...], wd_ref[...])
    yp = xp + 0.5 * y[:tm]
    ys = xs + 0.5 * y[tm:]
    if fg_ref is not None:
        yp, ys = _rmsnorm(yp, fg_ref[...]), _rmsnorm(ys, fg_ref[...])
    op_ref[...] = yp
    os_ref[...] = ys


def _ffn_kernel(xp_ref, xs_ref, g_ref, wu_ref, wd_ref, op_ref, os_ref, xn_ref, h_ref):
    _ffn_body(xp_ref[...], xs_ref[...], g_ref, wu_ref, wd_ref, op_ref, os_ref, xn_ref, h_ref)


def _mix_ffn_kernel(xp_ref, xs_ref, ap_ref, as_ref, wo_ref, g_ref, wu_ref, wd_ref, *rest):
    fg_ref = rest[0] if len(rest) == 5 else None
    xp = xp_ref[...] + _dot(ap_ref[...], wo_ref[...])
    xs = xs_ref[...] + _dot(as_ref[...], wo_ref[...])
    _ffn_body(xp, xs, g_ref, wu_ref, wd_ref, *rest[-4:], fg_ref=fg_ref)


def _ffn(xp, xs, gains, w_up, w_down, layer, which, mix=None, final_gain=None):
    d = xp.shape[1]
    d_ff = w_down.shape[2]
    tm, ts, steps = _token_tiles(xp, xs, None if mix is None else MIX_FFN_TILE)
    assert d_ff % FFN_COLS == 0
    xspec = _token_specs(tm, ts, d)
    gain = _resident((None, None, 1, d), (layer, 2 * which, 0, 0))
    weights = [_resident((None, None, d, 2 * d_ff), (layer, which, 0, 0)),
               _resident((None, None, d_ff, d), (layer, which, 0, 0))]
    gains4 = gains.reshape(gains.shape[0], gains.shape[1], 1, d)
    if mix is None:
        body, in_specs, args = _ffn_kernel, [*xspec, gain, *weights], (xp, xs, gains4, w_up, w_down)
    else:
        ap, a_s, w_out, j = mix
        k = ap.shape[1]
        body = _mix_ffn_kernel
        in_specs = [*xspec, *_token_specs(tm, ts, k), _resident((None, k, d), (j, 0, 0)),
                    gain, *weights]
        args = (xp, xs, ap, a_s, w_out, gains4, w_up, w_down)
        if final_gain is not None:
            in_specs.append(_resident((1, d)))
            args += (final_gain.reshape(1, d),)
    return pl.pallas_call(
        body,
        out_shape=(jax.ShapeDtypeStruct(xp.shape, F32), jax.ShapeDtypeStruct(xs.shape, F32)),
        grid=(steps,),
        in_specs=in_specs,
        out_specs=xspec,
        scratch_shapes=[pltpu.VMEM((tm + ts, d), BF16), pltpu.VMEM((tm + ts, d_ff), BF16)],
        compiler_params=_cparams(1),
        name="ffn" if mix is None else "mix_ffn",
    )(*args)


def _ret_proj_kernel(xp_ref, xs_ref, g_ref, w_ref, cp_ref, sp_ref, cs_ref, ss_ref,
                     qp_ref, qs_ref, kp_ref, ks_ref, vp_ref, vs_ref, gp_ref, gs_ref, xn_ref,
                     *, dk, dv, heads):
    tm = xp_ref.shape[0]
    _load_normed(xp_ref, xs_ref, g_ref, xn_ref)
    xn = xn_ref[...]
    cos = jnp.concatenate([cp_ref[...], cs_ref[...]], axis=0)
    sin = jnp.concatenate([sp_ref[...], ss_ref[...]], axis=0)
    half = dk // 2
    qk = heads * dk
    k_scale = dk ** -0.5

    def put(p_ref, s_ref, lo, val):
        val = val.astype(BF16)
        p_ref[:, lo:lo + val.shape[1]] = val[:tm]
        s_ref[:, lo:lo + val.shape[1]] = val[tm:]

    for h in range(heads):
        for base, p_ref, s_ref, scale in ((0, qp_ref, qs_ref, None), (qk, kp_ref, ks_ref, k_scale)):
            t = _dot(xn, w_ref[:, base + h * dk:base + (h + 1) * dk])
            x1, x2 = t[:, :half], t[:, half:]
            r1 = x1 * cos - x2 * sin
            r2 = x1 * sin + x2 * cos
            if scale is not None:
                r1, r2 = r1 * scale, r2 * scale
            put(p_ref, s_ref, h * dk, r1)
            put(p_ref, s_ref, h * dk + half, r2)
    for h in range(heads):
        lo = 2 * qk + h * dv
        put(vp_ref, vs_ref, h * dv, _dot(xn, w_ref[:, lo:lo + dv]))
        lo = 2 * qk + heads * dv + h * dv
        put(gp_ref, gs_ref, h * dv, _silu(_dot(xn, w_ref[:, lo:lo + dv])))


def _rope_tables(pos, half):
    inv_freq = ROPE_BASE ** (-jnp.arange(half, dtype=F32) / half)
    ang = pos[:, None] * inv_freq[None, :]
    return jnp.cos(ang), jnp.sin(ang)


def _ret_proj(xp, xs, gains, w_in, layer, j, seq_len, dec_len):
    d = xp.shape[1]
    heads = RET_HEADS
    dk = d // heads
    dv = 2 * dk
    half = dk // 2
    tm, ts, steps = _token_tiles(xp, xs)
    assert seq_len % tm == 0 and ts % dec_len == 0
    tiles_per_seq = seq_len // tm
    cos_p, sin_p = _rope_tables(jnp.arange(seq_len, dtype=F32), half)
    cos_s, sin_s = _rope_tables(
        jnp.tile(PAST_LEN + jnp.arange(dec_len, dtype=F32), ts // dec_len), half)
    tab_p = pl.BlockSpec((tm, half), lambda i: (i % tiles_per_seq, 0))
    widths = (heads * dk, heads * dk, heads * dv, heads * dv)
    out_shape, out_specs = [], []
    for w in widths:
        out_shape += [jax.ShapeDtypeStruct((xp.shape[0], w), BF16),
                      jax.ShapeDtypeStruct((xs.shape[0], w), BF16)]
        out_specs += list(_token_specs(tm, ts, w))
    return pl.pallas_call(
        functools.partial(_ret_proj_kernel, dk=dk, dv=dv, heads=heads),
        out_shape=tuple(out_shape),
        grid=(steps,),
        in_specs=[*_token_specs(tm, ts, d),
                  _resident((None, None, 1, d), (layer, 1, 0, 0)),
                  _resident((None,) + w_in.shape[1:], (j, 0, 0)),
                  tab_p, tab_p, _resident((ts, half)), _resident((ts, half))],
        out_specs=tuple(out_specs),
        scratch_shapes=[pltpu.VMEM((tm + ts, d), BF16)],
        compiler_params=_cparams(1),
        name="ret_proj",
    )(xp, xs, gains.reshape(gains.shape[0], gains.shape[1], 1, d), w_in,
      cos_p, sin_p, cos_s, sin_s)


def _head_norm_gate(o, gn, sg):
    ms = jnp.mean(o * o, axis=-1, keepdims=True)
    return (o * lax.rsqrt(ms + EPS) * gn * sg.astype(F32)).astype(BF16)


def _ret_core_prompt_kernel(q_ref, k_ref, v_ref, sg_ref, gn_ref, dm_ref, qd_ref, kd_ref,
                            cd_ref, *rest, n_chunks, chunk, hps, dk, dv):
    og_ref, s_ref = rest[-2:]
    h0 = pl.program_id(1) * hps
    for hh in range(hps):
        s_ref[0, hh] = jnp.zeros((dk, dv), F32)

    cpi = math.gcd(n_chunks, RET_CHUNKS_PER_ITER)

    def head_chain(hh, rows):
        qs = [q_ref[r, hh * dk:(hh + 1) * dk] for r in rows]
        ks = [k_ref[r, hh * dk:(hh + 1) * dk] for r in rows]
        vs = [v_ref[r, hh * dv:(hh + 1) * dv] for r in rows]
        sgs = [sg_ref[r, hh * dv:(hh + 1) * dv] for r in rows]
        s = s_ref[0, hh]
        cd = cd_ref[h0 + hh]
        scores = [_dot_nt(qc, kc) for qc, kc in zip(qs, ks)]
        updates = [_dot_tn((kc.astype(F32) * kd_ref[hh]).astype(BF16), vc)
                   for kc, vc in zip(ks, vs)]
        yield
        intra = [_dot((sc * dm_ref[hh]).astype(BF16), vc) for sc, vc in zip(scores, vs)]
        inter = []
        for qc, upd in zip(qs, updates):
            inter.append(_dot(qc, s.astype(BF16)))
            s = s * cd + upd
        yield
        return s, [_head_norm_gate(oi + oc * qd_ref[hh], gn_ref[hh], sgc)
                   for oi, oc, sgc in zip(intra, inter, sgs)]

    def body(it, carry):
        rows = [pl.ds(pl.multiple_of((it * cpi + cc) * chunk, chunk), chunk)
                for cc in range(cpi)]
        results = _run_staged([head_chain(hh, rows) for hh in range(hps)])
        for hh, (s_new, ogs) in enumerate(results):
            s_ref[0, hh] = s_new
            for r, og in zip(rows, ogs):
                og_ref[r, hh * dv:(hh + 1) * dv] = og
        return carry

    lax.fori_loop(0, n_chunks // cpi, body, 0)


def _ret_decay_tables(chunk):
    heads = RET_HEADS
    log_gamma = jnp.log(1.0 - jnp.power(2.0, -5.0 - jnp.arange(heads, dtype=F32)))
    idx = jnp.arange(chunk, dtype=F32)
    diff = idx[:, None] - idx[None, :]
    lg = log_gamma[:, None, None]
    decay_mat = jnp.where(diff[None] >= 0, jnp.exp(diff[None] * lg), 0.0)
    q_decay = jnp.exp((idx[None, :] + 1.0) * log_gamma[:, None])
    k_decay = jnp.exp((chunk - 1.0 - idx[None, :]) * log_gamma[:, None])
    chunk_decay = jnp.exp(chunk * log_gamma)
    return decay_mat, q_decay, k_decay, chunk_decay


def _state_out(prev, shape, j, block, index_map):
    spec = pl.BlockSpec((None,) + block, lambda *ids: (j,) + index_map(*ids))
    extra_in = [] if prev is None else [prev]
    extra_spec = [] if prev is None else [pl.BlockSpec(memory_space=pl.ANY)]
    return spec, jax.ShapeDtypeStruct(shape, F32), extra_in, extra_spec


def _ret_core_prompt(q, k, v, sg, gn, j, n_layers, prev_state, n_seqs, seq_len):
    n = q.shape[0]
    heads = RET_HEADS
    hps = RET_HEADS_PER_STEP
    dk = q.shape[1] // heads
    dv = v.shape[1] // heads
    chunk = math.gcd(seq_len, RET_CHUNK)
    dm, qd, kd, cd = _ret_decay_tables(chunk)
    per_step = lambda w: pl.BlockSpec((seq_len, hps * w), lambda b, h: (b, h))
    tab = lambda shape: pl.BlockSpec((hps,) + shape, lambda b, h: (h, 0, 0))
    s_spec, s_shape, extra_in, extra_spec = _state_out(
        prev_state, (n_layers, n_seqs, heads, dk, dv), j, (1, hps, dk, dv),
        lambda b, h: (b, h, 0, 0))
    n_in = 9 + len(extra_in)
    return pl.pallas_call(
        functools.partial(_ret_core_prompt_kernel, n_chunks=seq_len // chunk, chunk=chunk,
                          hps=hps, dk=dk, dv=dv),
        out_shape=(jax.ShapeDtypeStruct((n, heads * dv), BF16), s_shape),
        grid=(n_seqs, heads // hps),
        in_specs=[per_step(dk), per_step(dk), per_step(dv), per_step(dv),
                  pl.BlockSpec((None, hps, 1, dv), lambda b, h: (j, h, 0, 0)),
                  tab((chunk, chunk)), tab((chunk, 1)), tab((chunk, 1)),
                  pl.BlockSpec(memory_space=pltpu.SMEM)] + extra_spec,
        out_specs=(per_step(dv), s_spec),
        input_output_aliases={n_in - 1: 1} if extra_in else {},
        compiler_params=_cparams(2),
        name="ret_core_prompt",
    )(q, k, v, sg, gn.reshape(gn.shape[0], heads, 1, dv), dm, qd[:, :, None], kd[:, :, None],
      cd, *extra_in)


def _ret_core_sample_kernel(q_ref, k_ref, v_ref, sg_ref, gn_ref, dm_ref, qd_ref, kd_ref,
                            cd_ref, s0_ref, *rest, seqs, dec_len):
    og_ref, s_ref = rest[-2:]
    h = pl.program_id(1)
    rows = seqs * dec_len
    qb = q_ref[...]
    vb = v_ref[...]
    kdec = k_ref[...].astype(F32) * kd_ref[0]
    cd = cd_ref[h]
    row_seq = lax.broadcasted_iota(jnp.int32, (rows, 1), 0) // dec_len
    scores = _dot_nt(qb, k_ref[...]) * dm_ref[0]
    o = _dot(scores.astype(BF16), vb)
    inter = jnp.zeros(o.shape, F32)
    for i in range(seqs):
        s_old = s0_ref[i, 0]
        mine = row_seq == i
        inter = jnp.where(mine, _dot(qb, s_old.astype(BF16)), inter)
        k_i = jnp.where(mine, kdec, 0.0).astype(BF16)
        s_ref[i, 0] = s_old * cd + _dot_tn(k_i, vb)
    o = o + inter * qd_ref[0]
    og_ref[...] = _head_norm_gate(o, gn_ref[0], sg_ref[...])


def _ret_core_sample(q, k, v, sg, gn, s0_all, j, prev_state, dec_len):
    heads = RET_HEADS
    n_layers, n_seqs = s0_all.shape[:2]
    dk, dv = s0_all.shape[3], s0_all.shape[4]
    seqs = RET_SAMPLE_SEQS
    rows = seqs * dec_len
    assert n_seqs % seqs == 0
    chunk = math.gcd(dec_len, RET_CHUNK)
    assert chunk == dec_len
    dm, qd, kd, cd = _ret_decay_tables(chunk)
    seq_id = np.arange(rows) // dec_len
    same = jnp.asarray(seq_id[:, None] == seq_id[None, :])
    dm_blk = jnp.where(same[None], jnp.tile(dm, (1, seqs, seqs)), 0.0)
    qd_blk = jnp.tile(qd, (1, seqs))[:, :, None]
    kd_blk = jnp.tile(kd, (1, seqs))[:, :, None]
    per_head = lambda w: pl.BlockSpec((rows, w), lambda i, h: (i, h))
    tab = lambda shape: pl.BlockSpec((1,) + shape, lambda i, h: (h, 0, 0))
    state_in = pl.BlockSpec((None, seqs, 1, dk, dv), lambda i, h: (j, i, h, 0, 0))
    s_spec, s_shape, extra_in, extra_spec = _state_out(
        prev_state, s0_all.shape, j, (seqs, 1, dk, dv), lambda i, h: (i, h, 0, 0))
    n_in = 10 + len(extra_in)
    return pl.pallas_call(
        functools.partial(_ret_core_sample_kernel, seqs=seqs, dec_len=dec_len),
        out_shape=(jax.ShapeDtypeStruct((q.shape[0], heads * dv), BF16), s_shape),
        grid=(n_seqs // seqs, heads),
        in_specs=[per_head(dk), per_head(dk), per_head(dv), per_head(dv),
                  pl.BlockSpec((None, 1, 1, dv), lambda i, h: (j, h, 0, 0)),
                  tab((rows, rows)), tab((rows, 1)), tab((rows, 1)),
                  pl.BlockSpec(memory_space=pltpu.SMEM), state_in] + extra_spec,
        out_specs=(per_head(dv), s_spec),
        input_output_aliases={n_in - 1: 1} if extra_in else {},
        compiler_params=_cparams(2),
        name="ret_core_sample",
    )(q, k, v, sg, gn.reshape(gn.shape[0], heads, 1, dv), dm_blk, qd_blk, kd_blk, cd,
      s0_all, *extra_in)


def _hg_proj_kernel(xp_ref, xs_ref, g_ref, w_ref, lbl_ref,
                    qp_ref, qs_ref, kp_ref, ks_ref, gp_ref, gs_ref, vp_ref, vs_ref,
                    sp_ref, ss_ref, xn_ref, *, layer, cols):
    tm, d = xp_ref.shape
    _load_normed(xp_ref, xs_ref, g_ref, xn_ref)
    xn = xn_ref[...]
    logits = lbl_ref[...]
    e = jnp.exp(logits - jnp.max(logits, axis=0, keepdims=True))
    sm = e / jnp.sum(e, axis=0, keepdims=True)
    cum = sm[0:1]
    for i in range(1, layer + 1):
        cum = cum + sm[i:i + 1]
    lb = cum - sm[0:1]
    log_lb = jnp.log(lb)
    log_1m_lb = jnp.log1p(-lb)

    def put(p_ref, s_ref, lo, val):
        p_ref[:, lo:lo + cols] = val[:tm].astype(p_ref.dtype)
        s_ref[:, lo:lo + cols] = val[tm:].astype(s_ref.dtype)

    for c in range(d // cols):
        lo = c * cols
        put(qp_ref, qs_ref, lo, _silu(_dot(xn, w_ref[:, lo:lo + cols])))
        z = _dot(xn, w_ref[:, d + lo:d + lo + cols])
        ez = _exp_neg(jnp.abs(z))
        ez1 = 1.0 + ez
        log_sig = jnp.minimum(z, 0.0) - jnp.log(ez1)
        a = log_lb[:, lo:lo + cols]
        b = log_1m_lb[:, lo:lo + cols] + log_sig
        delta = a - b
        g = jnp.where(delta != delta, a + b,
                      jnp.maximum(a, b) + jnp.log(1.0 + _exp_neg(jnp.abs(delta))))
        put(gp_ref, gs_ref, lo, g)
        sig_neg = jnp.where(z >= 0.0, ez, 1.0) * (1.0 / ez1)
        put(kp_ref, ks_ref, lo, (1.0 - lb[:, lo:lo + cols]) * sig_neg)
        put(vp_ref, vs_ref, lo, _dot(xn, w_ref[:, 2 * d + lo:2 * d + lo + cols]))
        put(sp_ref, ss_ref, lo, _silu(_dot(xn, w_ref[:, 3 * d + lo:3 * d + lo + cols])))


def _hg_proj(xp, xs, gains, w_in, lb_logits, layer, j):
    d = xp.shape[1]
    tm, ts, steps = _token_tiles(xp, xs)
    out_shape, out_specs = [], []
    for dt in (BF16, BF16, F32, BF16, BF16):
        out_shape += [jax.ShapeDtypeStruct(xp.shape, dt), jax.ShapeDtypeStruct(xs.shape, dt)]
        out_specs += list(_token_specs(tm, ts, d))
    return pl.pallas_call(
        functools.partial(_hg_proj_kernel, layer=j, cols=2 * LANE),
        out_shape=tuple(out_shape),
        grid=(steps,),
        in_specs=[*_token_specs(tm, ts, d),
                  _resident((None, None, 1, d), (layer, 1, 0, 0)),
                  _resident((None,) + w_in.shape[1:], (j, 0, 0)),
                  _resident(lb_logits.shape)],
        out_specs=tuple(out_specs),
        scratch_shapes=[pltpu.VMEM((tm + ts, d), BF16)],
        compiler_params=_cparams(1),
        name="hg_proj",
    )(xp, xs, gains.reshape(gains.shape[0], gains.shape[1], 1, d), w_in, lb_logits)


def _hg_tables(chunk, block):
    n_levels = int(math.log2(block))
    assert 2 ** n_levels == block and chunk % block == 0 and chunk == LANE
    t = np.arange(chunk)[:, None]
    s = np.arange(chunk)[None, :]
    same_run = (t // block) == (s // block)
    sums = np.concatenate([same_run & (s <= t), same_run & (s > t)], axis=0)
    masks = [t == s]
    for l in range(n_levels):
        m = 2 ** l
        mid = (t // (2 * m)) * (2 * m) + m
        masks.append(((t // (2 * m)) == (s // (2 * m))) & (t >= mid) & (s < mid))
    r = np.arange(chunk)
    rows = np.stack([r % 2 == 1, r % 4 == 0, r % 4 >= 2, r % 4 == 3,
                     np.where(r % 8 >= 4, 1.0, -1.0)]).astype(np.float32)
    rows = np.broadcast_to(rows[:, :, None], rows.shape + (LANE,))
    block_causal = ((t // HG_FAST_BLOCK) == (s // HG_FAST_BLOCK)) & (s <= t)
    rows = np.concatenate([rows, block_causal[None].astype(np.float32)])
    sums = jnp.asarray(sums.astype(np.float32), dtype=BF16)
    masks = jnp.asarray(np.stack(masks).astype(np.float32), dtype=BF16)
    return sums, masks, jnp.asarray(rows)


def _level_exponents(b, g2, b_ref, rows_ref, block):
    c = b.shape[0]
    out = []
    n_levels = int(math.log2(block))
    for l in range(n_levels):
        m = 2 ** l
        if m == 1:
            out.append(g2 * rows_ref[0])
        elif m == 2:
            nxt = pltpu.roll(g2, c - 1, 0)
            prv = pltpu.roll(g2, 1, 0)
            out.append(nxt * rows_ref[1] + g2 * rows_ref[2] + prv * rows_ref[3])
        elif 2 * m == SUBLANE:
            pieces = [b[lo:lo + SUBLANE] - b_ref[pl.ds(lo + m - 1, 1), :]
                      for lo in range(0, c, SUBLANE)]
            out.append(jnp.concatenate(pieces, axis=0) * rows_ref[4])
        else:
            pieces = []
            for lo in range(0, c, 2 * m):
                ref_row = b_ref[pl.ds(lo + m - 1, 1), :]
                pieces.append(ref_row - b[lo:lo + m])
                pieces.append(b[lo + m:lo + 2 * m] - ref_row)
            out.append(jnp.concatenate(pieces, axis=0))
    return out


def _hg_chunk(q, k, v, g, sums_ref, masks_ref, rows_ref, b_ref, block):
    c = q.shape[0]
    g2 = g * LOG2E
    g_hi = g2.astype(BF16)
    g_lo = (g2 - g_hi.astype(F32)).astype(BF16)
    g_split = jnp.concatenate([g_hi, g_lo], axis=1)
    both = _dot(sums_ref[:c, :] if block == c else sums_ref[...], g_split)
    yield
    both = both[:, :LANE] + both[:, LANE:]
    if block == c:
        b = both
        b_rest = b[c - 1:c, :] - b
    else:
        b, b_rest = both[:c], both[c:]
    b_ref[...] = b
    level_scores = [_dot_nt(q, k)]
    for nd in _level_exponents(b, g2, b_ref, rows_ref, block):
        e = jnp.exp2(nd).astype(BF16)
        level_scores.append(_dot_nt(q * e, k * e))
    eb = jnp.exp2(b)
    q_dec = q * eb.astype(BF16)
    k_dec = k * jnp.exp2(b_rest).astype(BF16)
    yield
    scores = level_scores[0].astype(BF16) * masks_ref[0]
    for l, s in enumerate(level_scores[1:]):
        scores = scores + s.astype(BF16) * masks_ref[1 + l]
    o_intra = _dot(scores, v)
    return o_intra, q_dec, k_dec, eb


def _hg_chunk_fast(q, k, v, g, sums_ref, rows_ref):
    c = q.shape[0]
    blk = HG_FAST_BLOCK
    g2 = g * LOG2E
    g_hi = g2.astype(BF16)
    g_lo = (g2 - g_hi.astype(F32)).astype(BF16)
    b = _dot(sums_ref[:c, :], jnp.concatenate([g_hi, g_lo], axis=1))
    yield
    b = b[:, :LANE] + b[:, LANE:]
    qf = q.astype(F32)
    kf = k.astype(F32)
    q_blocks, k_blocks, cross = [], [], [jnp.zeros((blk, c), F32)]
    for i in range(c // blk):
        rows = slice(i * blk, (i + 1) * blk)
        if i == 0:
            d = b[rows]
        else:
            r = b[i * blk - 1:i * blk, :]
            d = b[rows] - r
        q_i = (qf[rows] * jnp.exp2(d)).astype(BF16)
        q_blocks.append(q_i)
        k_blocks.append((kf[rows] * jnp.exp2(-d)).astype(BF16))
        if i > 0:
            k_before = (kf[:i * blk] * jnp.exp2(r - b[:i * blk])).astype(BF16)
            k_before = jnp.concatenate(
                [k_before, jnp.zeros((c - i * blk, LANE), BF16)], axis=0)
            cross.append(_dot_nt(q_i, k_before))
    within = _dot_nt(jnp.concatenate(q_blocks, axis=0), jnp.concatenate(k_blocks, axis=0))
    eb = jnp.exp2(b)
    q_dec = (qf * eb).astype(BF16)
    k_dec = (kf * jnp.exp2(b[c - 1:c, :] - b)).astype(BF16)
    yield
    scores = within * rows_ref[5] + jnp.concatenate(cross, axis=0)
    o_intra = _dot(scores.astype(BF16), v)
    return o_intra, q_dec, k_dec, eb


def _hg_core_prompt_kernel(q_ref, k_ref, v_ref, g_ref, sg_ref, gn_ref, sums_ref, masks_ref,
                           rows_ref, blocks_ref, *rest, n_chunks, chunk, hps, dh):
    og_ref, s_ref, st_ref, b_ref = rest[-4:]
    st_ref[...] = jnp.zeros(st_ref.shape, F32)

    cpi = math.gcd(n_chunks, HG_CHUNKS_PER_ITER)

    def chunk_loop(fast):
        def body(it, carry):
            rows = [pl.ds(pl.multiple_of((it * cpi + cc) * chunk, chunk), chunk)
                    for cc in range(cpi)]
            heads = [slice(hh * dh, (hh + 1) * dh) for hh in range(hps)]
            pairs = [(hh, cc) for hh in range(hps) for cc in range(cpi)]
            vs = {(hh, cc): v_ref[rows[cc], heads[hh]] for hh, cc in pairs}
            chains = []
            for hh, cc in pairs:
                args = (q_ref[rows[cc], heads[hh]], k_ref[rows[cc], heads[hh]], vs[hh, cc],
                        g_ref[rows[cc], heads[hh]])
                if fast:
                    chains.append(_hg_chunk_fast(*args, sums_ref, rows_ref))
                else:
                    chains.append(_hg_chunk(*args, sums_ref, masks_ref, rows_ref,
                                            b_ref.at[hh * cpi + cc], chunk))
            intra = dict(zip(pairs, _run_staged(chains)))
            updates = {p: _dot_tn(vs[p], intra[p][2]) for p in pairs}
            states = [st_ref[hh] for hh in range(hps)]
            outs = {}
            for cc in range(cpi):
                inters = [_dot_nt(intra[hh, cc][1], states[hh].astype(BF16))
                          for hh in range(hps)]
                for hh in range(hps):
                    o_intra, _, _, eb = intra[hh, cc]
                    states[hh] = states[hh] * eb[chunk - 1:chunk, :] + updates[hh, cc]
                    outs[hh, cc] = _head_norm_gate(o_intra + inters[hh], gn_ref[hh],
                                                   sg_ref[rows[cc], heads[hh]])
            for hh in range(hps):
                st_ref[hh] = states[hh]
            for hh, cc in pairs:
                og_ref[rows[cc], heads[hh]] = outs[hh, cc]
            return carry

        lax.fori_loop(0, n_chunks // cpi, body, 0)

    block_sums = _dot(blocks_ref[...], jnp.abs(g_ref[...]).astype(BF16))
    fast_ok = jnp.max(block_sums) * LOG2E < HG_FAST_LIMIT
    pl.when(fast_ok)(lambda: chunk_loop(True))
    pl.when(jnp.logical_not(fast_ok))(lambda: chunk_loop(False))
    for hh in range(hps):
        s_ref[0, hh] = st_ref[hh].T


def _hg_core_prompt(q, k, v, g, sg, gn, j, n_layers, prev_state, n_seqs, seq_len):
    n, d = q.shape
    heads = HG_HEADS
    hps = HG_HEADS_PER_STEP
    dh = d // heads
    chunk = math.gcd(seq_len, HG_CHUNK)
    sums, masks, row_tabs = _hg_tables(chunk, chunk)
    block_of_row = np.arange(seq_len) // HG_FAST_BLOCK
    blocks = jnp.asarray(np.arange(seq_len // HG_FAST_BLOCK)[:, None] == block_of_row[None, :],
                         dtype=BF16)
    per_step = pl.BlockSpec((seq_len, hps * dh), lambda b, h: (b, h))
    s_spec, s_shape, extra_in, extra_spec = _state_out(
        prev_state, (n_layers, n_seqs, heads, dh, dh), j, (1, hps, dh, dh),
        lambda b, h: (b, h, 0, 0))
    n_in = 10 + len(extra_in)
    return pl.pallas_call(
        functools.partial(_hg_core_prompt_kernel, n_chunks=seq_len // chunk, chunk=chunk,
                          hps=hps, dh=dh),
        out_shape=(jax.ShapeDtypeStruct((n, d), BF16), s_shape),
        grid=(n_seqs, heads // hps),
        in_specs=[per_step, per_step, per_step, per_step, per_step,
                  pl.BlockSpec((None, hps, 1, dh), lambda b, h: (j, h, 0, 0)),
                  _resident(sums.shape), _resident(masks.shape),
                  _resident(row_tabs.shape), _resident(blocks.shape)] + extra_spec,
        out_specs=(per_step, s_spec),
        input_output_aliases={n_in - 1: 1} if extra_in else {},
        scratch_shapes=[pltpu.VMEM((hps, dh, dh), F32),
                        pltpu.VMEM((hps * HG_CHUNKS_PER_ITER, chunk, dh), F32)],
        compiler_params=_cparams(2),
        name="hg_core_prompt",
    )(q, k, v, g, sg, gn.reshape(gn.shape[0], heads, 1, dh), sums, masks, row_tabs, blocks,
      *extra_in)


def _hg_core_sample_kernel(q_ref, k_ref, v_ref, g_ref, sg_ref, gn_ref, sums_ref, masks_ref,
                           rows_ref, s0_ref, *rest, seqs, dec_len):
    og_ref, s_ref, b_ref = rest[-3:]
    rows = seqs * dec_len
    vb = v_ref[...]
    (o, q_dec, k_dec, eb), = _run_staged([_hg_chunk(
        q_ref[...], k_ref[...], vb, g_ref[...], sums_ref, masks_ref, rows_ref, b_ref, dec_len)])
    eb_t = eb.T
    k_dec_t = k_dec.astype(F32).T
    row_seq = lax.broadcasted_iota(jnp.int32, (rows, 1), 0) // dec_len
    col_seq = lax.broadcasted_iota(jnp.int32, (1, rows), 1) // dec_len
    inter = jnp.zeros(o.shape, F32)
    for i in range(seqs):
        s_old = s0_ref[i, 0]
        inter = jnp.where(row_seq == i, _dot(q_dec, s_old.astype(BF16)), inter)
        k_i = jnp.where(col_seq == i, k_dec_t, 0.0).astype(BF16)
        last = (i + 1) * dec_len - 1
        s_ref[i, 0] = s_old * eb_t[:, last:last + 1] + _dot(k_i, vb)
    og_ref[...] = _head_norm_gate(o + inter, gn_ref[0], sg_ref[...])


def _hg_core_sample(q, k, v, g, sg, gn, s0_all, j, prev_state, dec_len):
    heads = HG_HEADS
    n_layers, n_seqs = s0_all.shape[:2]
    dh = s0_all.shape[3]
    rows = math.gcd(n_seqs * dec_len, HG_CHUNK)
    seqs = rows // dec_len
    assert rows % dec_len == 0
    sums, masks, row_tabs = _hg_tables(rows, dec_len)
    per_head = pl.BlockSpec((rows, dh), lambda i, h: (i, h))
    state_in = pl.BlockSpec((None, seqs, 1, dh, dh), lambda i, h: (j, i, h, 0, 0))
    s_spec, s_shape, extra_in, extra_spec = _state_out(
        prev_state, s0_all.shape, j, (seqs, 1, dh, dh), lambda i, h: (i, h, 0, 0))
    n_in = 10 + len(extra_in)
    return pl.pallas_call(
        functools.partial(_hg_core_sample_kernel, seqs=seqs, dec_len=dec_len),
        out_shape=(jax.ShapeDtypeStruct(q.shape, BF16), s_shape),
        grid=(n_seqs // seqs, heads),
        in_specs=[per_head, per_head, per_head, per_head, per_head,
                  pl.BlockSpec((None, 1, 1, dh), lambda i, h: (j, h, 0, 0)),
                  _resident(sums.shape), _resident(masks.shape), _resident(row_tabs.shape),
                  state_in] + extra_spec,
        out_specs=(per_head, s_spec),
        input_output_aliases={n_in - 1: 1} if extra_in else {},
        scratch_shapes=[pltpu.VMEM((rows, dh), F32)],
        compiler_params=_cparams(2),
        name="hg_core_sample",
    )(q, k, v, g, sg, gn.reshape(gn.shape[0], heads, 1, dh), sums, masks, row_tabs, s0_all,
      *extra_in)


def kernel(x_prompt, x_sample, state_ret, state_hgrn, norm_gain, ffn_w_up, ffn_w_down,
           ret_w_in, ret_norm, ret_w_out, hg_w_in, hg_lb_logits, hg_norm, hg_w_out, final_norm):
    n_seqs, seq_len, d = x_prompt.shape
    dec_seqs, dec_len, _ = x_sample.shape
    depth = norm_gain.shape[0]
    n_ret, n_hg = state_ret.shape[0], state_hgrn.shape[0]

    xp = x_prompt.reshape(n_seqs * seq_len, d)
    xs = x_sample.reshape(dec_seqs * dec_len, d)
    w_up, w_down = ffn_w_up.astype(BF16), ffn_w_down.astype(BF16)
    ret_in, ret_out = ret_w_in.astype(BF16), ret_w_out.astype(BF16)
    hg_in, hg_out = hg_w_in.astype(BF16), hg_w_out.astype(BF16)

    ret_p = ret_s = hg_p = hg_s = None
    for layer in range(depth):
        xp, xs = _ffn(xp, xs, norm_gain, w_up, w_down, layer, 0)
        j = layer // 2
        if layer % 2 == 0:
            qp, qs, kp, ks, vp, vs, gp, gs = _ret_proj(
                xp, xs, norm_gain, ret_in, layer, j, seq_len, dec_len)
            ap, ret_p = _ret_core_prompt(qp, kp, vp, gp, ret_norm, j, n_ret, ret_p,
                                         n_seqs, seq_len)
            a_s, ret_s = _ret_core_sample(qs, ks, vs, gs, ret_norm, state_ret, j, ret_s, dec_len)
            mix = (ap, a_s, ret_out, j)
        else:
            qp, qs, kp, ks, gp, gs, vp, vs, sp, ss = _hg_proj(
                xp, xs, norm_gain, hg_in, hg_lb_logits, layer, j)
            ap, hg_p = _hg_core_prompt(qp, kp, vp, gp, sp, hg_norm, j, n_hg, hg_p,
                                       n_seqs, seq_len)
            a_s, hg_s = _hg_core_sample(qs, ks, vs, gs, ss, hg_norm, state_hgrn, j, hg_s,
                                        dec_len)
            mix = (ap, a_s, hg_out, j)
        xp, xs = _ffn(xp, xs, norm_gain, w_up, w_down, layer, 1, mix,
                      final_norm if layer == depth - 1 else None)
    return (xp.reshape(n_seqs, seq_len, d), xs.reshape(dec_seqs, dec_len, d),
            ret_p, ret_s, hg_p, hg_s)
```

```python
import functools
import math

import jax
import jax.numpy as jnp
import numpy as np
from jax import lax
from jax.experimental import pallas as pl
from jax.experimental.pallas import tpu as pltpu

F32 = jnp.float32
BF16 = jnp.bfloat16

EPS = 1e-6
ROPE_BASE = 10000.0
PAST_LEN = 16384
RET_HEADS = 4
RET_CHUNK = 256
RET_HEADS_PER_STEP = 2
RET_CHUNKS_PER_ITER = 2
HG_HEADS = 8
HG_CHUNK = 128
HG_HEADS_PER_STEP = 4
HG_CHUNKS_PER_ITER = 4
HG_FAST_BLOCK = 32
HG_FAST_LIMIT = 120.0
TOKEN_TILE = 1024
MIX_FFN_TILE = 512
FFN_COLS = 256
LANE = 128
SUBLANE = 8
VMEM_LIMIT = 56 * 1024 * 1024
LOG2E = 1.4426950408889634


def _cparams(n_axes):
    return pltpu.CompilerParams(
        dimension_semantics=("arbitrary",) * n_axes,
        vmem_limit_bytes=VMEM_LIMIT)


def _resident(shape, index=None):
    index = (0,) * len(shape) if index is None else index
    return pl.BlockSpec(shape, lambda *_: index, pipeline_mode=pl.Buffered(1))


def _dot(a, b):
    return jnp.dot(a, b, preferred_element_type=F32)


def _dot_nt(a, b):
    return lax.dot_general(a, b, (((1,), (1,)), ((), ())),
                           preferred_element_type=F32)


def _dot_tn(a, b):
    return lax.dot_general(a, b, (((0,), (0,)), ((), ())),
                           preferred_element_type=F32)


def _rmsnorm(x, gain):
    ms = jnp.mean(x * x, axis=-1, keepdims=True)
    return x * lax.rsqrt(ms + EPS) * gain


def _exp_neg(x):
    return jnp.exp2(x * (-LOG2E))


def _silu(x):
    return x * (1.0 / (1.0 + _exp_neg(x)))


def _run_staged(gens):
    results = [None] * len(gens)
    live = list(range(len(gens)))
    while live:
        still = []
        for i in live:
            try:
                next(gens[i])
                still.append(i)
            except StopIteration as stop:
                results[i] = stop.value
        live = still
    return results


def _token_specs(tm, ts, width):
    return (pl.BlockSpec((tm, width), lambda i: (i, 0)),
            pl.BlockSpec((ts, width), lambda i: (i, 0)))


def _token_tiles(xp, xs, tile=None):
    n_prompt, n_sample = xp.shape[0], xs.shape[0]
    tm = math.gcd(n_prompt, TOKEN_TILE if tile is None else tile)
    steps = n_prompt // tm
    ts = n_sample // steps
    assert ts * steps == n_sample and ts % 16 == 0
    return tm, ts, steps


def _load_normed(xp_ref, xs_ref, g_ref, xn_ref):
    tm = xp_ref.shape[0]
    xn_ref[:tm, :] = _rmsnorm(xp_ref[...], g_ref[...]).astype(BF16)
    xn_ref[tm:, :] = _rmsnorm(xs_ref[...], g_ref[...]).astype(BF16)


def _ffn_body(xp, xs, g_ref, wu_ref, wd_ref, op_ref, os_ref, xn_ref, h_ref, fg_ref=None):
    tm = xp.shape[0]
    d_ff = wd_ref.shape[0]
    xn_ref[:tm, :] = _rmsnorm(xp, g_ref[...]).astype(BF16)
    xn_ref[tm:, :] = _rmsnorm(xs, g_ref[...]).astype(BF16)
    xn = xn_ref[...]
    for c in range(d_ff // FFN_COLS):
        lo = c * FFN_COLS
        a = _dot(xn, wu_ref[:, lo:lo + FFN_COLS])
        b = _dot(xn, wu_ref[:, d_ff + lo:d_ff + lo + FFN_COLS])
        h_ref[:, lo:lo + FFN_COLS] = (_silu(a) * b).astype(BF16)
    y = _dot(h_ref[...], wd_ref[...])
    yp = xp + 0.5 * y[:tm]
    ys = xs + 0.5 * y[tm:]
    if fg_ref is not None:
        yp, ys = _rmsnorm(yp, fg_ref[...]), _rmsnorm(ys, fg_ref[...])
    op_ref[...] = yp
    os_ref[...] = ys


def _ffn_kernel(xp_ref, xs_ref, g_ref, wu_ref, wd_ref, op_ref, os_ref, xn_ref, h_ref):
    _ffn_body(xp_ref[...], xs_ref[...], g_ref, wu_ref, wd_ref, op_ref, os_ref, xn_ref, h_ref)


def _mix_ffn_kernel(xp_ref, xs_ref, ap_ref, as_ref, wo_ref, g_ref, wu_ref, wd_ref, *rest):
    fg_ref = rest[0] if len(rest) == 5 else None
    xp = xp_ref[...] + _dot(ap_ref[...], wo_ref[...])
    xs = xs_ref[...] + _dot(as_ref[...], wo_ref[...])
    _ffn_body(xp, xs, g_ref, wu_ref, wd_ref, *rest[-4:], fg_ref=fg_ref)


def _ffn(xp, xs, gains, w_up, w_down, layer, which, mix=None, final_gain=None):
    d = xp.shape[1]
    d_ff = w_down.shape[2]
    tm, ts, steps = _token_tiles(xp, xs, None if mix is None else MIX_FFN_TILE)
    assert d_ff % FFN_COLS == 0
    xspec = _token_specs(tm, ts, d)
    gain = _resident((None, None, 1, d), (layer, 2 * which, 0, 0))
    weights = [_resident((None, None, d, 2 * d_ff), (layer, which, 0, 0)),
               _resident((None, None, d_ff, d), (layer, which, 0, 0))]
    gains4 = gains.reshape(gains.shape[0], gains.shape[1], 1, d)
    if mix is None:
        body, in_specs, args = _ffn_kernel, [*xspec, gain, *weights], (xp, xs, gains4, w_up, w_down)
    else:
        ap, a_s, w_out, j = mix
        k = ap.shape[1]
        body = _mix_ffn_kernel
        in_specs = [*xspec, *_token_specs(tm, ts, k), _resident((None, k, d), (j, 0, 0)),
                    gain, *weights]
        args = (xp, xs, ap, a_s, w_out, gains4, w_up, w_down)
        if final_gain is not None:
            in_specs.append(_resident((1, d)))
            args += (final_gain.reshape(1, d),)
    return pl.pallas_call(
        body,
        out_shape=(jax.ShapeDtypeStruct(xp.shape, F32), jax.ShapeDtypeStruct(xs.shape, F32)),
        grid=(steps,),
        in_specs=in_specs,
        out_specs=xspec,
        scratch_shapes=[pltpu.VMEM((tm + ts, d), BF16), pltpu.VMEM((tm + ts, d_ff), BF16)],
        compiler_params=_cparams(1),
        name="ffn" if mix is None else "mix_ffn",
    )(*args)


def _ret_proj_kernel(xp_ref, xs_ref, g_ref, w_ref, cp_ref, sp_ref, cs_ref, ss_ref,
                     qp_ref, qs_ref, kp_ref, ks_ref, vp_ref, vs_ref, gp_ref, gs_ref, xn_ref,
                     *, dk, dv, heads):
    tm = xp_ref.shape[0]
    _load_normed(xp_ref, xs_ref, g_ref, xn_ref)
    xn = xn_ref[...]
    cos = jnp.concatenate([cp_ref[...], cs_ref[...]], axis=0)
    sin = jnp.concatenate([sp_ref[...], ss_ref[...]], axis=0)
    half = dk // 2
    qk = heads * dk
    k_scale = dk ** -0.5

    def put(p_ref, s_ref, lo, val):
        val = val.astype(BF16)
        p_ref[:, lo:lo + val.shape[1]] = val[:tm]
        s_ref[:, lo:lo + val.shape[1]] = val[tm:]

    for h in range(heads):
        for base, p_ref, s_ref, scale in ((0, qp_ref, qs_ref, None), (qk, kp_ref, ks_ref, k_scale)):
            t = _dot(xn, w_ref[:, base + h * dk:base + (h + 1) * dk])
            x1, x2 = t[:, :half], t[:, half:]
            r1 = x1 * cos - x2 * sin
            r2 = x1 * sin + x2 * cos
            if scale is not None:
                r1, r2 = r1 * scale, r2 * scale
            put(p_ref, s_ref, h * dk, r1)
            put(p_ref, s_ref, h * dk + half, r2)
    for h in range(heads):
        lo = 2 * qk + h * dv
        put(vp_ref, vs_ref, h * dv, _dot(xn, w_ref[:, lo:lo + dv]))
        lo = 2 * qk + heads * dv + h * dv
        put(gp_ref, gs_ref, h * dv, _silu(_dot(xn, w_ref[:, lo:lo + dv])))


def _rope_tables(pos, half):
    inv_freq = ROPE_BASE ** (-jnp.arange(half, dtype=F32) / half)
    ang = pos[:, None] * inv_freq[None, :]
    return jnp.cos(ang), jnp.sin(ang)


def _ret_proj(xp, xs, gains, w_in, layer, j, seq_len, dec_len):
    d = xp.shape[1]
    heads = RET_HEADS
    dk = d // heads
    dv = 2 * dk
    half = dk // 2
    tm, ts, steps = _token_tiles(xp, xs)
    assert seq_len % tm == 0 and ts % dec_len == 0
    tiles_per_seq = seq_len // tm
    cos_p, sin_p = _rope_tables(jnp.arange(seq_len, dtype=F32), half)
    cos_s, sin_s = _rope_tables(
        jnp.tile(PAST_LEN + jnp.arange(dec_len, dtype=F32), ts // dec_len), half)
    tab_p = pl.BlockSpec((tm, half), lambda i: (i % tiles_per_seq, 0))
    widths = (heads * dk, heads * dk, heads * dv, heads * dv)
    out_shape, out_specs = [], []
    for w in widths:
        out_shape += [jax.ShapeDtypeStruct((xp.shape[0], w), BF16),
                      jax.ShapeDtypeStruct((xs.shape[0], w), BF16)]
        out_specs += list(_token_specs(tm, ts, w))
    return pl.pallas_call(
        functools.partial(_ret_proj_kernel, dk=dk, dv=dv, heads=heads),
        out_shape=tuple(out_shape),
        grid=(steps,),
        in_specs=[*_token_specs(tm, ts, d),
                  _resident((None, None, 1, d), (layer, 1, 0, 0)),
                  _resident((None,) + w_in.shape[1:], (j, 0, 0)),
                  tab_p, tab_p, _resident((ts, half)), _resident((ts, half))],
        out_specs=tuple(out_specs),
        scratch_shapes=[pltpu.VMEM((tm + ts, d), BF16)],
        compiler_params=_cparams(1),
        name="ret_proj",
    )(xp, xs, gains.reshape(gains.shape[0], gains.shape[1], 1, d), w_in,
      cos_p, sin_p, cos_s, sin_s)


def _head_norm_gate(o, gn, sg):
    ms = jnp.mean(o * o, axis=-1, keepdims=True)
    return (o * lax.rsqrt(ms + EPS) * gn * sg.astype(F32)).astype(BF16)


def _ret_prompt_step(q_ref, k_ref, v_ref, sg_ref, gn_ref, dm_ref, qd_ref, kd_ref, cd_ref,
                     og_ref, s_ref, *, it, cpi, chunk, hps, dk, dv):
    h0 = pl.program_id(1) * hps

    @pl.when(it == 0)
    def _():
        for hh in range(hps):
            s_ref[0, hh] = jnp.zeros((dk, dv), F32)

    def head_chain(hh, rows):
        qs = [q_ref[r, hh * dk:(hh + 1) * dk] for r in rows]
        ks = [k_ref[r, hh * dk:(hh + 1) * dk] for r in rows]
        vs = [v_ref[r, hh * dv:(hh + 1) * dv] for r in rows]
        sgs = [sg_ref[r, hh * dv:(hh + 1) * dv] for r in rows]
        s = s_ref[0, hh]
        cd = cd_ref[h0 + hh]
        scores = [_dot_nt(qc, kc) for qc, kc in zip(qs, ks)]
        updates = [_dot_tn((kc.astype(F32) * kd_ref[hh]).astype(BF16), vc)
                   for kc, vc in zip(ks, vs)]
        yield
        intra = [_dot((sc * dm_ref[hh]).astype(BF16), vc) for sc, vc in zip(scores, vs)]
        inter = []
        for qc, upd in zip(qs, updates):
            inter.append(_dot(qc, s.astype(BF16)))
            s = s * cd + upd
        yield
        return s, [_head_norm_gate(oi + oc * qd_ref[hh], gn_ref[hh], sgc)
                   for oi, oc, sgc in zip(intra, inter, sgs)]

    rows = [pl.ds(pl.multiple_of((it * cpi + cc) * chunk, chunk), chunk) for cc in range(cpi)]
    results = _run_staged([head_chain(hh, rows) for hh in range(hps)])
    for hh, (s_new, ogs) in enumerate(results):
        s_ref[0, hh] = s_new
        for r, og in zip(rows, ogs):
            og_ref[r, hh * dv:(hh + 1) * dv] = og


def _ret_decay_tables(chunk):
    heads = RET_HEADS
    log_gamma = jnp.log(1.0 - jnp.power(2.0, -5.0 - jnp.arange(heads, dtype=F32)))
    idx = jnp.arange(chunk, dtype=F32)
    diff = idx[:, None] - idx[None, :]
    lg = log_gamma[:, None, None]
    decay_mat = jnp.where(diff[None] >= 0, jnp.exp(diff[None] * lg), 0.0)
    q_decay = jnp.exp((idx[None, :] + 1.0) * log_gamma[:, None])
    k_decay = jnp.exp((chunk - 1.0 - idx[None, :]) * log_gamma[:, None])
    chunk_decay = jnp.exp(chunk * log_gamma)
    return decay_mat, q_decay, k_decay, chunk_decay


def _state_out(prev, shape, j, block, index_map):
    spec = pl.BlockSpec((None,) + block, lambda *ids: (j,) + index_map(*ids))
    extra_in = [] if prev is None else [prev]
    extra_spec = [] if prev is None else [pl.BlockSpec(memory_space=pl.ANY)]
    return spec, jax.ShapeDtypeStruct(shape, F32), extra_in, extra_spec


def _ret_sample_step(q_ref, k_ref, v_ref, sg_ref, gn_ref, dm_ref, qd_ref, kd_ref, cd_ref,
                     s0_ref, og_ref, s_ref, *, h, seqs, dec_len):
    rows = seqs * dec_len
    qb = q_ref[...]
    vb = v_ref[...]
    kdec = k_ref[...].astype(F32) * kd_ref[0]
    cd = cd_ref[h]
    row_seq = lax.broadcasted_iota(jnp.int32, (rows, 1), 0) // dec_len
    scores = _dot_nt(qb, k_ref[...]) * dm_ref[0]
    o = _dot(scores.astype(BF16), vb)
    inter = jnp.zeros(o.shape, F32)
    for i in range(seqs):
        s_old = s0_ref[i, 0]
        mine = row_seq == i
        inter = jnp.where(mine, _dot(qb, s_old.astype(BF16)), inter)
        k_i = jnp.where(mine, kdec, 0.0).astype(BF16)
        s_ref[i, 0] = s_old * cd + _dot_tn(k_i, vb)
    o = o + inter * qd_ref[0]
    og_ref[...] = _head_norm_gate(o, gn_ref[0], sg_ref[...])


def _ret_core_kernel(*refs, n_prev, heads, cpi, chunk, hps, dk, dv, seqs, dec_len):
    prompt_in, sample_in = refs[:9], refs[9:19]
    og_p, s_p, og_s, s_s = refs[19 + n_prev:]
    n_h, n_it = pl.num_programs(1), pl.num_programs(2)
    unit = (pl.program_id(0) * n_h + pl.program_id(1)) * n_it + pl.program_id(2)
    _ret_sample_step(*sample_in, og_s, s_s, h=unit % heads, seqs=seqs, dec_len=dec_len)
    _ret_prompt_step(*prompt_in, og_p, s_p, it=pl.program_id(2), cpi=cpi, chunk=chunk,
                     hps=hps, dk=dk, dv=dv)


def _ret_core(prompt, sample, gn, s0_all, j, prev_prompt, prev_sample, n_seqs, seq_len, dec_len):
    qp, kp, vp, gp = prompt
    qs, ks, vs, gs = sample
    heads = RET_HEADS
    hps = RET_HEADS_PER_STEP
    n_layers, dec_seqs = s0_all.shape[:2]
    dk, dv = s0_all.shape[3], s0_all.shape[4]
    gn4 = gn.reshape(gn.shape[0], heads, 1, dv)

    chunk = math.gcd(seq_len, RET_CHUNK)
    n_chunks = seq_len // chunk
    cpi = math.gcd(n_chunks, RET_CHUNKS_PER_ITER)
    grid = (n_seqs, heads // hps, n_chunks // cpi)
    dm, qd, kd, cd = _ret_decay_tables(chunk)
    per_step = lambda w: pl.BlockSpec((seq_len, hps * w), lambda b, h, t: (b, h))
    tab = lambda shape: pl.BlockSpec((hps,) + shape, lambda b, h, t: (h, 0, 0))
    sp_spec, sp_shape, prev_p, prev_p_spec = _state_out(
        prev_prompt, (n_layers, n_seqs, heads, dk, dv), j, (1, hps, dk, dv),
        lambda b, h, t: (b, h, 0, 0))
    prompt_specs = [per_step(dk), per_step(dk), per_step(dv), per_step(dv),
                    pl.BlockSpec((None, hps, 1, dv), lambda b, h, t: (j, h, 0, 0)),
                    tab((chunk, chunk)), tab((chunk, 1)), tab((chunk, 1)),
                    pl.BlockSpec(memory_space=pltpu.SMEM)]
    prompt_args = (qp, kp, vp, gp, gn4, dm, qd[:, :, None], kd[:, :, None], cd)

    n_units = grid[0] * grid[1] * grid[2]
    seqs = dec_seqs * heads // n_units
    rows = seqs * dec_len
    assert seqs * n_units == dec_seqs * heads and rows % 16 == 0
    assert math.gcd(dec_len, RET_CHUNK) == dec_len
    dm_s, qd_s, kd_s, cd_s = _ret_decay_tables(dec_len)
    seq_id = np.arange(rows) // dec_len
    same = jnp.asarray(seq_id[:, None] == seq_id[None, :])
    dm_blk = jnp.where(same[None], jnp.tile(dm_s, (1, seqs, seqs)), 0.0)
    qd_blk = jnp.tile(qd_s, (1, seqs))[:, :, None]
    kd_blk = jnp.tile(kd_s, (1, seqs))[:, :, None]

    def unit(b, h, t):
        u = (b * grid[1] + h) * grid[2] + t
        return u // heads, u % heads

    per_unit = lambda w: pl.BlockSpec((rows, w), lambda b, h, t: unit(b, h, t))
    tab_s = lambda shape: pl.BlockSpec((1,) + shape, lambda b, h, t: (unit(b, h, t)[1], 0, 0))
    state_in = pl.BlockSpec((None, seqs, 1, dk, dv),
                            lambda b, h, t: (j,) + unit(b, h, t) + (0, 0))
    ss_spec, ss_shape, prev_s, prev_s_spec = _state_out(
        prev_sample, s0_all.shape, j, (seqs, 1, dk, dv),
        lambda b, h, t: unit(b, h, t) + (0, 0))
    sample_specs = [per_unit(dk), per_unit(dk), per_unit(dv), per_unit(dv),
                    pl.BlockSpec((None, 1, 1, dv),
                                 lambda b, h, t: (j, unit(b, h, t)[1], 0, 0)),
                    tab_s((rows, rows)), tab_s((rows, 1)), tab_s((rows, 1)),
                    pl.BlockSpec(memory_space=pltpu.SMEM), state_in]
    sample_args = (qs, ks, vs, gs, gn4, dm_blk, qd_blk, kd_blk, cd_s, s0_all)

    n_prev = len(prev_p) + len(prev_s)
    aliases = {}
    if prev_p:
        aliases[19] = 1
    if prev_s:
        aliases[19 + len(prev_p)] = 3
    return pl.pallas_call(
        functools.partial(_ret_core_kernel, n_prev=n_prev, heads=heads, cpi=cpi, chunk=chunk,
                          hps=hps, dk=dk, dv=dv, seqs=seqs, dec_len=dec_len),
        out_shape=(jax.ShapeDtypeStruct((qp.shape[0], heads * dv), BF16), sp_shape,
                   jax.ShapeDtypeStruct((qs.shape[0], heads * dv), BF16), ss_shape),
        grid=grid,
        in_specs=prompt_specs + sample_specs + prev_p_spec + prev_s_spec,
        out_specs=(per_step(dv), sp_spec, per_unit(dv), ss_spec),
        input_output_aliases=aliases,
        compiler_params=_cparams(3),
        name="ret_core",
    )(*prompt_args, *sample_args, *prev_p, *prev_s)


def _hg_proj_kernel(xp_ref, xs_ref, g_ref, w_ref, lbl_ref,
                    qp_ref, qs_ref, kp_ref, ks_ref, gp_ref, gs_ref, vp_ref, vs_ref,
                    sp_ref, ss_ref, xn_ref, *, layer, cols):
    tm, d = xp_ref.shape
    _load_normed(xp_ref, xs_ref, g_ref, xn_ref)
    xn = xn_ref[...]
    logits = lbl_ref[...]
    e = jnp.exp(logits - jnp.max(logits, axis=0, keepdims=True))
    sm = e / jnp.sum(e, axis=0, keepdims=True)
    cum = sm[0:1]
    for i in range(1, layer + 1):
        cum = cum + sm[i:i + 1]
    lb = cum - sm[0:1]
    log_lb = jnp.log(lb)
    log_1m_lb = jnp.log1p(-lb)

    def put(p_ref, s_ref, lo, val):
        p_ref[:, lo:lo + cols] = val[:tm].astype(p_ref.dtype)
        s_ref[:, lo:lo + cols] = val[tm:].astype(s_ref.dtype)

    for c in range(d // cols):
        lo = c * cols
        put(qp_ref, qs_ref, lo, _silu(_dot(xn, w_ref[:, lo:lo + cols])))
        z = _dot(xn, w_ref[:, d + lo:d + lo + cols])
        ez = _exp_neg(jnp.abs(z))
        ez1 = 1.0 + ez
        log_sig = jnp.minimum(z, 0.0) - jnp.log(ez1)
        a = log_lb[:, lo:lo + cols]
        b = log_1m_lb[:, lo:lo + cols] + log_sig
        delta = a - b
        g = jnp.where(delta != delta, a + b,
                      jnp.maximum(a, b) + jnp.log(1.0 + _exp_neg(jnp.abs(delta))))
        put(gp_ref, gs_ref, lo, g)
        sig_neg = jnp.where(z >= 0.0, ez, 1.0) * (1.0 / ez1)
        put(kp_ref, ks_ref, lo, (1.0 - lb[:, lo:lo + cols]) * sig_neg)
        put(vp_ref, vs_ref, lo, _dot(xn, w_ref[:, 2 * d + lo:2 * d + lo + cols]))
        put(sp_ref, ss_ref, lo, _silu(_dot(xn, w_ref[:, 3 * d + lo:3 * d + lo + cols])))


def _hg_proj(xp, xs, gains, w_in, lb_logits, layer, j):
    d = xp.shape[1]
    tm, ts, steps = _token_tiles(xp, xs)
    out_shape, out_specs = [], []
    for dt in (BF16, BF16, F32, BF16, BF16):
        out_shape += [jax.ShapeDtypeStruct(xp.shape, dt), jax.ShapeDtypeStruct(xs.shape, dt)]
        out_specs += list(_token_specs(tm, ts, d))
    return pl.pallas_call(
        functools.partial(_hg_proj_kernel, layer=j, cols=2 * LANE),
        out_shape=tuple(out_shape),
        grid=(steps,),
        in_specs=[*_token_specs(tm, ts, d),
                  _resident((None, None, 1, d), (layer, 1, 0, 0)),
                  _resident((None,) + w_in.shape[1:], (j, 0, 0)),
                  _resident(lb_logits.shape)],
        out_specs=tuple(out_specs),
        scratch_shapes=[pltpu.VMEM((tm + ts, d), BF16)],
        compiler_params=_cparams(1),
        name="hg_proj",
    )(xp, xs, gains.reshape(gains.shape[0], gains.shape[1], 1, d), w_in, lb_logits)


def _hg_tables(chunk, block):
    n_levels = int(math.log2(block))
    assert 2 ** n_levels == block and chunk % block == 0 and chunk == LANE
    t = np.arange(chunk)[:, None]
    s = np.arange(chunk)[None, :]
    same_run = (t // block) == (s // block)
    sums = np.concatenate([same_run & (s <= t), same_run & (s > t)], axis=0)
    masks = [t == s]
    for l in range(n_levels):
        m = 2 ** l
        mid = (t // (2 * m)) * (2 * m) + m
        masks.append(((t // (2 * m)) == (s // (2 * m))) & (t >= mid) & (s < mid))
    r = np.arange(chunk)
    rows = np.stack([r % 2 == 1, r % 4 == 0, r % 4 >= 2, r % 4 == 3,
                     np.where(r % 8 >= 4, 1.0, -1.0)]).astype(np.float32)
    rows = np.broadcast_to(rows[:, :, None], rows.shape + (LANE,))
    block_causal = ((t // HG_FAST_BLOCK) == (s // HG_FAST_BLOCK)) & (s <= t)
    rows = np.concatenate([rows, block_causal[None].astype(np.float32)])
    sums = jnp.asarray(sums.astype(np.float32), dtype=BF16)
    masks = jnp.asarray(np.stack(masks).astype(np.float32), dtype=BF16)
    return sums, masks, jnp.asarray(rows)


def _level_exponents(b, g2, b_ref, rows_ref, block):
    c = b.shape[0]
    out = []
    n_levels = int(math.log2(block))
    for l in range(n_levels):
        m = 2 ** l
        if m == 1:
            out.append(g2 * rows_ref[0])
        elif m == 2:
            nxt = pltpu.roll(g2, c - 1, 0)
            prv = pltpu.roll(g2, 1, 0)
            out.append(nxt * rows_ref[1] + g2 * rows_ref[2] + prv * rows_ref[3])
        elif 2 * m == SUBLANE:
            pieces = [b[lo:lo + SUBLANE] - b_ref[pl.ds(lo + m - 1, 1), :]
                      for lo in range(0, c, SUBLANE)]
            out.append(jnp.concatenate(pieces, axis=0) * rows_ref[4])
        else:
            pieces = []
            for lo in range(0, c, 2 * m):
                ref_row = b_ref[pl.ds(lo + m - 1, 1), :]
                pieces.append(ref_row - b[lo:lo + m])
                pieces.append(b[lo + m:lo + 2 * m] - ref_row)
            out.append(jnp.concatenate(pieces, axis=0))
    return out


def _hg_chunk(q, k, v, g, sums_ref, masks_ref, rows_ref, b_ref, block):
    c = q.shape[0]
    g2 = g * LOG2E
    g_hi = g2.astype(BF16)
    g_lo = (g2 - g_hi.astype(F32)).astype(BF16)
    g_split = jnp.concatenate([g_hi, g_lo], axis=1)
    both = _dot(sums_ref[:c, :] if block == c else sums_ref[...], g_split)
    yield
    both = both[:, :LANE] + both[:, LANE:]
    if block == c:
        b = both
        b_rest = b[c - 1:c, :] - b
    else:
        b, b_rest = both[:c], both[c:]
    b_ref[...] = b
    level_scores = [_dot_nt(q, k)]
    for nd in _level_exponents(b, g2, b_ref, rows_ref, block):
        e = jnp.exp2(nd).astype(BF16)
        level_scores.append(_dot_nt(q * e, k * e))
    eb = jnp.exp2(b)
    q_dec = q * eb.astype(BF16)
    k_dec = k * jnp.exp2(b_rest).astype(BF16)
    yield
    scores = level_scores[0].astype(BF16) * masks_ref[0]
    for l, s in enumerate(level_scores[1:]):
        scores = scores + s.astype(BF16) * masks_ref[1 + l]
    o_intra = _dot(scores, v)
    return o_intra, q_dec, k_dec, eb


def _hg_chunk_fast(q, k, v, g, sums_ref, rows_ref):
    c = q.shape[0]
    blk = HG_FAST_BLOCK
    g2 = g * LOG2E
    g_hi = g2.astype(BF16)
    g_lo = (g2 - g_hi.astype(F32)).astype(BF16)
    b = _dot(sums_ref[:c, :], jnp.concatenate([g_hi, g_lo], axis=1))
    yield
    b = b[:, :LANE] + b[:, LANE:]
    qf = q.astype(F32)
    kf = k.astype(F32)
    q_blocks, k_blocks, cross = [], [], [jnp.zeros((blk, c), F32)]
    for i in range(c // blk):
        rows = slice(i * blk, (i + 1) * blk)
        if i == 0:
            d = b[rows]
        else:
            r = b[i * blk - 1:i * blk, :]
            d = b[rows] - r
        q_i = (qf[rows] * jnp.exp2(d)).astype(BF16)
        q_blocks.append(q_i)
        k_blocks.append((kf[rows] * jnp.exp2(-d)).astype(BF16))
        if i > 0:
            k_before = (kf[:i * blk] * jnp.exp2(r - b[:i * blk])).astype(BF16)
            k_before = jnp.concatenate(
                [k_before, jnp.zeros((c - i * blk, LANE), BF16)], axis=0)
            cross.append(_dot_nt(q_i, k_before))
    within = _dot_nt(jnp.concatenate(q_blocks, axis=0), jnp.concatenate(k_blocks, axis=0))
    eb = jnp.exp2(b)
    q_dec = (qf * eb).astype(BF16)
    k_dec = (kf * jnp.exp2(b[c - 1:c, :] - b)).astype(BF16)
    yield
    scores = within * rows_ref[5] + jnp.concatenate(cross, axis=0)
    o_intra = _dot(scores.astype(BF16), v)
    return o_intra, q_dec, k_dec, eb


def _hg_core_prompt_kernel(q_ref, k_ref, v_ref, g_ref, sg_ref, gn_ref, sums_ref, masks_ref,
                           rows_ref, blocks_ref, *rest, n_chunks, chunk, hps, dh):
    og_ref, s_ref, st_ref, b_ref = rest[-4:]
    st_ref[...] = jnp.zeros(st_ref.shape, F32)

    cpi = math.gcd(n_chunks, HG_CHUNKS_PER_ITER)

    def chunk_loop(fast):
        def body(it, carry):
            rows = [pl.ds(pl.multiple_of((it * cpi + cc) * chunk, chunk), chunk)
                    for cc in range(cpi)]
            heads = [slice(hh * dh, (hh + 1) * dh) for hh in range(hps)]
            pairs = [(hh, cc) for hh in range(hps) for cc in range(cpi)]
            vs = {(hh, cc): v_ref[rows[cc], heads[hh]] for hh, cc in pairs}
            chains = []
            for hh, cc in pairs:
                args = (q_ref[rows[cc], heads[hh]], k_ref[rows[cc], heads[hh]], vs[hh, cc],
                        g_ref[rows[cc], heads[hh]])
                if fast:
                    chains.append(_hg_chunk_fast(*args, sums_ref, rows_ref))
                else:
                    chains.append(_hg_chunk(*args, sums_ref, masks_ref, rows_ref,
                                            b_ref.at[hh * cpi + cc], chunk))
            intra = dict(zip(pairs, _run_staged(chains)))
            updates = {p: _dot_tn(vs[p], intra[p][2]) for p in pairs}
            states = [st_ref[hh] for hh in range(hps)]
            outs = {}
            for cc in range(cpi):
                inters = [_dot_nt(intra[hh, cc][1], states[hh].astype(BF16))
                          for hh in range(hps)]
                for hh in range(hps):
                    o_intra, _, _, eb = intra[hh, cc]
                    states[hh] = states[hh] * eb[chunk - 1:chunk, :] + updates[hh, cc]
                    outs[hh, cc] = _head_norm_gate(o_intra + inters[hh], gn_ref[hh],
                                                   sg_ref[rows[cc], heads[hh]])
            for hh in range(hps):
                st_ref[hh] = states[hh]
            for hh, cc in pairs:
                og_ref[rows[cc], heads[hh]] = outs[hh, cc]
            return carry

        lax.fori_loop(0, n_chunks // cpi, body, 0)

    block_sums = _dot(blocks_ref[...], jnp.abs(g_ref[...]).astype(BF16))
    fast_ok = jnp.max(block_sums) * LOG2E < HG_FAST_LIMIT
    pl.when(fast_ok)(lambda: chunk_loop(True))
    pl.when(jnp.logical_not(fast_ok))(lambda: chunk_loop(False))
    for hh in range(hps):
        s_ref[0, hh] = st_ref[hh].T


def _hg_core_prompt(q, k, v, g, sg, gn, j, n_layers, prev_state, n_seqs, seq_len):
    n, d = q.shape
    heads = HG_HEADS
    hps = HG_HEADS_PER_STEP
    dh = d // heads
    chunk = math.gcd(seq_len, HG_CHUNK)
    sums, masks, row_tabs = _hg_tables(chunk, chunk)
    block_of_row = np.arange(seq_len) // HG_FAST_BLOCK
    blocks = jnp.asarray(np.arange(seq_len // HG_FAST_BLOCK)[:, None] == block_of_row[None, :],
                         dtype=BF16)
    per_step = pl.BlockSpec((seq_len, hps * dh), lambda b, h: (b, h))
    s_spec, s_shape, extra_in, extra_spec = _state_out(
        prev_state, (n_layers, n_seqs, heads, dh, dh), j, (1, hps, dh, dh),
        lambda b, h: (b, h, 0, 0))
    n_in = 10 + len(extra_in)
    return pl.pallas_call(
        functools.partial(_hg_core_prompt_kernel, n_chunks=seq_len // chunk, chunk=chunk,
                          hps=hps, dh=dh),
        out_shape=(jax.ShapeDtypeStruct((n, d), BF16), s_shape),
        grid=(n_seqs, heads // hps),
        in_specs=[per_step, per_step, per_step, per_step, per_step,
                  pl.BlockSpec((None, hps, 1, dh), lambda b, h: (j, h, 0, 0)),
                  _resident(sums.shape), _resident(masks.shape),
                  _resident(row_tabs.shape), _resident(blocks.shape)] + extra_spec,
        out_specs=(per_step, s_spec),
        input_output_aliases={n_in - 1: 1} if extra_in else {},
        scratch_shapes=[pltpu.VMEM((hps, dh, dh), F32),
                        pltpu.VMEM((hps * HG_CHUNKS_PER_ITER, chunk, dh), F32)],
        compiler_params=_cparams(2),
        name="hg_core_prompt",
    )(q, k, v, g, sg, gn.reshape(gn.shape[0], heads, 1, dh), sums, masks, row_tabs, blocks,
      *extra_in)


def _hg_core_sample_kernel(q_ref, k_ref, v_ref, g_ref, sg_ref, gn_ref, sums_ref, masks_ref,
                           rows_ref, s0_ref, *rest, seqs, dec_len):
    og_ref, s_ref, b_ref = rest[-3:]
    rows = seqs * dec_len
    vb = v_ref[...]
    (o, q_dec, k_dec, eb), = _run_staged([_hg_chunk(
        q_ref[...], k_ref[...], vb, g_ref[...], sums_ref, masks_ref, rows_ref, b_ref, dec_len)])
    eb_t = eb.T
    k_dec_t = k_dec.astype(F32).T
    row_seq = lax.broadcasted_iota(jnp.int32, (rows, 1), 0) // dec_len
    col_seq = lax.broadcasted_iota(jnp.int32, (1, rows), 1) // dec_len
    inter = jnp.zeros(o.shape, F32)
    for i in range(seqs):
        s_old = s0_ref[i, 0]
        inter = jnp.where(row_seq == i, _dot(q_dec, s_old.astype(BF16)), inter)
        k_i = jnp.where(col_seq == i, k_dec_t, 0.0).astype(BF16)
        last = (i + 1) * dec_len - 1
        s_ref[i, 0] = s_old * eb_t[:, last:last + 1] + _dot(k_i, vb)
    og_ref[...] = _head_norm_gate(o + inter, gn_ref[0], sg_ref[...])


def _hg_core_sample(q, k, v, g, sg, gn, s0_all, j, prev_state, dec_len):
    heads = HG_HEADS
    n_layers, n_seqs = s0_all.shape[:2]
    dh = s0_all.shape[3]
    rows = math.gcd(n_seqs * dec_len, HG_CHUNK)
    seqs = rows // dec_len
    assert rows % dec_len == 0
    sums, masks, row_tabs = _hg_tables(rows, dec_len)
    per_head = pl.BlockSpec((rows, dh), lambda i, h: (i, h))
    state_in = pl.BlockSpec((None, seqs, 1, dh, dh), lambda i, h: (j, i, h, 0, 0))
    s_spec, s_shape, extra_in, extra_spec = _state_out(
        prev_state, s0_all.shape, j, (seqs, 1, dh, dh), lambda i, h: (i, h, 0, 0))
    n_in = 10 + len(extra_in)
    return pl.pallas_call(
        functools.partial(_hg_core_sample_kernel, seqs=seqs, dec_len=dec_len),
        out_shape=(jax.ShapeDtypeStruct(q.shape, BF16), s_shape),
        grid=(n_seqs // seqs, heads),
        in_specs=[per_head, per_head, per_head, per_head, per_head,
                  pl.BlockSpec((None, 1, 1, dh), lambda i, h: (j, h, 0, 0)),
                  _resident(sums.shape), _resident(masks.shape), _resident(row_tabs.shape),
                  state_in] + extra_spec,
        out_specs=(per_head, s_spec),
        input_output_aliases={n_in - 1: 1} if extra_in else {},
        scratch_shapes=[pltpu.VMEM((rows, dh), F32)],
        compiler_params=_cparams(2),
        name="hg_core_sample",
    )(q, k, v, g, sg, gn.reshape(gn.shape[0], heads, 1, dh), sums, masks, row_tabs, s0_all,
      *extra_in)


def kernel(x_prompt, x_sample, state_ret, state_hgrn, norm_gain, ffn_w_up, ffn_w_down,
           ret_w_in, ret_norm, ret_w_out, hg_w_in, hg_lb_logits, hg_norm, hg_w_out, final_norm):
    n_seqs, seq_len, d = x_prompt.shape
    dec_seqs, dec_len, _ = x_sample.shape
    depth = norm_gain.shape[0]
    n_ret, n_hg = state_ret.shape[0], state_hgrn.shape[0]

    xp = x_prompt.reshape(n_seqs * seq_len, d)
    xs = x_sample.reshape(dec_seqs * dec_len, d)
    w_up, w_down = ffn_w_up.astype(BF16), ffn_w_down.astype(BF16)
    ret_in, ret_out = ret_w_in.astype(BF16), ret_w_out.astype(BF16)
    hg_in, hg_out = hg_w_in.astype(BF16), hg_w_out.astype(BF16)

    ret_p = ret_s = hg_p = hg_s = None
    for layer in range(depth):
        xp, xs = _ffn(xp, xs, norm_gain, w_up, w_down, layer, 0)
        j = layer // 2
        if layer % 2 == 0:
            qp, qs, kp, ks, vp, vs, gp, gs = _ret_proj(
                xp, xs, norm_gain, ret_in, layer, j, seq_len, dec_len)
            ap, ret_p, a_s, ret_s = _ret_core(
                (qp, kp, vp, gp), (qs, ks, vs, gs), ret_norm, state_ret, j, ret_p, ret_s,
                n_seqs, seq_len, dec_len)
            mix = (ap, a_s, ret_out, j)
        else:
            qp, qs, kp, ks, gp, gs, vp, vs, sp, ss = _hg_proj(
                xp, xs, norm_gain, hg_in, hg_lb_logits, layer, j)
            ap, hg_p = _hg_core_prompt(qp, kp, vp, gp, sp, hg_norm, j, n_hg, hg_p,
                                       n_seqs, seq_len)
            a_s, hg_s = _hg_core_sample(qs, ks, vs, gs, ss, hg_norm, state_hgrn, j, hg_s,
                                        dec_len)
            mix = (ap, a_s, hg_out, j)
        xp, xs = _ffn(xp, xs, norm_gain, w_up, w_down, layer, 1, mix,
                      final_norm if layer == depth - 1 else None)
    return (xp.reshape(n_seqs, seq_len, d), xs.reshape(dec_seqs, dec_len, d),
            ret_p, ret_s, hg_p, hg_s)
```

```python
import functools
import math

import jax
import jax.numpy as jnp
import numpy as np
from jax import lax
from jax.experimental import pallas as pl
from jax.experimental.pallas import tpu as pltpu

F32 = jnp.float32
BF16 = jnp.bfloat16

EPS = 1e-6
ROPE_BASE = 10000.0
PAST_LEN = 16384
RET_HEADS = 4
RET_CHUNK = 256
RET_HEADS_PER_STEP = 2
RET_CHUNKS_PER_ITER = 2
HG_HEADS = 8
HG_CHUNK = 128
HG_HEADS_PER_STEP = 4
HG_CHUNKS_PER_ITER = 4
HG_FAST_BLOCK = 32
HG_FAST_LIMIT = 120.0
TOKEN_TILE = 1024
MIX_FFN_TILE = 512
FFN_COLS = 256
LANE = 128
SUBLANE = 8
VMEM_LIMIT = 56 * 1024 * 1024
LOG2E = 1.4426950408889634


def _cparams(n_axes):
    return pltpu.CompilerParams(
        dimension_semantics=("arbitrary",) * n_axes,
        vmem_limit_bytes=VMEM_LIMIT)


def _resident(shape, index=None):
    index = (0,) * len(shape) if index is None else index
    return pl.BlockSpec(shape, lambda *_: index, pipeline_mode=pl.Buffered(1))


def _dot(a, b):
    return jnp.dot(a, b, preferred_element_type=F32)


def _dot_nt(a, b):
    return lax.dot_general(a, b, (((1,), (1,)), ((), ())),
                           preferred_element_type=F32)


def _dot_tn(a, b):
    return lax.dot_general(a, b, (((0,), (0,)), ((), ())),
                           preferred_element_type=F32)


def _rmsnorm(x, gain):
    ms = jnp.mean(x * x, axis=-1, keepdims=True)
    return x * lax.rsqrt(ms + EPS) * gain


def _exp_neg(x):
    return jnp.exp2(x * (-LOG2E))


def _silu(x):
    return x * (1.0 / (1.0 + _exp_neg(x)))


def _run_staged(gens):
    results = [None] * len(gens)
    live = list(range(len(gens)))
    while live:
        still = []
        for i in live:
            try:
                next(gens[i])
                still.append(i)
            except StopIteration as stop:
                results[i] = stop.value
        live = still
    return results


def _token_specs(tm, ts, width):
    return (pl.BlockSpec((tm, width), lambda i: (i, 0)),
            pl.BlockSpec((ts, width), lambda i: (i, 0)))


def _token_tiles(xp, xs, tile=None):
    n_prompt, n_sample = xp.shape[0], xs.shape[0]
    tm = math.gcd(n_prompt, TOKEN_TILE if tile is None else tile)
    steps = n_prompt // tm
    ts = n_sample // steps
    assert ts * steps == n_sample and ts % 16 == 0
    return tm, ts, steps


def _load_normed(xp_ref, xs_ref, g_ref, xn_ref):
    tm = xp_ref.shape[0]
    xn_ref[:tm, :] = _rmsnorm(xp_ref[...], g_ref[...]).astype(BF16)
    xn_ref[tm:, :] = _rmsnorm(xs_ref[...], g_ref[...]).astype(BF16)


def _ffn_body(xp, xs, g_ref, wu_ref, wd_ref, op_ref, os_ref, xn_ref, h_ref, fg_ref=None):
    tm = xp.shape[0]
    d_ff = wd_ref.shape[0]
    xn_ref[:tm, :] = _rmsnorm(xp, g_ref[...]).astype(BF16)
    xn_ref[tm:, :] = _rmsnorm(xs, g_ref[...]).astype(BF16)
    xn = xn_ref[...]
    for c in range(d_ff // FFN_COLS):
        lo = c * FFN_COLS
        a = _dot(xn, wu_ref[:, lo:lo + FFN_COLS])
        b = _dot(xn, wu_ref[:, d_ff + lo:d_ff + lo + FFN_COLS])
        h_ref[:, lo:lo + FFN_COLS] = (_silu(a) * b).astype(BF16)
    y = _dot(h_ref[...], wd_ref[...])
    yp = xp + 0.5 * y[:tm]
    ys = xs + 0.5 * y[tm:]
    if fg_ref is not None:
        yp, ys = _rmsnorm(yp, fg_ref[...]), _rmsnorm(ys, fg_ref[...])
    op_ref[...] = yp
    os_ref[...] = ys


def _ffn_kernel(xp_ref, xs_ref, g_ref, wu_ref, wd_ref, op_ref, os_ref, xn_ref, h_ref):
    _ffn_body(xp_ref[...], xs_ref[...], g_ref, wu_ref, wd_ref, op_ref, os_ref, xn_ref, h_ref)


def _mix_ffn_kernel(xp_ref, xs_ref, ap_ref, as_ref, wo_ref, g_ref, wu_ref, wd_ref, *rest):
    fg_ref = rest[0] if len(rest) == 5 else None
    xp = xp_ref[...] + _dot(ap_ref[...], wo_ref[...])
    xs = xs_ref[...] + _dot(as_ref[...], wo_ref[...])
    _ffn_body(xp, xs, g_ref, wu_ref, wd_ref, *rest[-4:], fg_ref=fg_ref)


def _ffn(xp, xs, gains, w_up, w_down, layer, which, mix=None, final_gain=None):
    d = xp.shape[1]
    d_ff = w_down.shape[2]
    tm, ts, steps = _token_tiles(xp, xs, None if mix is None else MIX_FFN_TILE)
    assert d_ff % FFN_COLS == 0
    xspec = _token_specs(tm, ts, d)
    gain = _resident((None, None, 1, d), (layer, 2 * which, 0, 0))
    weights = [_resident((None, None, d, 2 * d_ff), (layer, which, 0, 0)),
               _resident((None, None, d_ff, d), (layer, which, 0, 0))]
    gains4 = gains.reshape(gains.shape[0], gains.shape[1], 1, d)
    if mix is None:
        body, in_specs, args = _ffn_kernel, [*xspec, gain, *weights], (xp, xs, gains4, w_up, w_down)
    else:
        ap, a_s, w_out, j = mix
        k = ap.shape[1]
        body = _mix_ffn_kernel
        in_specs = [*xspec, *_token_specs(tm, ts, k), _resident((None, k, d), (j, 0, 0)),
                    gain, *weights]
        args = (xp, xs, ap, a_s, w_out, gains4, w_up, w_down)
        if final_gain is not None:
            in_specs.append(_resident((1, d)))
            args += (final_gain.reshape(1, d),)
    return pl.pallas_call(
        body,
        out_shape=(jax.ShapeDtypeStruct(xp.shape, F32), jax.ShapeDtypeStruct(xs.shape, F32)),
        grid=(steps,),
        in_specs=in_specs,
        out_specs=xspec,
        scratch_shapes=[pltpu.VMEM((tm + ts, d), BF16), pltpu.VMEM((tm + ts, d_ff), BF16)],
        compiler_params=_cparams(1),
        name="ffn" if mix is None else "mix_ffn",
    )(*args)


def _ret_proj_kernel(xp_ref, xs_ref, g_ref, w_ref, cp_ref, sp_ref, cs_ref, ss_ref,
                     qp_ref, qs_ref, kp_ref, ks_ref, vp_ref, vs_ref, gp_ref, gs_ref, xn_ref,
                     *, dk, dv, heads):
    tm = xp_ref.shape[0]
    _load_normed(xp_ref, xs_ref, g_ref, xn_ref)
    xn = xn_ref[...]
    cos = jnp.concatenate([cp_ref[...], cs_ref[...]], axis=0)
    sin = jnp.concatenate([sp_ref[...], ss_ref[...]], axis=0)
    half = dk // 2
    qk = heads * dk
    k_scale = dk ** -0.5

    def put(p_ref, s_ref, lo, val):
        val = val.astype(BF16)
        p_ref[:, lo:lo + val.shape[1]] = val[:tm]
        s_ref[:, lo:lo + val.shape[1]] = val[tm:]

    for h in range(heads):
        for base, p_ref, s_ref, scale in ((0, qp_ref, qs_ref, None), (qk, kp_ref, ks_ref, k_scale)):
            t = _dot(xn, w_ref[:, base + h * dk:base + (h + 1) * dk])
            x1, x2 = t[:, :half], t[:, half:]
            r1 = x1 * cos - x2 * sin
            r2 = x1 * sin + x2 * cos
            if scale is not None:
                r1, r2 = r1 * scale, r2 * scale
            put(p_ref, s_ref, h * dk, r1)
            put(p_ref, s_ref, h * dk + half, r2)
    for h in range(heads):
        lo = 2 * qk + h * dv
        put(vp_ref, vs_ref, h * dv, _dot(xn, w_ref[:, lo:lo + dv]))
        lo = 2 * qk + heads * dv + h * dv
        put(gp_ref, gs_ref, h * dv, _silu(_dot(xn, w_ref[:, lo:lo + dv])))


def _rope_tables(pos, half):
    inv_freq = ROPE_BASE ** (-jnp.arange(half, dtype=F32) / half)
    ang = pos[:, None] * inv_freq[None, :]
    return jnp.cos(ang), jnp.sin(ang)


def _ret_proj(xp, xs, gains, w_in, layer, j, seq_len, dec_len):
    d = xp.shape[1]
    heads = RET_HEADS
    dk = d // heads
    dv = 2 * dk
    half = dk // 2
    tm, ts, steps = _token_tiles(xp, xs)
    assert seq_len % tm == 0 and ts % dec_len == 0
    tiles_per_seq = seq_len // tm
    cos_p, sin_p = _rope_tables(jnp.arange(seq_len, dtype=F32), half)
    cos_s, sin_s = _rope_tables(
        jnp.tile(PAST_LEN + jnp.arange(dec_len, dtype=F32), ts // dec_len), half)
    tab_p = pl.BlockSpec((tm, half), lambda i: (i % tiles_per_seq, 0))
    widths = (heads * dk, heads * dk, heads * dv, heads * dv)
    out_shape, out_specs = [], []
    for w in widths:
        out_shape += [jax.ShapeDtypeStruct((xp.shape[0], w), BF16),
                      jax.ShapeDtypeStruct((xs.shape[0], w), BF16)]
        out_specs += list(_token_specs(tm, ts, w))
    return pl.pallas_call(
        functools.partial(_ret_proj_kernel, dk=dk, dv=dv, heads=heads),
        out_shape=tuple(out_shape),
        grid=(steps,),
        in_specs=[*_token_specs(tm, ts, d),
                  _resident((None, None, 1, d), (layer, 1, 0, 0)),
                  _resident((None,) + w_in.shape[1:], (j, 0, 0)),
                  tab_p, tab_p, _resident((ts, half)), _resident((ts, half))],
        out_specs=tuple(out_specs),
        scratch_shapes=[pltpu.VMEM((tm + ts, d), BF16)],
        compiler_params=_cparams(1),
        name="ret_proj",
    )(xp, xs, gains.reshape(gains.shape[0], gains.shape[1], 1, d), w_in,
      cos_p, sin_p, cos_s, sin_s)


def _head_norm_gate(o, gn, sg):
    ms = jnp.mean(o * o, axis=-1, keepdims=True)
    return (o * lax.rsqrt(ms + EPS) * gn * sg.astype(F32)).astype(BF16)


def _ret_prompt_step(q_ref, k_ref, v_ref, sg_ref, gn_ref, dm_ref, qd_ref, kd_ref, cd_ref,
                     og_ref, s_ref, *, it, cpi, chunk, hps, dk, dv):
    h0 = pl.program_id(1) * hps

    @pl.when(it == 0)
    def _():
        for hh in range(hps):
            s_ref[0, hh] = jnp.zeros((dk, dv), F32)

    def head_chain(hh, rows):
        qs = [q_ref[r, hh * dk:(hh + 1) * dk] for r in rows]
        ks = [k_ref[r, hh * dk:(hh + 1) * dk] for r in rows]
        vs = [v_ref[r, hh * dv:(hh + 1) * dv] for r in rows]
        sgs = [sg_ref[r, hh * dv:(hh + 1) * dv] for r in rows]
        s = s_ref[0, hh]
        cd = cd_ref[h0 + hh]
        scores = [_dot_nt(qc, kc) for qc, kc in zip(qs, ks)]
        updates = [_dot_tn((kc.astype(F32) * kd_ref[hh]).astype(BF16), vc)
                   for kc, vc in zip(ks, vs)]
        yield
        intra = [_dot((sc * dm_ref[hh]).astype(BF16), vc) for sc, vc in zip(scores, vs)]
        inter = []
        for qc, upd in zip(qs, updates):
            inter.append(_dot(qc, s.astype(BF16)))
            s = s * cd + upd
        yield
        return s, [_head_norm_gate(oi + oc * qd_ref[hh], gn_ref[hh], sgc)
                   for oi, oc, sgc in zip(intra, inter, sgs)]

    rows = [pl.ds(cc * chunk, chunk) for cc in range(cpi)]
    results = _run_staged([head_chain(hh, rows) for hh in range(hps)])
    for hh, (s_new, ogs) in enumerate(results):
        s_ref[0, hh] = s_new
        for r, og in zip(rows, ogs):
            og_ref[r, hh * dv:(hh + 1) * dv] = og


def _ret_decay_tables(chunk):
    heads = RET_HEADS
    log_gamma = jnp.log(1.0 - jnp.power(2.0, -5.0 - jnp.arange(heads, dtype=F32)))
    idx = jnp.arange(chunk, dtype=F32)
    diff = idx[:, None] - idx[None, :]
    lg = log_gamma[:, None, None]
    decay_mat = jnp.where(diff[None] >= 0, jnp.exp(diff[None] * lg), 0.0)
    q_decay = jnp.exp((idx[None, :] + 1.0) * log_gamma[:, None])
    k_decay = jnp.exp((chunk - 1.0 - idx[None, :]) * log_gamma[:, None])
    chunk_decay = jnp.exp(chunk * log_gamma)
    return decay_mat, q_decay, k_decay, chunk_decay


def _state_out(prev, shape, j, block, index_map):
    spec = pl.BlockSpec((None,) + block, lambda *ids: (j,) + index_map(*ids))
    extra_in = [] if prev is None else [prev]
    extra_spec = [] if prev is None else [pl.BlockSpec(memory_space=pl.ANY)]
    return spec, jax.ShapeDtypeStruct(shape, F32), extra_in, extra_spec


def _ret_sample_step(q_ref, k_ref, v_ref, sg_ref, gn_ref, dm_ref, qd_ref, kd_ref, cd_ref,
                     s0_ref, og_ref, s_ref, *, h, seqs, dec_len):
    rows = seqs * dec_len
    qb = q_ref[...]
    vb = v_ref[...]
    kdec = k_ref[...].astype(F32) * kd_ref[0]
    cd = cd_ref[h]
    row_seq = lax.broadcasted_iota(jnp.int32, (rows, 1), 0) // dec_len
    scores = _dot_nt(qb, k_ref[...]) * dm_ref[0]
    o = _dot(scores.astype(BF16), vb)
    inter = jnp.zeros(o.shape, F32)
    for i in range(seqs):
        s_old = s0_ref[i, 0]
        mine = row_seq == i
        inter = jnp.where(mine, _dot(qb, s_old.astype(BF16)), inter)
        k_i = jnp.where(mine, kdec, 0.0).astype(BF16)
        s_ref[i, 0] = s_old * cd + _dot_tn(k_i, vb)
    o = o + inter * qd_ref[0]
    og_ref[...] = _head_norm_gate(o, gn_ref[0], sg_ref[...])


def _ret_core_kernel(*refs, n_prev, heads, cpi, chunk, hps, dk, dv, seqs, dec_len):
    prompt_in, sample_in = refs[:9], refs[9:19]
    og_p, s_p, og_s, s_s = refs[19 + n_prev:]
    n_h, n_it = pl.num_programs(1), pl.num_programs(2)
    unit = (pl.program_id(0) * n_h + pl.program_id(1)) * n_it + pl.program_id(2)
    _ret_sample_step(*sample_in, og_s, s_s, h=unit % heads, seqs=seqs, dec_len=dec_len)
    _ret_prompt_step(*prompt_in, og_p, s_p, it=pl.program_id(2), cpi=cpi, chunk=chunk,
                     hps=hps, dk=dk, dv=dv)


def _ret_core(prompt, sample, gn, s0_all, j, prev_prompt, prev_sample, n_seqs, seq_len, dec_len):
    qp, kp, vp, gp = prompt
    qs, ks, vs, gs = sample
    heads = RET_HEADS
    hps = RET_HEADS_PER_STEP
    n_layers, dec_seqs = s0_all.shape[:2]
    dk, dv = s0_all.shape[3], s0_all.shape[4]
    gn4 = gn.reshape(gn.shape[0], heads, 1, dv)

    chunk = math.gcd(seq_len, RET_CHUNK)
    n_chunks = seq_len // chunk
    cpi = math.gcd(n_chunks, RET_CHUNKS_PER_ITER)
    grid = (n_seqs, heads // hps, n_chunks // cpi)
    dm, qd, kd, cd = _ret_decay_tables(chunk)
    per_step = lambda w: pl.BlockSpec((cpi * chunk, hps * w),
                                      lambda b, h, t: (b * grid[2] + t, h))
    tab = lambda shape: pl.BlockSpec((hps,) + shape, lambda b, h, t: (h, 0, 0))
    sp_spec, sp_shape, prev_p, prev_p_spec = _state_out(
        prev_prompt, (n_layers, n_seqs, heads, dk, dv), j, (1, hps, dk, dv),
        lambda b, h, t: (b, h, 0, 0))
    prompt_specs = [per_step(dk), per_step(dk), per_step(dv), per_step(dv),
                    pl.BlockSpec((None, hps, 1, dv), lambda b, h, t: (j, h, 0, 0)),
                    tab((chunk, chunk)), tab((chunk, 1)), tab((chunk, 1)),
                    pl.BlockSpec(memory_space=pltpu.SMEM)]
    prompt_args = (qp, kp, vp, gp, gn4, dm, qd[:, :, None], kd[:, :, None], cd)

    n_units = grid[0] * grid[1] * grid[2]
    seqs = dec_seqs * heads // n_units
    rows = seqs * dec_len
    assert seqs * n_units == dec_seqs * heads and rows % 16 == 0
    assert math.gcd(dec_len, RET_CHUNK) == dec_len
    dm_s, qd_s, kd_s, cd_s = _ret_decay_tables(dec_len)
    seq_id = np.arange(rows) // dec_len
    same = jnp.asarray(seq_id[:, None] == seq_id[None, :])
    dm_blk = jnp.where(same[None], jnp.tile(dm_s, (1, seqs, seqs)), 0.0)
    qd_blk = jnp.tile(qd_s, (1, seqs))[:, :, None]
    kd_blk = jnp.tile(kd_s, (1, seqs))[:, :, None]

    def unit(b, h, t):
        u = (b * grid[1] + h) * grid[2] + t
        return u // heads, u % heads

    per_unit = lambda w: pl.BlockSpec((rows, w), lambda b, h, t: unit(b, h, t))
    tab_s = lambda shape: pl.BlockSpec((1,) + shape, lambda b, h, t: (unit(b, h, t)[1], 0, 0))
    state_in = pl.BlockSpec((None, seqs, 1, dk, dv),
                            lambda b, h, t: (j,) + unit(b, h, t) + (0, 0))
    ss_spec, ss_shape, prev_s, prev_s_spec = _state_out(
        prev_sample, s0_all.shape, j, (seqs, 1, dk, dv),
        lambda b, h, t: unit(b, h, t) + (0, 0))
    sample_specs = [per_unit(dk), per_unit(dk), per_unit(dv), per_unit(dv),
                    pl.BlockSpec((None, 1, 1, dv),
                                 lambda b, h, t: (j, unit(b, h, t)[1], 0, 0)),
                    tab_s((rows, rows)), tab_s((rows, 1)), tab_s((rows, 1)),
                    pl.BlockSpec(memory_space=pltpu.SMEM), state_in]
    sample_args = (qs, ks, vs, gs, gn4, dm_blk, qd_blk, kd_blk, cd_s, s0_all)

    n_prev = len(prev_p) + len(prev_s)
    aliases = {}
    if prev_p:
        aliases[19] = 1
    if prev_s:
        aliases[19 + len(prev_p)] = 3
    return pl.pallas_call(
        functools.partial(_ret_core_kernel, n_prev=n_prev, heads=heads, cpi=cpi, chunk=chunk,
                          hps=hps, dk=dk, dv=dv, seqs=seqs, dec_len=dec_len),
        out_shape=(jax.ShapeDtypeStruct((qp.shape[0], heads * dv), BF16), sp_shape,
                   jax.ShapeDtypeStruct((qs.shape[0], heads * dv), BF16), ss_shape),
        grid=grid,
        in_specs=prompt_specs + sample_specs + prev_p_spec + prev_s_spec,
        out_specs=(per_step(dv), sp_spec, per_unit(dv), ss_spec),
        input_output_aliases=aliases,
        compiler_params=_cparams(3),
        name="ret_core",
    )(*prompt_args, *sample_args, *prev_p, *prev_s)


def _hg_proj_kernel(xp_ref, xs_ref, g_ref, w_ref, lbl_ref,
                    qp_ref, qs_ref, kp_ref, ks_ref, gp_ref, gs_ref, vp_ref, vs_ref,
                    sp_ref, ss_ref, xn_ref, *, layer, cols):
    tm, d = xp_ref.shape
    _load_normed(xp_ref, xs_ref, g_ref, xn_ref)
    xn = xn_ref[...]
    logits = lbl_ref[...]
    e = jnp.exp(logits - jnp.max(logits, axis=0, keepdims=True))
    sm = e / jnp.sum(e, axis=0, keepdims=True)
    cum = sm[0:1]
    for i in range(1, layer + 1):
        cum = cum + sm[i:i + 1]
    lb = cum - sm[0:1]
    log_lb = jnp.log(lb)
    log_1m_lb = jnp.log1p(-lb)

    def put(p_ref, s_ref, lo, val):
        p_ref[:, lo:lo + cols] = val[:tm].astype(p_ref.dtype)
        s_ref[:, lo:lo + cols] = val[tm:].astype(s_ref.dtype)

    for c in range(d // cols):
        lo = c * cols
        put(qp_ref, qs_ref, lo, _silu(_dot(xn, w_ref[:, lo:lo + cols])))
        z = _dot(xn, w_ref[:, d + lo:d + lo + cols])
        ez = _exp_neg(jnp.abs(z))
        ez1 = 1.0 + ez
        log_sig = jnp.minimum(z, 0.0) - jnp.log(ez1)
        a = log_lb[:, lo:lo + cols]
        b = log_1m_lb[:, lo:lo + cols] + log_sig
        delta = a - b
        g = jnp.where(delta != delta, a + b,
                      jnp.maximum(a, b) + jnp.log(1.0 + _exp_neg(jnp.abs(delta))))
        put(gp_ref, gs_ref, lo, g)
        sig_neg = jnp.where(z >= 0.0, ez, 1.0) * (1.0 / ez1)
        put(kp_ref, ks_ref, lo, (1.0 - lb[:, lo:lo + cols]) * sig_neg)
        put(vp_ref, vs_ref, lo, _dot(xn, w_ref[:, 2 * d + lo:2 * d + lo + cols]))
        put(sp_ref, ss_ref, lo, _silu(_dot(xn, w_ref[:, 3 * d + lo:3 * d + lo + cols])))


def _hg_proj(xp, xs, gains, w_in, lb_logits, layer, j):
    d = xp.shape[1]
    tm, ts, steps = _token_tiles(xp, xs)
    out_shape, out_specs = [], []
    for dt in (BF16, BF16, F32, BF16, BF16):
        out_shape += [jax.ShapeDtypeStruct(xp.shape, dt), jax.ShapeDtypeStruct(xs.shape, dt)]
        out_specs += list(_token_specs(tm, ts, d))
    return pl.pallas_call(
        functools.partial(_hg_proj_kernel, layer=j, cols=2 * LANE),
        out_shape=tuple(out_shape),
        grid=(steps,),
        in_specs=[*_token_specs(tm, ts, d),
                  _resident((None, None, 1, d), (layer, 1, 0, 0)),
                  _resident((None,) + w_in.shape[1:], (j, 0, 0)),
                  _resident(lb_logits.shape)],
        out_specs=tuple(out_specs),
        scratch_shapes=[pltpu.VMEM((tm + ts, d), BF16)],
        compiler_params=_cparams(1),
        name="hg_proj",
    )(xp, xs, gains.reshape(gains.shape[0], gains.shape[1], 1, d), w_in, lb_logits)


def _hg_tables(chunk, block):
    n_levels = int(math.log2(block))
    assert 2 ** n_levels == block and chunk % block == 0 and chunk == LANE
    t = np.arange(chunk)[:, None]
    s = np.arange(chunk)[None, :]
    same_run = (t // block) == (s // block)
    sums = np.concatenate([same_run & (s <= t), same_run & (s > t)], axis=0)
    masks = [t == s]
    for l in range(n_levels):
        m = 2 ** l
        mid = (t // (2 * m)) * (2 * m) + m
        masks.append(((t // (2 * m)) == (s // (2 * m))) & (t >= mid) & (s < mid))
    r = np.arange(chunk)
    rows = np.stack([r % 2 == 1, r % 4 == 0, r % 4 >= 2, r % 4 == 3,
                     np.where(r % 8 >= 4, 1.0, -1.0)]).astype(np.float32)
    rows = np.broadcast_to(rows[:, :, None], rows.shape + (LANE,))
    block_causal = ((t // HG_FAST_BLOCK) == (s // HG_FAST_BLOCK)) & (s <= t)
    rows = np.concatenate([rows, block_causal[None].astype(np.float32)])
    sums = jnp.asarray(sums.astype(np.float32), dtype=BF16)
    masks = jnp.asarray(np.stack(masks).astype(np.float32), dtype=BF16)
    return sums, masks, jnp.asarray(rows)


def _level_exponents(b, g2, b_ref, rows_ref, block):
    c = b.shape[0]
    out = []
    n_levels = int(math.log2(block))
    for l in range(n_levels):
        m = 2 ** l
        if m == 1:
            out.append(g2 * rows_ref[0])
        elif m == 2:
            nxt = pltpu.roll(g2, c - 1, 0)
            prv = pltpu.roll(g2, 1, 0)
            out.append(nxt * rows_ref[1] + g2 * rows_ref[2] + prv * rows_ref[3])
        elif 2 * m == SUBLANE:
            pieces = [b[lo:lo + SUBLANE] - b_ref[pl.ds(lo + m - 1, 1), :]
                      for lo in range(0, c, SUBLANE)]
            out.append(jnp.concatenate(pieces, axis=0) * rows_ref[4])
        else:
            pieces = []
            for lo in range(0, c, 2 * m):
                ref_row = b_ref[pl.ds(lo + m - 1, 1), :]
                pieces.append(ref_row - b[lo:lo + m])
                pieces.append(b[lo + m:lo + 2 * m] - ref_row)
            out.append(jnp.concatenate(pieces, axis=0))
    return out


def _hg_chunk(q, k, v, g, sums_ref, masks_ref, rows_ref, b_ref, block):
    c = q.shape[0]
    g2 = g * LOG2E
    g_hi = g2.astype(BF16)
    g_lo = (g2 - g_hi.astype(F32)).astype(BF16)
    g_split = jnp.concatenate([g_hi, g_lo], axis=1)
    both = _dot(sums_ref[:c, :] if block == c else sums_ref[...], g_split)
    yield
    both = both[:, :LANE] + both[:, LANE:]
    if block == c:
        b = both
        b_rest = b[c - 1:c, :] - b
    else:
        b, b_rest = both[:c], both[c:]
    b_ref[...] = b
    level_scores = [_dot_nt(q, k)]
    for nd in _level_exponents(b, g2, b_ref, rows_ref, block):
        e = jnp.exp2(nd).astype(BF16)
        level_scores.append(_dot_nt(q * e, k * e))
    eb = jnp.exp2(b)
    q_dec = q * eb.astype(BF16)
    k_dec = k * jnp.exp2(b_rest).astype(BF16)
    yield
    scores = level_scores[0].astype(BF16) * masks_ref[0]
    for l, s in enumerate(level_scores[1:]):
        scores = scores + s.astype(BF16) * masks_ref[1 + l]
    o_intra = _dot(scores, v)
    return o_intra, q_dec, k_dec, eb


def _hg_chunk_fast(q, k, v, g, sums_ref, rows_ref):
    c = q.shape[0]
    blk = HG_FAST_BLOCK
    g2 = g * LOG2E
    g_hi = g2.astype(BF16)
    g_lo = (g2 - g_hi.astype(F32)).astype(BF16)
    b = _dot(sums_ref[:c, :], jnp.concatenate([g_hi, g_lo], axis=1))
    yield
    b = b[:, :LANE] + b[:, LANE:]
    qf = q.astype(F32)
    kf = k.astype(F32)
    q_blocks, k_blocks, cross = [], [], [jnp.zeros((blk, c), F32)]
    for i in range(c // blk):
        rows = slice(i * blk, (i + 1) * blk)
        if i == 0:
            d = b[rows]
        else:
            r = b[i * blk - 1:i * blk, :]
            d = b[rows] - r
        q_i = (qf[rows] * jnp.exp2(d)).astype(BF16)
        q_blocks.append(q_i)
        k_blocks.append((kf[rows] * jnp.exp2(-d)).astype(BF16))
        if i > 0:
            k_before = (kf[:i * blk] * jnp.exp2(r - b[:i * blk])).astype(BF16)
            k_before = jnp.concatenate(
                [k_before, jnp.zeros((c - i * blk, LANE), BF16)], axis=0)
            cross.append(_dot_nt(q_i, k_before))
    within = _dot_nt(jnp.concatenate(q_blocks, axis=0), jnp.concatenate(k_blocks, axis=0))
    eb = jnp.exp2(b)
    q_dec = (qf * eb).astype(BF16)
    k_dec = (kf * jnp.exp2(b[c - 1:c, :] - b)).astype(BF16)
    yield
    scores = within * rows_ref[5] + jnp.concatenate(cross, axis=0)
    o_intra = _dot(scores.astype(BF16), v)
    return o_intra, q_dec, k_dec, eb


def _hg_core_prompt_kernel(q_ref, k_ref, v_ref, g_ref, sg_ref, gn_ref, sums_ref, masks_ref,
                           rows_ref, blocks_ref, *rest, n_chunks, chunk, hps, dh):
    og_ref, s_ref, st_ref, b_ref = rest[-4:]
    st_ref[...] = jnp.zeros(st_ref.shape, F32)

    cpi = math.gcd(n_chunks, HG_CHUNKS_PER_ITER)

    def chunk_loop(fast):
        def body(it, carry):
            rows = [pl.ds(pl.multiple_of((it * cpi + cc) * chunk, chunk), chunk)
                    for cc in range(cpi)]
            heads = [slice(hh * dh, (hh + 1) * dh) for hh in range(hps)]
            pairs = [(hh, cc) for hh in range(hps) for cc in range(cpi)]
            vs = {(hh, cc): v_ref[rows[cc], heads[hh]] for hh, cc in pairs}
            chains = []
            for hh, cc in pairs:
                args = (q_ref[rows[cc], heads[hh]], k_ref[rows[cc], heads[hh]], vs[hh, cc],
                        g_ref[rows[cc], heads[hh]])
                if fast:
                    chains.append(_hg_chunk_fast(*args, sums_ref, rows_ref))
                else:
                    chains.append(_hg_chunk(*args, sums_ref, masks_ref, rows_ref,
                                            b_ref.at[hh * cpi + cc], chunk))
            intra = dict(zip(pairs, _run_staged(chains)))
            updates = {p: _dot_tn(vs[p], intra[p][2]) for p in pairs}
            states = [st_ref[hh] for hh in range(hps)]
            outs = {}
            for cc in range(cpi):
                inters = [_dot_nt(intra[hh, cc][1], states[hh].astype(BF16))
                          for hh in range(hps)]
                for hh in range(hps):
                    o_intra, _, _, eb = intra[hh, cc]
                    states[hh] = states[hh] * eb[chunk - 1:chunk, :] + updates[hh, cc]
                    outs[hh, cc] = _head_norm_gate(o_intra + inters[hh], gn_ref[hh],
                                                   sg_ref[rows[cc], heads[hh]])
            for hh in range(hps):
                st_ref[hh] = states[hh]
            for hh, cc in pairs:
                og_ref[rows[cc], heads[hh]] = outs[hh, cc]
            return carry

        lax.fori_loop(0, n_chunks // cpi, body, 0)

    block_sums = _dot(blocks_ref[...], jnp.abs(g_ref[...]).astype(BF16))
    fast_ok = jnp.max(block_sums) * LOG2E < HG_FAST_LIMIT
    pl.when(fast_ok)(lambda: chunk_loop(True))
    pl.when(jnp.logical_not(fast_ok))(lambda: chunk_loop(False))
    for hh in range(hps):
        s_ref[0, hh] = st_ref[hh].T


def _hg_core_prompt(q, k, v, g, sg, gn, j, n_layers, prev_state, n_seqs, seq_len):
    n, d = q.shape
    heads = HG_HEADS
    hps = HG_HEADS_PER_STEP
    dh = d // heads
    chunk = math.gcd(seq_len, HG_CHUNK)
    sums, masks, row_tabs = _hg_tables(chunk, chunk)
    block_of_row = np.arange(seq_len) // HG_FAST_BLOCK
    blocks = jnp.asarray(np.arange(seq_len // HG_FAST_BLOCK)[:, None] == block_of_row[None, :],
                         dtype=BF16)
    per_step = pl.BlockSpec((seq_len, hps * dh), lambda b, h: (b, h))
    s_spec, s_shape, extra_in, extra_spec = _state_out(
        prev_state, (n_layers, n_seqs, heads, dh, dh), j, (1, hps, dh, dh),
        lambda b, h: (b, h, 0, 0))
    n_in = 10 + len(extra_in)
    return pl.pallas_call(
        functools.partial(_hg_core_prompt_kernel, n_chunks=seq_len // chunk, chunk=chunk,
                          hps=hps, dh=dh),
        out_shape=(jax.ShapeDtypeStruct((n, d), BF16), s_shape),
        grid=(n_seqs, heads // hps),
        in_specs=[per_step, per_step, per_step, per_step, per_step,
                  pl.BlockSpec((None, hps, 1, dh), lambda b, h: (j, h, 0, 0)),
                  _resident(sums.shape), _resident(masks.shape),
                  _resident(row_tabs.shape), _resident(blocks.shape)] + extra_spec,
        out_specs=(per_step, s_spec),
        input_output_aliases={n_in - 1: 1} if extra_in else {},
        scratch_shapes=[pltpu.VMEM((hps, dh, dh), F32),
                        pltpu.VMEM((hps * HG_CHUNKS_PER_ITER, chunk, dh), F32)],
        compiler_params=_cparams(2),
        name="hg_core_prompt",
    )(q, k, v, g, sg, gn.reshape(gn.shape[0], heads, 1, dh), sums, masks, row_tabs, blocks,
      *extra_in)


def _hg_core_sample_kernel(q_ref, k_ref, v_ref, g_ref, sg_ref, gn_ref, sums_ref, masks_ref,
                           rows_ref, s0_ref, *rest, seqs, dec_len):
    og_ref, s_ref, b_ref = rest[-3:]
    rows = seqs * dec_len
    vb = v_ref[...]
    (o, q_dec, k_dec, eb), = _run_staged([_hg_chunk(
        q_ref[...], k_ref[...], vb, g_ref[...], sums_ref, masks_ref, rows_ref, b_ref, dec_len)])
    eb_t = eb.T
    k_dec_t = k_dec.astype(F32).T
    row_seq = lax.broadcasted_iota(jnp.int32, (rows, 1), 0) // dec_len
    col_seq = lax.broadcasted_iota(jnp.int32, (1, rows), 1) // dec_len
    inter = jnp.zeros(o.shape, F32)
    for i in range(seqs):
        s_old = s0_ref[i, 0]
        inter = jnp.where(row_seq == i, _dot(q_dec, s_old.astype(BF16)), inter)
        k_i = jnp.where(col_seq == i, k_dec_t, 0.0).astype(BF16)
        last = (i + 1) * dec_len - 1
        s_ref[i, 0] = s_old * eb_t[:, last:last + 1] + _dot(k_i, vb)
    og_ref[...] = _head_norm_gate(o + inter, gn_ref[0], sg_ref[...])


def _hg_core_sample(q, k, v, g, sg, gn, s0_all, j, prev_state, dec_len):
    heads = HG_HEADS
    n_layers, n_seqs = s0_all.shape[:2]
    dh = s0_all.shape[3]
    rows = math.gcd(n_seqs * dec_len, HG_CHUNK)
    seqs = rows // dec_len
    assert rows % dec_len == 0
    sums, masks, row_tabs = _hg_tables(rows, dec_len)
    per_head = pl.BlockSpec((rows, dh), lambda i, h: (i, h))
    state_in = pl.BlockSpec((None, seqs, 1, dh, dh), lambda i, h: (j, i, h, 0, 0))
    s_spec, s_shape, extra_in, extra_spec = _state_out(
        prev_state, s0_all.shape, j, (seqs, 1, dh, dh), lambda i, h: (i, h, 0, 0))
    n_in = 10 + len(extra_in)
    return pl.pallas_call(
        functools.partial(_hg_core_sample_kernel, seqs=seqs, dec_len=dec_len),
        out_shape=(jax.ShapeDtypeStruct(q.shape, BF16), s_shape),
        grid=(n_seqs // seqs, heads),
        in_specs=[per_head, per_head, per_head, per_head, per_head,
                  pl.BlockSpec((None, 1, 1, dh), lambda i, h: (j, h, 0, 0)),
                  _resident(sums.shape), _resident(masks.shape), _resident(row_tabs.shape),
                  state_in] + extra_spec,
        out_specs=(per_head, s_spec),
        input_output_aliases={n_in - 1: 1} if extra_in else {},
        scratch_shapes=[pltpu.VMEM((rows, dh), F32)],
        compiler_params=_cparams(2),
        name="hg_core_sample",
    )(q, k, v, g, sg, gn.reshape(gn.shape[0], heads, 1, dh), sums, masks, row_tabs, s0_all,
      *extra_in)


def kernel(x_prompt, x_sample, state_ret, state_hgrn, norm_gain, ffn_w_up, ffn_w_down,
           ret_w_in, ret_norm, ret_w_out, hg_w_in, hg_lb_logits, hg_norm, hg_w_out, final_norm):
    n_seqs, seq_len, d = x_prompt.shape
    dec_seqs, dec_len, _ = x_sample.shape
    depth = norm_gain.shape[0]
    n_ret, n_hg = state_ret.shape[0], state_hgrn.shape[0]

    xp = x_prompt.reshape(n_seqs * seq_len, d)
    xs = x_sample.reshape(dec_seqs * dec_len, d)
    w_up, w_down = ffn_w_up.astype(BF16), ffn_w_down.astype(BF16)
    ret_in, ret_out = ret_w_in.astype(BF16), ret_w_out.astype(BF16)
    hg_in, hg_out = hg_w_in.astype(BF16), hg_w_out.astype(BF16)

    ret_p = ret_s = hg_p = hg_s = None
    for layer in range(depth):
        xp, xs = _ffn(xp, xs, norm_gain, w_up, w_down, layer, 0)
        j = layer // 2
        if layer % 2 == 0:
            qp, qs, kp, ks, vp, vs, gp, gs = _ret_proj(
                xp, xs, norm_gain, ret_in, layer, j, seq_len, dec_len)
            ap, ret_p, a_s, ret_s = _ret_core(
                (qp, kp, vp, gp), (qs, ks, vs, gs), ret_norm, state_ret, j, ret_p, ret_s,
                n_seqs, seq_len, dec_len)
            mix = (ap, a_s, ret_out, j)
        else:
            qp, qs, kp, ks, gp, gs, vp, vs, sp, ss = _hg_proj(
                xp, xs, norm_gain, hg_in, hg_lb_logits, layer, j)
            ap, hg_p = _hg_core_prompt(qp, kp, vp, gp, sp, hg_norm, j, n_hg, hg_p,
                                       n_seqs, seq_len)
            a_s, hg_s = _hg_core_sample(qs, ks, vs, gs, ss, hg_norm, state_hgrn, j, hg_s,
                                        dec_len)
            mix = (ap, a_s, hg_out, j)
        xp, xs = _ffn(xp, xs, norm_gain, w_up, w_down, layer, 1, mix,
                      final_norm if layer == depth - 1 else None)
    return (xp.reshape(n_seqs, seq_len, d), xs.reshape(dec_seqs, dec_len, d),
            ret_p, ret_s, hg_p, hg_s)
```

```python
import functools
import math

import jax
import jax.numpy as jnp
import numpy as np
from jax import lax
from jax.experimental import pallas as pl
from jax.experimental.pallas import tpu as pltpu

F32 = jnp.float32
BF16 = jnp.bfloat16

EPS = 1e-6
ROPE_BASE = 10000.0
PAST_LEN = 16384
RET_HEADS = 4
RET_CHUNK = 256
RET_HEADS_PER_STEP = 2
RET_CHUNKS_PER_ITER = 2
HG_HEADS = 8
HG_CHUNK = 128
HG_HEADS_PER_STEP = 4
HG_CHUNKS_PER_ITER = 4
HG_FAST_BLOCK = 32
HG_FAST_LIMIT = 120.0
TOKEN_TILE = 1024
MIX_FFN_TILE = 512
FFN_COLS = 256
LANE = 128
SUBLANE = 8
VMEM_LIMIT = 56 * 1024 * 1024
LOG2E = 1.4426950408889634


def _cparams(n_axes):
    return pltpu.CompilerParams(
        dimension_semantics=("arbitrary",) * n_axes,
        vmem_limit_bytes=VMEM_LIMIT)


def _resident(shape, index=None):
    index = (0,) * len(shape) if index is None else index
    return pl.BlockSpec(shape, lambda *_: index, pipeline_mode=pl.Buffered(1))


def _dot(a, b):
    return jnp.dot(a, b, preferred_element_type=F32)


def _dot_nt(a, b):
    return lax.dot_general(a, b, (((1,), (1,)), ((), ())),
                           preferred_element_type=F32)


def _dot_tn(a, b):
    return lax.dot_general(a, b, (((0,), (0,)), ((), ())),
                           preferred_element_type=F32)


def _rmsnorm(x, gain):
    ms = jnp.mean(x * x, axis=-1, keepdims=True)
    return x * lax.rsqrt(ms + EPS) * gain


def _exp_neg(x):
    return jnp.exp2(x * (-LOG2E))


def _silu(x):
    return x * (1.0 / (1.0 + _exp_neg(x)))


def _run_staged(gens):
    results = [None] * len(gens)
    live = list(range(len(gens)))
    while live:
        still = []
        for i in live:
            try:
                next(gens[i])
                still.append(i)
            except StopIteration as stop:
                results[i] = stop.value
        live = still
    return results


def _token_specs(tm, ts, width):
    return (pl.BlockSpec((tm, width), lambda i: (i, 0)),
            pl.BlockSpec((ts, width), lambda i: (i, 0)))


def _token_tiles(xp, xs, tile=None):
    n_prompt, n_sample = xp.shape[0], xs.shape[0]
    tm = math.gcd(n_prompt, TOKEN_TILE if tile is None else tile)
    steps = n_prompt // tm
    ts = n_sample // steps
    assert ts * steps == n_sample and ts % 16 == 0
    return tm, ts, steps


def _load_normed(xp_ref, xs_ref, g_ref, xn_ref):
    tm = xp_ref.shape[0]
    xn_ref[:tm, :] = _rmsnorm(xp_ref[...], g_ref[...]).astype(BF16)
    xn_ref[tm:, :] = _rmsnorm(xs_ref[...], g_ref[...]).astype(BF16)


def _ffn_body(xp, xs, g_ref, wu_ref, wd_ref, op_ref, os_ref, xn_ref, h_ref, fg_ref=None):
    tm = xp.shape[0]
    d_ff = wd_ref.shape[0]
    xn_ref[:tm, :] = _rmsnorm(xp, g_ref[...]).astype(BF16)
    xn_ref[tm:, :] = _rmsnorm(xs, g_ref[...]).astype(BF16)
    xn = xn_ref[...]
    for c in range(d_ff // FFN_COLS):
        lo = c * FFN_COLS
        a = _dot(xn, wu_ref[:, lo:lo + FFN_COLS])
        b = _dot(xn, wu_ref[:, d_ff + lo:d_ff + lo + FFN_COLS])
        h_ref[:, lo:lo + FFN_COLS] = (_silu(a) * b).astype(BF16)
    y = _dot(h_ref[...], wd_ref[...])
    yp = xp + 0.5 * y[:tm]
    ys = xs + 0.5 * y[tm:]
    if fg_ref is not None:
        yp, ys = _rmsnorm(yp, fg_ref[...]), _rmsnorm(ys, fg_ref[...])
    op_ref[...] = yp
    os_ref[...] = ys


def _ffn_kernel(xp_ref, xs_ref, g_ref, wu_ref, wd_ref, op_ref, os_ref, xn_ref, h_ref):
    _ffn_body(xp_ref[...], xs_ref[...], g_ref, wu_ref, wd_ref, op_ref, os_ref, xn_ref, h_ref)


def _mix_ffn_kernel(xp_ref, xs_ref, ap_ref, as_ref, *rest, gated):
    ap, a_s = ap_ref[...], as_ref[...]
    if gated:
        ap, a_s = ap * rest[0][...], a_s * rest[1][...]
        rest = rest[2:]
    wo_ref, g_ref, wu_ref, wd_ref = rest[:4]
    fg_ref = rest[4] if len(rest) == 9 else None
    xp = xp_ref[...] + _dot(ap, wo_ref[...])
    xs = xs_ref[...] + _dot(a_s, wo_ref[...])
    _ffn_body(xp, xs, g_ref, wu_ref, wd_ref, *rest[-4:], fg_ref=fg_ref)


def _ffn(xp, xs, gains, w_up, w_down, layer, which, mix=None, final_gain=None):
    d = xp.shape[1]
    d_ff = w_down.shape[2]
    tm, ts, steps = _token_tiles(xp, xs, None if mix is None else MIX_FFN_TILE)
    assert d_ff % FFN_COLS == 0
    xspec = _token_specs(tm, ts, d)
    gain = _resident((None, None, 1, d), (layer, 2 * which, 0, 0))
    weights = [_resident((None, None, d, 2 * d_ff), (layer, which, 0, 0)),
               _resident((None, None, d_ff, d), (layer, which, 0, 0))]
    gains4 = gains.reshape(gains.shape[0], gains.shape[1], 1, d)
    if mix is None:
        body, in_specs, args = _ffn_kernel, [*xspec, gain, *weights], (xp, xs, gains4, w_up, w_down)
    else:
        ap, a_s, w_out, j, gates = mix
        k = ap.shape[1]
        body = functools.partial(_mix_ffn_kernel, gated=gates is not None)
        in_specs = [*xspec, *_token_specs(tm, ts, k)]
        args = (xp, xs, ap, a_s)
        if gates is not None:
            in_specs += _token_specs(tm, ts, k)
            args += tuple(gates)
        in_specs += [_resident((None, k, d), (j, 0, 0)), gain, *weights]
        args += (w_out, gains4, w_up, w_down)
        if final_gain is not None:
            in_specs.append(_resident((1, d)))
            args += (final_gain.reshape(1, d),)
    return pl.pallas_call(
        body,
        out_shape=(jax.ShapeDtypeStruct(xp.shape, F32), jax.ShapeDtypeStruct(xs.shape, F32)),
        grid=(steps,),
        in_specs=in_specs,
        out_specs=xspec,
        scratch_shapes=[pltpu.VMEM((tm + ts, d), BF16), pltpu.VMEM((tm + ts, d_ff), BF16)],
        compiler_params=_cparams(1),
        name="ffn" if mix is None else "mix_ffn",
    )(*args)


def _ret_proj_kernel(xp_ref, xs_ref, g_ref, w_ref, cp_ref, sp_ref, cs_ref, ss_ref,
                     qp_ref, qs_ref, kp_ref, ks_ref, vp_ref, vs_ref, gp_ref, gs_ref, xn_ref,
                     *, dk, dv, heads):
    tm = xp_ref.shape[0]
    _load_normed(xp_ref, xs_ref, g_ref, xn_ref)
    xn = xn_ref[...]
    cos = jnp.concatenate([cp_ref[...], cs_ref[...]], axis=0)
    sin = jnp.concatenate([sp_ref[...], ss_ref[...]], axis=0)
    half = dk // 2
    qk = heads * dk
    k_scale = dk ** -0.5

    def put(p_ref, s_ref, lo, val):
        val = val.astype(BF16)
        p_ref[:, lo:lo + val.shape[1]] = val[:tm]
        s_ref[:, lo:lo + val.shape[1]] = val[tm:]

    for h in range(heads):
        for base, p_ref, s_ref, scale in ((0, qp_ref, qs_ref, None), (qk, kp_ref, ks_ref, k_scale)):
            t = _dot(xn, w_ref[:, base + h * dk:base + (h + 1) * dk])
            x1, x2 = t[:, :half], t[:, half:]
            r1 = x1 * cos - x2 * sin
            r2 = x1 * sin + x2 * cos
            if scale is not None:
                r1, r2 = r1 * scale, r2 * scale
            put(p_ref, s_ref, h * dk, r1)
            put(p_ref, s_ref, h * dk + half, r2)
    for h in range(heads):
        lo = 2 * qk + h * dv
        put(vp_ref, vs_ref, h * dv, _dot(xn, w_ref[:, lo:lo + dv]))
        lo = 2 * qk + heads * dv + h * dv
        put(gp_ref, gs_ref, h * dv, _silu(_dot(xn, w_ref[:, lo:lo + dv])))


def _rope_tables(pos, half):
    inv_freq = ROPE_BASE ** (-jnp.arange(half, dtype=F32) / half)
    ang = pos[:, None] * inv_freq[None, :]
    return jnp.cos(ang), jnp.sin(ang)


def _ret_proj(xp, xs, gains, w_in, layer, j, seq_len, dec_len):
    d = xp.shape[1]
    heads = RET_HEADS
    dk = d // heads
    dv = 2 * dk
    half = dk // 2
    tm, ts, steps = _token_tiles(xp, xs)
    assert seq_len % tm == 0 and ts % dec_len == 0
    tiles_per_seq = seq_len // tm
    cos_p, sin_p = _rope_tables(jnp.arange(seq_len, dtype=F32), half)
    cos_s, sin_s = _rope_tables(
        jnp.tile(PAST_LEN + jnp.arange(dec_len, dtype=F32), ts // dec_len), half)
    tab_p = pl.BlockSpec((tm, half), lambda i: (i % tiles_per_seq, 0))
    widths = (heads * dk, heads * dk, heads * dv, heads * dv)
    out_shape, out_specs = [], []
    for w in widths:
        out_shape += [jax.ShapeDtypeStruct((xp.shape[0], w), BF16),
                      jax.ShapeDtypeStruct((xs.shape[0], w), BF16)]
        out_specs += list(_token_specs(tm, ts, w))
    return pl.pallas_call(
        functools.partial(_ret_proj_kernel, dk=dk, dv=dv, heads=heads),
        out_shape=tuple(out_shape),
        grid=(steps,),
        in_specs=[*_token_specs(tm, ts, d),
                  _resident((None, None, 1, d), (layer, 1, 0, 0)),
                  _resident((None,) + w_in.shape[1:], (j, 0, 0)),
                  tab_p, tab_p, _resident((ts, half)), _resident((ts, half))],
        out_specs=tuple(out_specs),
        scratch_shapes=[pltpu.VMEM((tm + ts, d), BF16)],
        compiler_params=_cparams(1),
        name="ret_proj",
    )(xp, xs, gains.reshape(gains.shape[0], gains.shape[1], 1, d), w_in,
      cos_p, sin_p, cos_s, sin_s)


def _head_norm(o, gn):
    ms = jnp.mean(o * o, axis=-1, keepdims=True)
    return o * lax.rsqrt(ms + EPS) * gn


def _head_norm_gate(o, gn, sg):
    return (_head_norm(o, gn) * sg.astype(F32)).astype(BF16)


def _ret_prompt_step(q_ref, k_ref, v_ref, gn_ref, dm_ref, qd_ref, kd_ref, cd_ref,
                     on_ref, s_ref, *, it, cpi, chunk, hps, dk, dv):
    h0 = pl.program_id(1) * hps

    @pl.when(it == 0)
    def _():
        for hh in range(hps):
            s_ref[0, hh] = jnp.zeros((dk, dv), F32)

    def head_chain(hh, rows):
        qs = [q_ref[r, hh * dk:(hh + 1) * dk] for r in rows]
        ks = [k_ref[r, hh * dk:(hh + 1) * dk] for r in rows]
        vs = [v_ref[r, hh * dv:(hh + 1) * dv] for r in rows]
        s = s_ref[0, hh]
        cd = cd_ref[h0 + hh]
        scores = [_dot_nt(qc, kc) for qc, kc in zip(qs, ks)]
        updates = [_dot_tn((kc.astype(F32) * kd_ref[hh]).astype(BF16), vc)
                   for kc, vc in zip(ks, vs)]
        yield
        intra = [_dot((sc * dm_ref[hh]).astype(BF16), vc) for sc, vc in zip(scores, vs)]
        inter = []
        for qc, upd in zip(qs, updates):
            inter.append(_dot(qc, s.astype(BF16)))
            s = s * cd + upd
        yield
        return s, [_head_norm(oi + oc * qd_ref[hh], gn_ref[hh]).astype(BF16)
                   for oi, oc in zip(intra, inter)]

    rows = [pl.ds(cc * chunk, chunk) for cc in range(cpi)]
    results = _run_staged([head_chain(hh, rows) for hh in range(hps)])
    for hh, (s_new, ons) in enumerate(results):
        s_ref[0, hh] = s_new
        for r, on in zip(rows, ons):
            on_ref[r, hh * dv:(hh + 1) * dv] = on


def _ret_decay_tables(chunk):
    heads = RET_HEADS
    log_gamma = jnp.log(1.0 - jnp.power(2.0, -5.0 - jnp.arange(heads, dtype=F32)))
    idx = jnp.arange(chunk, dtype=F32)
    diff = idx[:, None] - idx[None, :]
    lg = log_gamma[:, None, None]
    decay_mat = jnp.where(diff[None] >= 0, jnp.exp(diff[None] * lg), 0.0)
    q_decay = jnp.exp((idx[None, :] + 1.0) * log_gamma[:, None])
    k_decay = jnp.exp((chunk - 1.0 - idx[None, :]) * log_gamma[:, None])
    chunk_decay = jnp.exp(chunk * log_gamma)
    return decay_mat, q_decay, k_decay, chunk_decay


def _state_out(prev, shape, j, block, index_map):
    spec = pl.BlockSpec((None,) + block, lambda *ids: (j,) + index_map(*ids))
    extra_in = [] if prev is None else [prev]
    extra_spec = [] if prev is None else [pl.BlockSpec(memory_space=pl.ANY)]
    return spec, jax.ShapeDtypeStruct(shape, F32), extra_in, extra_spec


def _ret_sample_step(q_ref, k_ref, v_ref, gn_ref, dm_ref, qd_ref, kd_ref, cd_ref,
                     s0_ref, on_ref, s_ref, *, h, seqs, dec_len):
    rows = seqs * dec_len
    qb = q_ref[...]
    vb = v_ref[...]
    kdec = k_ref[...].astype(F32) * kd_ref[0]
    cd = cd_ref[h]
    row_seq = lax.broadcasted_iota(jnp.int32, (rows, 1), 0) // dec_len
    scores = _dot_nt(qb, k_ref[...]) * dm_ref[0]
    o = _dot(scores.astype(BF16), vb)
    inter = jnp.zeros(o.shape, F32)
    for i in range(seqs):
        s_old = s0_ref[i, 0]
        mine = row_seq == i
        inter = jnp.where(mine, _dot(qb, s_old.astype(BF16)), inter)
        k_i = jnp.where(mine, kdec, 0.0).astype(BF16)
        s_ref[i, 0] = s_old * cd + _dot_tn(k_i, vb)
    o = o + inter * qd_ref[0]
    on_ref[...] = _head_norm(o, gn_ref[0]).astype(BF16)


def _ret_core_kernel(*refs, n_prev, heads, cpi, chunk, hps, dk, dv, seqs, dec_len):
    prompt_in, sample_in = refs[:8], refs[8:17]
    og_p, s_p, og_s, s_s = refs[17 + n_prev:]
    n_h, n_it = pl.num_programs(1), pl.num_programs(2)
    unit = (pl.program_id(0) * n_h + pl.program_id(1)) * n_it + pl.program_id(2)
    _ret_sample_step(*sample_in, og_s, s_s, h=unit % heads, seqs=seqs, dec_len=dec_len)
    _ret_prompt_step(*prompt_in, og_p, s_p, it=pl.program_id(2), cpi=cpi, chunk=chunk,
                     hps=hps, dk=dk, dv=dv)


def _ret_core(prompt, sample, gn, s0_all, j, prev_prompt, prev_sample, n_seqs, seq_len, dec_len):
    qp, kp, vp = prompt
    qs, ks, vs = sample
    heads = RET_HEADS
    hps = RET_HEADS_PER_STEP
    n_layers, dec_seqs = s0_all.shape[:2]
    dk, dv = s0_all.shape[3], s0_all.shape[4]
    gn4 = gn.reshape(gn.shape[0], heads, 1, dv)

    chunk = math.gcd(seq_len, RET_CHUNK)
    n_chunks = seq_len // chunk
    cpi = math.gcd(n_chunks, RET_CHUNKS_PER_ITER)
    grid = (n_seqs, heads // hps, n_chunks // cpi)
    dm, qd, kd, cd = _ret_decay_tables(chunk)
    per_step = lambda w: pl.BlockSpec((cpi * chunk, hps * w),
                                      lambda b, h, t: (b * grid[2] + t, h))
    tab = lambda shape: pl.BlockSpec((hps,) + shape, lambda b, h, t: (h, 0, 0))
    sp_spec, sp_shape, prev_p, prev_p_spec = _state_out(
        prev_prompt, (n_layers, n_seqs, heads, dk, dv), j, (1, hps, dk, dv),
        lambda b, h, t: (b, h, 0, 0))
    prompt_specs = [per_step(dk), per_step(dk), per_step(dv),
                    pl.BlockSpec((None, hps, 1, dv), lambda b, h, t: (j, h, 0, 0)),
                    tab((chunk, chunk)), tab((chunk, 1)), tab((chunk, 1)),
                    pl.BlockSpec(memory_space=pltpu.SMEM)]
    prompt_args = (qp, kp, vp, gn4, dm, qd[:, :, None], kd[:, :, None], cd)

    n_units = grid[0] * grid[1] * grid[2]
    seqs = dec_seqs * heads // n_units
    rows = seqs * dec_len
    assert seqs * n_units == dec_seqs * heads and rows % 16 == 0
    assert math.gcd(dec_len, RET_CHUNK) == dec_len
    dm_s, qd_s, kd_s, cd_s = _ret_decay_tables(dec_len)
    seq_id = np.arange(rows) // dec_len
    same = jnp.asarray(seq_id[:, None] == seq_id[None, :])
    dm_blk = jnp.where(same[None], jnp.tile(dm_s, (1, seqs, seqs)), 0.0)
    qd_blk = jnp.tile(qd_s, (1, seqs))[:, :, None]
    kd_blk = jnp.tile(kd_s, (1, seqs))[:, :, None]

    def unit(b, h, t):
        u = (b * grid[1] + h) * grid[2] + t
        return u // heads, u % heads

    per_unit = lambda w: pl.BlockSpec((rows, w), lambda b, h, t: unit(b, h, t))
    tab_s = lambda shape: pl.BlockSpec((1,) + shape, lambda b, h, t: (unit(b, h, t)[1], 0, 0))
    state_in = pl.BlockSpec((None, seqs, 1, dk, dv),
                            lambda b, h, t: (j,) + unit(b, h, t) + (0, 0))
    ss_spec, ss_shape, prev_s, prev_s_spec = _state_out(
        prev_sample, s0_all.shape, j, (seqs, 1, dk, dv),
        lambda b, h, t: unit(b, h, t) + (0, 0))
    sample_specs = [per_unit(dk), per_unit(dk), per_unit(dv),
                    pl.BlockSpec((None, 1, 1, dv),
                                 lambda b, h, t: (j, unit(b, h, t)[1], 0, 0)),
                    tab_s((rows, rows)), tab_s((rows, 1)), tab_s((rows, 1)),
                    pl.BlockSpec(memory_space=pltpu.SMEM), state_in]
    sample_args = (qs, ks, vs, gn4, dm_blk, qd_blk, kd_blk, cd_s, s0_all)

    n_prev = len(prev_p) + len(prev_s)
    aliases = {}
    if prev_p:
        aliases[17] = 1
    if prev_s:
        aliases[17 + len(prev_p)] = 3
    return pl.pallas_call(
        functools.partial(_ret_core_kernel, n_prev=n_prev, heads=heads, cpi=cpi, chunk=chunk,
                          hps=hps, dk=dk, dv=dv, seqs=seqs, dec_len=dec_len),
        out_shape=(jax.ShapeDtypeStruct((qp.shape[0], heads * dv), BF16), sp_shape,
                   jax.ShapeDtypeStruct((qs.shape[0], heads * dv), BF16), ss_shape),
        grid=grid,
        in_specs=prompt_specs + sample_specs + prev_p_spec + prev_s_spec,
        out_specs=(per_step(dv), sp_spec, per_unit(dv), ss_spec),
        input_output_aliases=aliases,
        compiler_params=_cparams(3),
        name="ret_core",
    )(*prompt_args, *sample_args, *prev_p, *prev_s)


def _hg_proj_kernel(xp_ref, xs_ref, g_ref, w_ref, lbl_ref,
                    qp_ref, qs_ref, kp_ref, ks_ref, gp_ref, gs_ref, vp_ref, vs_ref,
                    sp_ref, ss_ref, xn_ref, *, layer, cols):
    tm, d = xp_ref.shape
    _load_normed(xp_ref, xs_ref, g_ref, xn_ref)
    xn = xn_ref[...]
    logits = lbl_ref[...]
    e = jnp.exp(logits - jnp.max(logits, axis=0, keepdims=True))
    sm = e / jnp.sum(e, axis=0, keepdims=True)
    cum = sm[0:1]
    for i in range(1, layer + 1):
        cum = cum + sm[i:i + 1]
    lb = cum - sm[0:1]
    log_lb = jnp.log(lb)
    log_1m_lb = jnp.log1p(-lb)

    def put(p_ref, s_ref, lo, val):
        p_ref[:, lo:lo + cols] = val[:tm].astype(p_ref.dtype)
        s_ref[:, lo:lo + cols] = val[tm:].astype(s_ref.dtype)

    for c in range(d // cols):
        lo = c * cols
        put(qp_ref, qs_ref, lo, _silu(_dot(xn, w_ref[:, lo:lo + cols])))
        z = _dot(xn, w_ref[:, d + lo:d + lo + cols])
        ez = _exp_neg(jnp.abs(z))
        ez1 = 1.0 + ez
        log_sig = jnp.minimum(z, 0.0) - jnp.log(ez1)
        a = log_lb[:, lo:lo + cols]
        b = log_1m_lb[:, lo:lo + cols] + log_sig
        delta = a - b
        g = jnp.where(delta != delta, a + b,
                      jnp.maximum(a, b) + jnp.log(1.0 + _exp_neg(jnp.abs(delta))))
        put(gp_ref, gs_ref, lo, g)
        sig_neg = jnp.where(z >= 0.0, ez, 1.0) * (1.0 / ez1)
        put(kp_ref, ks_ref, lo, (1.0 - lb[:, lo:lo + cols]) * sig_neg)
        put(vp_ref, vs_ref, lo, _dot(xn, w_ref[:, 2 * d + lo:2 * d + lo + cols]))
        put(sp_ref, ss_ref, lo, _silu(_dot(xn, w_ref[:, 3 * d + lo:3 * d + lo + cols])))


def _hg_proj(xp, xs, gains, w_in, lb_logits, layer, j):
    d = xp.shape[1]
    tm, ts, steps = _token_tiles(xp, xs)
    out_shape, out_specs = [], []
    for dt in (BF16, BF16, F32, BF16, BF16):
        out_shape += [jax.ShapeDtypeStruct(xp.shape, dt), jax.ShapeDtypeStruct(xs.shape, dt)]
        out_specs += list(_token_specs(tm, ts, d))
    return pl.pallas_call(
        functools.partial(_hg_proj_kernel, layer=j, cols=2 * LANE),
        out_shape=tuple(out_shape),
        grid=(steps,),
        in_specs=[*_token_specs(tm, ts, d),
                  _resident((None, None, 1, d), (layer, 1, 0, 0)),
                  _resident((None,) + w_in.shape[1:], (j, 0, 0)),
                  _resident(lb_logits.shape)],
        out_specs=tuple(out_specs),
        scratch_shapes=[pltpu.VMEM((tm + ts, d), BF16)],
        compiler_params=_cparams(1),
        name="hg_proj",
    )(xp, xs, gains.reshape(gains.shape[0], gains.shape[1], 1, d), w_in, lb_logits)


def _hg_tables(chunk, block):
    n_levels = int(math.log2(block))
    assert 2 ** n_levels == block and chunk % block == 0 and chunk == LANE
    t = np.arange(chunk)[:, None]
    s = np.arange(chunk)[None, :]
    same_run = (t // block) == (s // block)
    sums = np.concatenate([same_run & (s <= t), same_run & (s > t)], axis=0)
    masks = [t == s]
    for l in range(n_levels):
        m = 2 ** l
        mid = (t // (2 * m)) * (2 * m) + m
        masks.append(((t // (2 * m)) == (s // (2 * m))) & (t >= mid) & (s < mid))
    r = np.arange(chunk)
    rows = np.stack([r % 2 == 1, r % 4 == 0, r % 4 >= 2, r % 4 == 3,
                     np.where(r % 8 >= 4, 1.0, -1.0)]).astype(np.float32)
    rows = np.broadcast_to(rows[:, :, None], rows.shape + (LANE,))
    block_causal = ((t // HG_FAST_BLOCK) == (s // HG_FAST_BLOCK)) & (s <= t)
    rows = np.concatenate([rows, block_causal[None].astype(np.float32)])
    sums = jnp.asarray(sums.astype(np.float32), dtype=BF16)
    masks = jnp.asarray(np.stack(masks).astype(np.float32), dtype=BF16)
    return sums, masks, jnp.asarray(rows)


def _level_exponents(b, g2, b_ref, rows_ref, block):
    c = b.shape[0]
    out = []
    n_levels = int(math.log2(block))
    for l in range(n_levels):
        m = 2 ** l
        if m == 1:
            out.append(g2 * rows_ref[0])
        elif m == 2:
            nxt = pltpu.roll(g2, c - 1, 0)
            prv = pltpu.roll(g2, 1, 0)
            out.append(nxt * rows_ref[1] + g2 * rows_ref[2] + prv * rows_ref[3])
        elif 2 * m == SUBLANE:
            pieces = [b[lo:lo + SUBLANE] - b_ref[pl.ds(lo + m - 1, 1), :]
                      for lo in range(0, c, SUBLANE)]
            out.append(jnp.concatenate(pieces, axis=0) * rows_ref[4])
        else:
            pieces = []
            for lo in range(0, c, 2 * m):
                ref_row = b_ref[pl.ds(lo + m - 1, 1), :]
                pieces.append(ref_row - b[lo:lo + m])
                pieces.append(b[lo + m:lo + 2 * m] - ref_row)
            out.append(jnp.concatenate(pieces, axis=0))
    return out


def _hg_chunk(q, k, v, g, sums_ref, masks_ref, rows_ref, b_ref, block):
    c = q.shape[0]
    g2 = g * LOG2E
    g_hi = g2.astype(BF16)
    g_lo = (g2 - g_hi.astype(F32)).astype(BF16)
    g_split = jnp.concatenate([g_hi, g_lo], axis=1)
    both = _dot(sums_ref[:c, :] if block == c else sums_ref[...], g_split)
    yield
    both = both[:, :LANE] + both[:, LANE:]
    if block == c:
        b = both
        b_rest = b[c - 1:c, :] - b
    else:
        b, b_rest = both[:c], both[c:]
    b_ref[...] = b
    level_scores = [_dot_nt(q, k)]
    for nd in _level_exponents(b, g2, b_ref, rows_ref, block):
        e = jnp.exp2(nd).astype(BF16)
        level_scores.append(_dot_nt(q * e, k * e))
    eb = jnp.exp2(b)
    q_dec = q * eb.astype(BF16)
    k_dec = k * jnp.exp2(b_rest).astype(BF16)
    yield
    scores = level_scores[0].astype(BF16) * masks_ref[0]
    for l, s in enumerate(level_scores[1:]):
        scores = scores + s.astype(BF16) * masks_ref[1 + l]
    o_intra = _dot(scores, v)
    return o_intra, q_dec, k_dec, eb


def _hg_chunk_fast(q, k, v, g, sums_ref, rows_ref):
    c = q.shape[0]
    blk = HG_FAST_BLOCK
    g2 = g * LOG2E
    g_hi = g2.astype(BF16)
    g_lo = (g2 - g_hi.astype(F32)).astype(BF16)
    b = _dot(sums_ref[:c, :], jnp.concatenate([g_hi, g_lo], axis=1))
    yield
    b = b[:, :LANE] + b[:, LANE:]
    qf = q.astype(F32)
    kf = k.astype(F32)
    q_blocks, k_blocks, cross = [], [], [jnp.zeros((blk, c), F32)]
    for i in range(c // blk):
        rows = slice(i * blk, (i + 1) * blk)
        if i == 0:
            d = b[rows]
        else:
            r = b[i * blk - 1:i * blk, :]
            d = b[rows] - r
        q_i = (qf[rows] * jnp.exp2(d)).astype(BF16)
        q_blocks.append(q_i)
        k_blocks.append((kf[rows] * jnp.exp2(-d)).astype(BF16))
        if i > 0:
            k_before = (kf[:i * blk] * jnp.exp2(r - b[:i * blk])).astype(BF16)
            k_before = jnp.concatenate(
                [k_before, jnp.zeros((c - i * blk, LANE), BF16)], axis=0)
            cross.append(_dot_nt(q_i, k_before))
    within = _dot_nt(jnp.concatenate(q_blocks, axis=0), jnp.concatenate(k_blocks, axis=0))
    eb = jnp.exp2(b)
    q_dec = (qf * eb).astype(BF16)
    k_dec = (kf * jnp.exp2(b[c - 1:c, :] - b)).astype(BF16)
    yield
    scores = within * rows_ref[5] + jnp.concatenate(cross, axis=0)
    o_intra = _dot(scores.astype(BF16), v)
    return o_intra, q_dec, k_dec, eb


def _hg_prompt_step(q_ref, k_ref, v_ref, g_ref, sg_ref, gn_ref, sums_ref, masks_ref,
                    rows_ref, blocks_ref, og_ref, s_ref, st_ref, b_ref,
                    *, it, last_it, cpi, chunk, hps, dh):
    @pl.when(it == 0)
    def _():
        st_ref[...] = jnp.zeros(st_ref.shape, F32)

    def run(fast):
        rows = [pl.ds(cc * chunk, chunk) for cc in range(cpi)]
        heads = [slice(hh * dh, (hh + 1) * dh) for hh in range(hps)]
        pairs = [(hh, cc) for hh in range(hps) for cc in range(cpi)]
        vs = {(hh, cc): v_ref[rows[cc], heads[hh]] for hh, cc in pairs}
        chains = []
        for hh, cc in pairs:
            args = (q_ref[rows[cc], heads[hh]], k_ref[rows[cc], heads[hh]], vs[hh, cc],
                    g_ref[rows[cc], heads[hh]])
            if fast:
                chains.append(_hg_chunk_fast(*args, sums_ref, rows_ref))
            else:
                chains.append(_hg_chunk(*args, sums_ref, masks_ref, rows_ref,
                                        b_ref.at[hh * cpi + cc], chunk))
        intra = dict(zip(pairs, _run_staged(chains)))
        updates = {p: _dot_tn(vs[p], intra[p][2]) for p in pairs}
        states = [st_ref[hh] for hh in range(hps)]
        outs = {}
        for cc in range(cpi):
            inters = [_dot_nt(intra[hh, cc][1], states[hh].astype(BF16)) for hh in range(hps)]
            for hh in range(hps):
                o_intra, _, _, eb = intra[hh, cc]
                states[hh] = states[hh] * eb[chunk - 1:chunk, :] + updates[hh, cc]
                outs[hh, cc] = _head_norm_gate(o_intra + inters[hh], gn_ref[hh],
                                               sg_ref[rows[cc], heads[hh]])
        for hh in range(hps):
            st_ref[hh] = states[hh]
        for hh, cc in pairs:
            og_ref[rows[cc], heads[hh]] = outs[hh, cc]

    block_sums = _dot(blocks_ref[...], jnp.abs(g_ref[...]).astype(BF16))
    fast_ok = jnp.max(block_sums) * LOG2E < HG_FAST_LIMIT
    pl.when(fast_ok)(lambda: run(True))
    pl.when(jnp.logical_not(fast_ok))(lambda: run(False))

    @pl.when(it == last_it)
    def _():
        for hh in range(hps):
            s_ref[0, hh] = st_ref[hh].T


def _hg_sample_step(q_ref, k_ref, v_ref, g_ref, sg_ref, gn_ref, sums_ref, masks_ref, rows_ref,
                    s0_ref, og_ref, s_ref, b_ref, *, seqs, dec_len, n_heads, dh):
    rows = seqs * dec_len
    row_seq = lax.broadcasted_iota(jnp.int32, (rows, 1), 0) // dec_len
    col_seq = lax.broadcasted_iota(jnp.int32, (1, rows), 1) // dec_len
    for hh in range(n_heads):
        cols = slice(hh * dh, (hh + 1) * dh)
        vb = v_ref[:, cols]
        (o, q_dec, k_dec, eb), = _run_staged([_hg_chunk(
            q_ref[:, cols], k_ref[:, cols], vb, g_ref[:, cols], sums_ref, masks_ref, rows_ref,
            b_ref, dec_len)])
        eb_t = eb.T
        k_dec_t = k_dec.astype(F32).T
        inter = jnp.zeros(o.shape, F32)
        for i in range(seqs):
            s_old = s0_ref[i, hh]
            inter = jnp.where(row_seq == i, _dot(q_dec, s_old.astype(BF16)), inter)
            k_i = jnp.where(col_seq == i, k_dec_t, 0.0).astype(BF16)
            last = (i + 1) * dec_len - 1
            s_ref[i, hh] = s_old * eb_t[:, last:last + 1] + _dot(k_i, vb)
        og_ref[:, cols] = _head_norm_gate(o + inter, gn_ref[hh], sg_ref[:, cols])


def _hg_core_kernel(*refs, n_prev, steps_per_unit, prompt_kw, sample_kw):
    prompt_in, sample_in = refs[:10], refs[10:20]
    og_p, s_p, og_s, s_s, st_ref, b_ref, bs_ref = refs[20 + n_prev:]
    n_h, n_it = pl.num_programs(1), pl.num_programs(2)
    step = (pl.program_id(0) * n_h + pl.program_id(1)) * n_it + pl.program_id(2)

    @pl.when(step % steps_per_unit == 0)
    def _():
        _hg_sample_step(*sample_in, og_s, s_s, bs_ref, **sample_kw)

    _hg_prompt_step(*prompt_in, og_p, s_p, st_ref, b_ref, it=pl.program_id(2),
                    last_it=n_it - 1, **prompt_kw)


def _hg_core(prompt, sample, gn, s0_all, j, prev_prompt, prev_sample, n_seqs, seq_len, dec_len):
    heads = HG_HEADS
    hps = HG_HEADS_PER_STEP
    n_layers, dec_seqs = s0_all.shape[:2]
    dh = s0_all.shape[3]
    d = heads * dh
    gn4 = gn.reshape(gn.shape[0], heads, 1, dh)

    chunk = math.gcd(seq_len, HG_CHUNK)
    n_chunks = seq_len // chunk
    cpi = math.gcd(n_chunks, HG_CHUNKS_PER_ITER)
    grid = (n_seqs, heads // hps, n_chunks // cpi)
    step_rows = cpi * chunk
    sums, masks, row_tabs = _hg_tables(chunk, chunk)
    block_of_row = np.arange(step_rows) // HG_FAST_BLOCK
    blocks = jnp.asarray(np.arange(step_rows // HG_FAST_BLOCK)[:, None] == block_of_row[None, :],
                         dtype=BF16)
    per_step = pl.BlockSpec((step_rows, hps * dh), lambda b, h, t: (b * grid[2] + t, h))
    sp_spec, sp_shape, prev_p, prev_p_spec = _state_out(
        prev_prompt, (n_layers, n_seqs, heads, dh, dh), j, (1, hps, dh, dh),
        lambda b, h, t: (b, h, 0, 0))
    prompt_specs = [per_step] * 5 + [
        pl.BlockSpec((None, hps, 1, dh), lambda b, h, t: (j, h, 0, 0)),
        _resident(sums.shape), _resident(masks.shape), _resident(row_tabs.shape),
        _resident(blocks.shape)]
    prompt_args = (*prompt, gn4, sums, masks, row_tabs, blocks)

    rows = math.gcd(dec_seqs * dec_len, HG_CHUNK)
    seqs = rows // dec_len
    n_steps = grid[0] * grid[1] * grid[2]
    seq_blocks = dec_seqs // seqs
    hu = max(1, seq_blocks * heads // n_steps)
    units_per_block = heads // hu
    spu = n_steps // (seq_blocks * units_per_block)
    assert seqs * dec_len == rows and seq_blocks * units_per_block * spu == n_steps
    sums_s, masks_s, row_tabs_s = _hg_tables(rows, dec_len)

    def unit(b, h, t):
        u = ((b * grid[1] + h) * grid[2] + t) // spu
        return u // units_per_block, u % units_per_block

    per_unit = pl.BlockSpec((rows, hu * dh), lambda b, h, t: unit(b, h, t))
    state_in = pl.BlockSpec((None, seqs, hu, dh, dh),
                            lambda b, h, t: (j,) + unit(b, h, t) + (0, 0))
    ss_spec, ss_shape, prev_s, prev_s_spec = _state_out(
        prev_sample, s0_all.shape, j, (seqs, hu, dh, dh),
        lambda b, h, t: unit(b, h, t) + (0, 0))
    sample_specs = [per_unit] * 5 + [
        pl.BlockSpec((None, hu, 1, dh), lambda b, h, t: (j, unit(b, h, t)[1], 0, 0)),
        _resident(sums_s.shape), _resident(masks_s.shape), _resident(row_tabs_s.shape),
        state_in]
    sample_args = (*sample, gn4, sums_s, masks_s, row_tabs_s, s0_all)

    aliases = {}
    if prev_p:
        aliases[20] = 1
    if prev_s:
        aliases[20 + len(prev_p)] = 3
    return pl.pallas_call(
        functools.partial(
            _hg_core_kernel, n_prev=len(prev_p) + len(prev_s), steps_per_unit=spu,
            prompt_kw=dict(cpi=cpi, chunk=chunk, hps=hps, dh=dh),
            sample_kw=dict(seqs=seqs, dec_len=dec_len, n_heads=hu, dh=dh)),
        out_shape=(jax.ShapeDtypeStruct((n_seqs * seq_len, d), BF16), sp_shape,
                   jax.ShapeDtypeStruct((dec_seqs * dec_len, d), BF16), ss_shape),
        grid=grid,
        in_specs=prompt_specs + sample_specs + prev_p_spec + prev_s_spec,
        out_specs=(per_step, sp_spec, per_unit, ss_spec),
        input_output_aliases=aliases,
        scratch_shapes=[pltpu.VMEM((hps, dh, dh), F32),
                        pltpu.VMEM((hps * cpi, chunk, dh), F32),
                        pltpu.VMEM((rows, dh), F32)],
        compiler_params=_cparams(3),
        name="hg_core",
    )(*prompt_args, *sample_args, *prev_p, *prev_s)


def kernel(x_prompt, x_sample, state_ret, state_hgrn, norm_gain, ffn_w_up, ffn_w_down,
           ret_w_in, ret_norm, ret_w_out, hg_w_in, hg_lb_logits, hg_norm, hg_w_out, final_norm):
    n_seqs, seq_len, d = x_prompt.shape
    dec_seqs, dec_len, _ = x_sample.shape
    depth = norm_gain.shape[0]
    n_ret, n_hg = state_ret.shape[0], state_hgrn.shape[0]

    xp = x_prompt.reshape(n_seqs * seq_len, d)
    xs = x_sample.reshape(dec_seqs * dec_len, d)
    w_up, w_down = ffn_w_up.astype(BF16), ffn_w_down.astype(BF16)
    ret_in, ret_out = ret_w_in.astype(BF16), ret_w_out.astype(BF16)
    hg_in, hg_out = hg_w_in.astype(BF16), hg_w_out.astype(BF16)

    ret_p = ret_s = hg_p = hg_s = None
    for layer in range(depth):
        xp, xs = _ffn(xp, xs, norm_gain, w_up, w_down, layer, 0)
        j = layer // 2
        if layer % 2 == 0:
            qp, qs, kp, ks, vp, vs, gp, gs = _ret_proj(
                xp, xs, norm_gain, ret_in, layer, j, seq_len, dec_len)
            ap, ret_p, a_s, ret_s = _ret_core(
                (qp, kp, vp), (qs, ks, vs), ret_norm, state_ret, j, ret_p, ret_s,
                n_seqs, seq_len, dec_len)
            mix = (ap, a_s, ret_out, j, (gp, gs))
        else:
            qp, qs, kp, ks, gp, gs, vp, vs, sp, ss = _hg_proj(
                xp, xs, norm_gain, hg_in, hg_lb_logits, layer, j)
            ap, hg_p, a_s, hg_s = _hg_core(
                (qp, kp, vp, gp, sp), (qs, ks, vs, gs, ss), hg_norm, state_hgrn, j, hg_p, hg_s,
                n_seqs, seq_len, dec_len)
            mix = (ap, a_s, hg_out, j, None)
        xp, xs = _ffn(xp, xs, norm_gain, w_up, w_down, layer, 1, mix,
                      final_norm if layer == depth - 1 else None)
    return (xp.reshape(n_seqs, seq_len, d), xs.reshape(dec_seqs, dec_len, d),
            ret_p, ret_s, hg_p, hg_s)
```

```python
import functools
import math

import jax
import jax.numpy as jnp
import numpy as np
from jax import lax
from jax.experimental import pallas as pl
from jax.experimental.pallas import tpu as pltpu

F32 = jnp.float32
BF16 = jnp.bfloat16

EPS = 1e-6
ROPE_BASE = 10000.0
PAST_LEN = 16384
RET_HEADS = 4
RET_CHUNK = 256
RET_HEADS_PER_STEP = 2
RET_CHUNKS_PER_ITER = 2
HG_HEADS = 8
HG_CHUNK = 128
HG_HEADS_PER_STEP = 4
HG_CHUNKS_PER_ITER = 4
HG_FAST_BLOCK = 32
HG_FAST_LIMIT = 120.0
TOKEN_TILE = 1024
MIX_FFN_TILE = 512
FFN_COLS = 256
LANE = 128
SUBLANE = 8
VMEM_LIMIT = 56 * 1024 * 1024
LOG2E = 1.4426950408889634


def _cparams(n_axes):
    return pltpu.CompilerParams(
        dimension_semantics=("arbitrary",) * n_axes,
        vmem_limit_bytes=VMEM_LIMIT)


def _resident(shape, index=None):
    index = (0,) * len(shape) if index is None else index
    return pl.BlockSpec(shape, lambda *_: index, pipeline_mode=pl.Buffered(1))


def _dot(a, b):
    return jnp.dot(a, b, preferred_element_type=F32)


def _dot_nt(a, b):
    return lax.dot_general(a, b, (((1,), (1,)), ((), ())),
                           preferred_element_type=F32)


def _dot_tn(a, b):
    return lax.dot_general(a, b, (((0,), (0,)), ((), ())),
                           preferred_element_type=F32)


def _rmsnorm(x, gain):
    ms = jnp.mean(x * x, axis=-1, keepdims=True)
    return x * lax.rsqrt(ms + EPS) * gain


def _exp_neg(x):
    return jnp.exp2(x * (-LOG2E))


def _silu(x):
    return x * (1.0 / (1.0 + _exp_neg(x)))


def _run_staged(gens):
    results = [None] * len(gens)
    live = list(range(len(gens)))
    while live:
        still = []
        for i in live:
            try:
                next(gens[i])
                still.append(i)
            except StopIteration as stop:
                results[i] = stop.value
        live = still
    return results


def _token_specs(tm, ts, width):
    return (pl.BlockSpec((tm, width), lambda i: (i, 0)),
            pl.BlockSpec((ts, width), lambda i: (i, 0)))


def _token_tiles(xp, xs, tile=None):
    n_prompt, n_sample = xp.shape[0], xs.shape[0]
    tm = math.gcd(n_prompt, TOKEN_TILE if tile is None else tile)
    steps = n_prompt // tm
    ts = n_sample // steps
    assert ts * steps == n_sample and ts % 16 == 0
    return tm, ts, steps


def _load_normed(xp_ref, xs_ref, g_ref, xn_ref):
    tm = xp_ref.shape[0]
    xn_ref[:tm, :] = _rmsnorm(xp_ref[...], g_ref[...]).astype(BF16)
    xn_ref[tm:, :] = _rmsnorm(xs_ref[...], g_ref[...]).astype(BF16)


def _ffn_body(xp, xs, g_ref, wu_ref, wd_ref, op_ref, os_ref, xn_ref, h_ref, fg_ref=None):
    tm = xp.shape[0]
    d_ff = wd_ref.shape[0]
    xn_ref[:tm, :] = _rmsnorm(xp, g_ref[...]).astype(BF16)
    xn_ref[tm:, :] = _rmsnorm(xs, g_ref[...]).astype(BF16)
    xn = xn_ref[...]
    for c in range(d_ff // FFN_COLS):
        lo = c * FFN_COLS
        a = _dot(xn, wu_ref[:, lo:lo + FFN_COLS])
        b = _dot(xn, wu_ref[:, d_ff + lo:d_ff + lo + FFN_COLS])
        h_ref[:, lo:lo + FFN_COLS] = (_silu(a) * b).astype(BF16)
    y = _dot(h_ref[...], wd_ref[...])
    yp = xp + 0.5 * y[:tm]
    ys = xs + 0.5 * y[tm:]
    if fg_ref is not None:
        yp, ys = _rmsnorm(yp, fg_ref[...]), _rmsnorm(ys, fg_ref[...])
    op_ref[...] = yp
    os_ref[...] = ys


def _ffn_kernel(xp_ref, xs_ref, g_ref, wu_ref, wd_ref, op_ref, os_ref, xn_ref, h_ref):
    _ffn_body(xp_ref[...], xs_ref[...], g_ref, wu_ref, wd_ref, op_ref, os_ref, xn_ref, h_ref)


def _mix_ffn_kernel(xp_ref, xs_ref, ap_ref, as_ref, *rest, gated):
    ap, a_s = ap_ref[...], as_ref[...]
    if gated:
        ap, a_s = ap * rest[0][...], a_s * rest[1][...]
        rest = rest[2:]
    wo_ref, g_ref, wu_ref, wd_ref = rest[:4]
    fg_ref = rest[4] if len(rest) == 9 else None
    xp = xp_ref[...] + _dot(ap, wo_ref[...])
    xs = xs_ref[...] + _dot(a_s, wo_ref[...])
    _ffn_body(xp, xs, g_ref, wu_ref, wd_ref, *rest[-4:], fg_ref=fg_ref)


def _ffn(xp, xs, gains, w_up, w_down, layer, which, mix=None, final_gain=None):
    d = xp.shape[1]
    d_ff = w_down.shape[2]
    tm, ts, steps = _token_tiles(xp, xs, None if mix is None else MIX_FFN_TILE)
    assert d_ff % FFN_COLS == 0
    xspec = _token_specs(tm, ts, d)
    gain = _resident((None, None, 1, d), (layer, 2 * which, 0, 0))
    weights = [_resident((None, None, d, 2 * d_ff), (layer, which, 0, 0)),
               _resident((None, None, d_ff, d), (layer, which, 0, 0))]
    gains4 = gains.reshape(gains.shape[0], gains.shape[1], 1, d)
    if mix is None:
        body, in_specs, args = _ffn_kernel, [*xspec, gain, *weights], (xp, xs, gains4, w_up, w_down)
    else:
        ap, a_s, w_out, j, gates = mix
        k = ap.shape[1]
        body = functools.partial(_mix_ffn_kernel, gated=gates is not None)
        in_specs = [*xspec, *_token_specs(tm, ts, k)]
        args = (xp, xs, ap, a_s)
        if gates is not None:
            in_specs += _token_specs(tm, ts, k)
            args += tuple(gates)
        in_specs += [_resident((None, k, d), (j, 0, 0)), gain, *weights]
        args += (w_out, gains4, w_up, w_down)
        if final_gain is not None:
            in_specs.append(_resident((1, d)))
            args += (final_gain.reshape(1, d),)
    return pl.pallas_call(
        body,
        out_shape=(jax.ShapeDtypeStruct(xp.shape, F32), jax.ShapeDtypeStruct(xs.shape, F32)),
        grid=(steps,),
        in_specs=in_specs,
        out_specs=xspec,
        scratch_shapes=[pltpu.VMEM((tm + ts, d), BF16), pltpu.VMEM((tm + ts, d_ff), BF16)],
        compiler_params=_cparams(1),
        name="ffn" if mix is None else "mix_ffn",
    )(*args)


def _ret_proj_kernel(xp_ref, xs_ref, g_ref, w_ref, cp_ref, sp_ref, cs_ref, ss_ref,
                     qp_ref, qs_ref, kp_ref, ks_ref, vp_ref, vs_ref, gp_ref, gs_ref, xn_ref,
                     *, dk, dv, heads):
    tm = xp_ref.shape[0]
    _load_normed(xp_ref, xs_ref, g_ref, xn_ref)
    xn = xn_ref[...]
    cos = jnp.concatenate([cp_ref[...], cs_ref[...]], axis=0)
    sin = jnp.concatenate([sp_ref[...], ss_ref[...]], axis=0)
    half = dk // 2
    qk = heads * dk
    k_scale = dk ** -0.5

    def put(p_ref, s_ref, lo, val):
        val = val.astype(BF16)
        p_ref[:, lo:lo + val.shape[1]] = val[:tm]
        s_ref[:, lo:lo + val.shape[1]] = val[tm:]

    for h in range(heads):
        for base, p_ref, s_ref, scale in ((0, qp_ref, qs_ref, None), (qk, kp_ref, ks_ref, k_scale)):
            t = _dot(xn, w_ref[:, base + h * dk:base + (h + 1) * dk])
            x1, x2 = t[:, :half], t[:, half:]
            r1 = x1 * cos - x2 * sin
            r2 = x1 * sin + x2 * cos
            if scale is not None:
                r1, r2 = r1 * scale, r2 * scale
            put(p_ref, s_ref, h * dk, r1)
            put(p_ref, s_ref, h * dk + half, r2)
    for h in range(heads):
        lo = 2 * qk + h * dv
        put(vp_ref, vs_ref, h * dv, _dot(xn, w_ref[:, lo:lo + dv]))
        lo = 2 * qk + heads * dv + h * dv
        put(gp_ref, gs_ref, h * dv, _silu(_dot(xn, w_ref[:, lo:lo + dv])))


def _rope_tables(pos, half):
    inv_freq = ROPE_BASE ** (-jnp.arange(half, dtype=F32) / half)
    ang = pos[:, None] * inv_freq[None, :]
    return jnp.cos(ang), jnp.sin(ang)


def _ret_proj(xp, xs, gains, w_in, layer, j, seq_len, dec_len):
    d = xp.shape[1]
    heads = RET_HEADS
    dk = d // heads
    dv = 2 * dk
    half = dk // 2
    tm, ts, steps = _token_tiles(xp, xs)
    assert seq_len % tm == 0 and ts % dec_len == 0
    tiles_per_seq = seq_len // tm
    cos_p, sin_p = _rope_tables(jnp.arange(seq_len, dtype=F32), half)
    cos_s, sin_s = _rope_tables(
        jnp.tile(PAST_LEN + jnp.arange(dec_len, dtype=F32), ts // dec_len), half)
    tab_p = pl.BlockSpec((tm, half), lambda i: (i % tiles_per_seq, 0))
    widths = (heads * dk, heads * dk, heads * dv, heads * dv)
    out_shape, out_specs = [], []
    for w in widths:
        out_shape += [jax.ShapeDtypeStruct((xp.shape[0], w), BF16),
                      jax.ShapeDtypeStruct((xs.shape[0], w), BF16)]
        out_specs += list(_token_specs(tm, ts, w))
    return pl.pallas_call(
        functools.partial(_ret_proj_kernel, dk=dk, dv=dv, heads=heads),
        out_shape=tuple(out_shape),
        grid=(steps,),
        in_specs=[*_token_specs(tm, ts, d),
                  _resident((None, None, 1, d), (layer, 1, 0, 0)),
                  _resident((None,) + w_in.shape[1:], (j, 0, 0)),
                  tab_p, tab_p, _resident((ts, half)), _resident((ts, half))],
        out_specs=tuple(out_specs),
        scratch_shapes=[pltpu.VMEM((tm + ts, d), BF16)],
        compiler_params=_cparams(1),
        name="ret_proj",
    )(xp, xs, gains.reshape(gains.shape[0], gains.shape[1], 1, d), w_in,
      cos_p, sin_p, cos_s, sin_s)


def _head_norm(o, gn):
    ms = jnp.mean(o * o, axis=-1, keepdims=True)
    return o * lax.rsqrt(ms + EPS) * gn


def _head_norm_gate(o, gn, sg):
    return (_head_norm(o, gn) * sg.astype(F32)).astype(BF16)


def _ret_prompt_step(q_ref, k_ref, v_ref, gn_ref, dm_ref, qd_ref, kd_ref, cd_ref,
                     on_ref, s_ref, *, it, cpi, chunk, hps, dk, dv):
    h0 = pl.program_id(1) * hps

    @pl.when(it == 0)
    def _():
        for hh in range(hps):
            s_ref[0, hh] = jnp.zeros((dk, dv), F32)

    def head_chain(hh, rows):
        qs = [q_ref[r, hh * dk:(hh + 1) * dk] for r in rows]
        ks = [k_ref[r, hh * dk:(hh + 1) * dk] for r in rows]
        vs = [v_ref[r, hh * dv:(hh + 1) * dv] for r in rows]
        s = s_ref[0, hh]
        cd = cd_ref[h0 + hh]
        scores = [_dot_nt(qc, kc) for qc, kc in zip(qs, ks)]
        updates = [_dot_tn((kc.astype(F32) * kd_ref[hh]).astype(BF16), vc)
                   for kc, vc in zip(ks, vs)]
        yield
        intra = [_dot((sc * dm_ref[hh]).astype(BF16), vc) for sc, vc in zip(scores, vs)]
        inter = []
        for qc, upd in zip(qs, updates):
            inter.append(_dot(qc, s.astype(BF16)))
            s = s * cd + upd
        yield
        return s, [_head_norm(oi + oc * qd_ref[hh], gn_ref[hh]).astype(BF16)
                   for oi, oc in zip(intra, inter)]

    rows = [pl.ds(cc * chunk, chunk) for cc in range(cpi)]
    results = _run_staged([head_chain(hh, rows) for hh in range(hps)])
    for hh, (s_new, ons) in enumerate(results):
        s_ref[0, hh] = s_new
        for r, on in zip(rows, ons):
            on_ref[r, hh * dv:(hh + 1) * dv] = on


def _ret_decay_tables(chunk):
    heads = RET_HEADS
    log_gamma = jnp.log(1.0 - jnp.power(2.0, -5.0 - jnp.arange(heads, dtype=F32)))
    idx = jnp.arange(chunk, dtype=F32)
    diff = idx[:, None] - idx[None, :]
    lg = log_gamma[:, None, None]
    decay_mat = jnp.where(diff[None] >= 0, jnp.exp(diff[None] * lg), 0.0)
    q_decay = jnp.exp((idx[None, :] + 1.0) * log_gamma[:, None])
    k_decay = jnp.exp((chunk - 1.0 - idx[None, :]) * log_gamma[:, None])
    chunk_decay = jnp.exp(chunk * log_gamma)
    return decay_mat, q_decay, k_decay, chunk_decay


def _state_out(prev, shape, j, block, index_map):
    spec = pl.BlockSpec((None,) + block, lambda *ids: (j,) + index_map(*ids))
    extra_in = [] if prev is None else [prev]
    extra_spec = [] if prev is None else [pl.BlockSpec(memory_space=pl.ANY)]
    return spec, jax.ShapeDtypeStruct(shape, F32), extra_in, extra_spec


def _ret_sample_step(q_ref, k_ref, v_ref, gn_ref, dm_ref, qd_ref, kd_ref, cd_ref,
                     s0_ref, on_ref, s_ref, *, h, seqs, dec_len):
    rows = seqs * dec_len
    qb = q_ref[...]
    vb = v_ref[...]
    kdec = k_ref[...].astype(F32) * kd_ref[0]
    cd = cd_ref[h]
    row_seq = lax.broadcasted_iota(jnp.int32, (rows, 1), 0) // dec_len
    scores = _dot_nt(qb, k_ref[...]) * dm_ref[0]
    o = _dot(scores.astype(BF16), vb)
    inter = jnp.zeros(o.shape, F32)
    for i in range(seqs):
        s_old = s0_ref[i, 0]
        mine = row_seq == i
        inter = jnp.where(mine, _dot(qb, s_old.astype(BF16)), inter)
        k_i = jnp.where(mine, kdec, 0.0).astype(BF16)
        s_ref[i, 0] = s_old * cd + _dot_tn(k_i, vb)
    o = o + inter * qd_ref[0]
    on_ref[...] = _head_norm(o, gn_ref[0]).astype(BF16)


def _ret_core_kernel(*refs, n_prev, heads, cpi, chunk, hps, dk, dv, seqs, dec_len):
    prompt_in, sample_in = refs[:8], refs[8:17]
    og_p, s_p, og_s, s_s = refs[17 + n_prev:]
    n_h, n_it = pl.num_programs(1), pl.num_programs(2)
    unit = (pl.program_id(0) * n_h + pl.program_id(1)) * n_it + pl.program_id(2)
    _ret_sample_step(*sample_in, og_s, s_s, h=unit % heads, seqs=seqs, dec_len=dec_len)
    _ret_prompt_step(*prompt_in, og_p, s_p, it=pl.program_id(2), cpi=cpi, chunk=chunk,
                     hps=hps, dk=dk, dv=dv)


def _ret_core(prompt, sample, gn, s0_all, j, prev_prompt, prev_sample, n_seqs, seq_len, dec_len):
    qp, kp, vp = prompt
    qs, ks, vs = sample
    heads = RET_HEADS
    hps = RET_HEADS_PER_STEP
    n_layers, dec_seqs = s0_all.shape[:2]
    dk, dv = s0_all.shape[3], s0_all.shape[4]
    gn4 = gn.reshape(gn.shape[0], heads, 1, dv)

    chunk = math.gcd(seq_len, RET_CHUNK)
    n_chunks = seq_len // chunk
    cpi = math.gcd(n_chunks, RET_CHUNKS_PER_ITER)
    grid = (n_seqs, heads // hps, n_chunks // cpi)
    dm, qd, kd, cd = _ret_decay_tables(chunk)
    per_step = lambda w: pl.BlockSpec((cpi * chunk, hps * w),
                                      lambda b, h, t: (b * grid[2] + t, h))
    tab = lambda shape: pl.BlockSpec((hps,) + shape, lambda b, h, t: (h, 0, 0))
    sp_spec, sp_shape, prev_p, prev_p_spec = _state_out(
        prev_prompt, (n_layers, n_seqs, heads, dk, dv), j, (1, hps, dk, dv),
        lambda b, h, t: (b, h, 0, 0))
    prompt_specs = [per_step(dk), per_step(dk), per_step(dv),
                    pl.BlockSpec((None, hps, 1, dv), lambda b, h, t: (j, h, 0, 0)),
                    tab((chunk, chunk)), tab((chunk, 1)), tab((chunk, 1)),
                    pl.BlockSpec(memory_space=pltpu.SMEM)]
    prompt_args = (qp, kp, vp, gn4, dm, qd[:, :, None], kd[:, :, None], cd)

    n_units = grid[0] * grid[1] * grid[2]
    seqs = dec_seqs * heads // n_units
    rows = seqs * dec_len
    assert seqs * n_units == dec_seqs * heads and rows % 16 == 0
    assert math.gcd(dec_len, RET_CHUNK) == dec_len
    dm_s, qd_s, kd_s, cd_s = _ret_decay_tables(dec_len)
    seq_id = np.arange(rows) // dec_len
    same = jnp.asarray(seq_id[:, None] == seq_id[None, :])
    dm_blk = jnp.where(same[None], jnp.tile(dm_s, (1, seqs, seqs)), 0.0)
    qd_blk = jnp.tile(qd_s, (1, seqs))[:, :, None]
    kd_blk = jnp.tile(kd_s, (1, seqs))[:, :, None]

    def unit(b, h, t):
        u = (b * grid[1] + h) * grid[2] + t
        return u // heads, u % heads

    per_unit = lambda w: pl.BlockSpec((rows, w), lambda b, h, t: unit(b, h, t))
    tab_s = lambda shape: pl.BlockSpec((1,) + shape, lambda b, h, t: (unit(b, h, t)[1], 0, 0))
    state_in = pl.BlockSpec((None, seqs, 1, dk, dv),
                            lambda b, h, t: (j,) + unit(b, h, t) + (0, 0))
    ss_spec, ss_shape, prev_s, prev_s_spec = _state_out(
        prev_sample, s0_all.shape, j, (seqs, 1, dk, dv),
        lambda b, h, t: unit(b, h, t) + (0, 0))
    sample_specs = [per_unit(dk), per_unit(dk), per_unit(dv),
                    pl.BlockSpec((None, 1, 1, dv),
                                 lambda b, h, t: (j, unit(b, h, t)[1], 0, 0)),
                    tab_s((rows, rows)), tab_s((rows, 1)), tab_s((rows, 1)),
                    pl.BlockSpec(memory_space=pltpu.SMEM), state_in]
    sample_args = (qs, ks, vs, gn4, dm_blk, qd_blk, kd_blk, cd_s, s0_all)

    n_prev = len(prev_p) + len(prev_s)
    aliases = {}
    if prev_p:
        aliases[17] = 1
    if prev_s:
        aliases[17 + len(prev_p)] = 3
    return pl.pallas_call(
        functools.partial(_ret_core_kernel, n_prev=n_prev, heads=heads, cpi=cpi, chunk=chunk,
                          hps=hps, dk=dk, dv=dv, seqs=seqs, dec_len=dec_len),
        out_shape=(jax.ShapeDtypeStruct((qp.shape[0], heads * dv), BF16), sp_shape,
                   jax.ShapeDtypeStruct((qs.shape[0], heads * dv), BF16), ss_shape),
        grid=grid,
        in_specs=prompt_specs + sample_specs + prev_p_spec + prev_s_spec,
        out_specs=(per_step(dv), sp_spec, per_unit(dv), ss_spec),
        input_output_aliases=aliases,
        compiler_params=_cparams(3),
        name="ret_core",
    )(*prompt_args, *sample_args, *prev_p, *prev_s)


def _hg_proj_kernel(xp_ref, xs_ref, g_ref, w_ref, lbl_ref,
                    qp_ref, qs_ref, kp_ref, ks_ref, gp_ref, gs_ref, vp_ref, vs_ref,
                    sp_ref, ss_ref, xn_ref, *, layer, cols):
    tm, d = xp_ref.shape
    _load_normed(xp_ref, xs_ref, g_ref, xn_ref)
    xn = xn_ref[...]
    logits = lbl_ref[...]
    e = jnp.exp(logits - jnp.max(logits, axis=0, keepdims=True))
    sm = e / jnp.sum(e, axis=0, keepdims=True)
    cum = sm[0:1]
    for i in range(1, layer + 1):
        cum = cum + sm[i:i + 1]
    lb = cum - sm[0:1]
    log_lb = jnp.log(lb)
    log_1m_lb = jnp.log1p(-lb)

    def put(p_ref, s_ref, lo, val):
        p_ref[:, lo:lo + cols] = val[:tm].astype(p_ref.dtype)
        s_ref[:, lo:lo + cols] = val[tm:].astype(s_ref.dtype)

    for c in range(d // cols):
        lo = c * cols
        put(qp_ref, qs_ref, lo, _silu(_dot(xn, w_ref[:, lo:lo + cols])))
        z = _dot(xn, w_ref[:, d + lo:d + lo + cols])
        ez = _exp_neg(jnp.abs(z))
        ez1 = 1.0 + ez
        log_sig = jnp.minimum(z, 0.0) - jnp.log(ez1)
        a = log_lb[:, lo:lo + cols]
        b = log_1m_lb[:, lo:lo + cols] + log_sig
        delta = a - b
        g = jnp.where(delta != delta, a + b,
                      jnp.maximum(a, b) + jnp.log(1.0 + _exp_neg(jnp.abs(delta))))
        put(gp_ref, gs_ref, lo, g)
        sig_neg = jnp.where(z >= 0.0, ez, 1.0) * (1.0 / ez1)
        put(kp_ref, ks_ref, lo, (1.0 - lb[:, lo:lo + cols]) * sig_neg)
        put(vp_ref, vs_ref, lo, _dot(xn, w_ref[:, 2 * d + lo:2 * d + lo + cols]))
        put(sp_ref, ss_ref, lo, _silu(_dot(xn, w_ref[:, 3 * d + lo:3 * d + lo + cols])))


def _hg_proj(xp, xs, gains, w_in, lb_logits, layer, j):
    d = xp.shape[1]
    tm, ts, steps = _token_tiles(xp, xs)
    out_shape, out_specs = [], []
    for dt in (BF16, BF16, F32, BF16, BF16):
        out_shape += [jax.ShapeDtypeStruct(xp.shape, dt), jax.ShapeDtypeStruct(xs.shape, dt)]
        out_specs += list(_token_specs(tm, ts, d))
    return pl.pallas_call(
        functools.partial(_hg_proj_kernel, layer=j, cols=2 * LANE),
        out_shape=tuple(out_shape),
        grid=(steps,),
        in_specs=[*_token_specs(tm, ts, d),
                  _resident((None, None, 1, d), (layer, 1, 0, 0)),
                  _resident((None,) + w_in.shape[1:], (j, 0, 0)),
                  _resident(lb_logits.shape)],
        out_specs=tuple(out_specs),
        scratch_shapes=[pltpu.VMEM((tm + ts, d), BF16)],
        compiler_params=_cparams(1),
        name="hg_proj",
    )(xp, xs, gains.reshape(gains.shape[0], gains.shape[1], 1, d), w_in, lb_logits)


def _hg_tables(chunk, block):
    n_levels = int(math.log2(block))
    assert 2 ** n_levels == block and chunk % block == 0 and chunk == LANE
    t = np.arange(chunk)[:, None]
    s = np.arange(chunk)[None, :]
    same_run = (t // block) == (s // block)
    sums = np.concatenate([same_run & (s <= t), same_run & (s > t)], axis=0)
    masks = [t == s]
    for l in range(n_levels):
        m = 2 ** l
        mid = (t // (2 * m)) * (2 * m) + m
        masks.append(((t // (2 * m)) == (s // (2 * m))) & (t >= mid) & (s < mid))
    r = np.arange(chunk)
    rows = np.stack([r % 2 == 1, r % 4 == 0, r % 4 >= 2, r % 4 == 3,
                     np.where(r % 8 >= 4, 1.0, -1.0)]).astype(np.float32)
    rows = np.broadcast_to(rows[:, :, None], rows.shape + (LANE,))
    block_causal = ((t // HG_FAST_BLOCK) == (s // HG_FAST_BLOCK)) & (s <= t)
    rows = np.concatenate([rows, block_causal[None].astype(np.float32)])
    sums = jnp.asarray(sums.astype(np.float32), dtype=BF16)
    masks = jnp.asarray(np.stack(masks).astype(np.float32), dtype=BF16)
    return sums, masks, jnp.asarray(rows)


def _level_exponents(b, g2, b_ref, rows_ref, block):
    c = b.shape[0]
    out = []
    n_levels = int(math.log2(block))
    for l in range(n_levels):
        m = 2 ** l
        if m == 1:
            out.append(g2 * rows_ref[0])
        elif m == 2:
            nxt = pltpu.roll(g2, c - 1, 0)
            prv = pltpu.roll(g2, 1, 0)
            out.append(nxt * rows_ref[1] + g2 * rows_ref[2] + prv * rows_ref[3])
        elif 2 * m == SUBLANE:
            pieces = [b[lo:lo + SUBLANE] - b_ref[pl.ds(lo + m - 1, 1), :]
                      for lo in range(0, c, SUBLANE)]
            out.append(jnp.concatenate(pieces, axis=0) * rows_ref[4])
        else:
            pieces = []
            for lo in range(0, c, 2 * m):
                ref_row = b_ref[pl.ds(lo + m - 1, 1), :]
                pieces.append(ref_row - b[lo:lo + m])
                pieces.append(b[lo + m:lo + 2 * m] - ref_row)
            out.append(jnp.concatenate(pieces, axis=0))
    return out


def _hg_chunk(q, k, v, g, sums_ref, masks_ref, rows_ref, b_ref, block):
    c = q.shape[0]
    g2 = g * LOG2E
    g_hi = g2.astype(BF16)
    g_lo = (g2 - g_hi.astype(F32)).astype(BF16)
    g_split = jnp.concatenate([g_hi, g_lo], axis=1)
    both = _dot(sums_ref[:c, :] if block == c else sums_ref[...], g_split)
    yield
    both = both[:, :LANE] + both[:, LANE:]
    if block == c:
        b = both
        b_rest = b[c - 1:c, :] - b
    else:
        b, b_rest = both[:c], both[c:]
    b_ref[...] = b
    level_scores = [_dot_nt(q, k)]
    for nd in _level_exponents(b, g2, b_ref, rows_ref, block):
        e = jnp.exp2(nd).astype(BF16)
        level_scores.append(_dot_nt(q * e, k * e))
    eb = jnp.exp2(b)
    q_dec = q * eb.astype(BF16)
    k_dec = k * jnp.exp2(b_rest).astype(BF16)
    yield
    scores = level_scores[0].astype(BF16) * masks_ref[0]
    for l, s in enumerate(level_scores[1:]):
        scores = scores + s.astype(BF16) * masks_ref[1 + l]
    o_intra = _dot(scores, v)
    return o_intra, q_dec, k_dec, eb


def _hg_chunk_fast(q, k, v, g, sums_ref, rows_ref):
    c = q.shape[0]
    blk = HG_FAST_BLOCK
    g2 = g * LOG2E
    g_hi = g2.astype(BF16)
    g_lo = (g2 - g_hi.astype(F32)).astype(BF16)
    b = _dot(sums_ref[:c, :], jnp.concatenate([g_hi, g_lo], axis=1))
    yield
    b = b[:, :LANE] + b[:, LANE:]
    qf = q.astype(F32)
    kf = k.astype(F32)
    q_blocks, k_blocks, cross = [], [], [jnp.zeros((blk, c), F32)]
    for i in range(c // blk):
        rows = slice(i * blk, (i + 1) * blk)
        if i == 0:
            d = b[rows]
        else:
            r = b[i * blk - 1:i * blk, :]
            d = b[rows] - r
        q_i = (qf[rows] * jnp.exp2(d)).astype(BF16)
        q_blocks.append(q_i)
        k_blocks.append((kf[rows] * jnp.exp2(-d)).astype(BF16))
        if i > 0:
            k_before = (kf[:i * blk] * jnp.exp2(r - b[:i * blk])).astype(BF16)
            k_before = jnp.concatenate(
                [k_before, jnp.zeros((c - i * blk, LANE), BF16)], axis=0)
            cross.append(_dot_nt(q_i, k_before))
    within = _dot_nt(jnp.concatenate(q_blocks, axis=0), jnp.concatenate(k_blocks, axis=0))
    eb = jnp.exp2(b)
    q_dec = (qf * eb).astype(BF16)
    k_dec = (kf * jnp.exp2(b[c - 1:c, :] - b)).astype(BF16)
    yield
    scores = within * rows_ref[5] + jnp.concatenate(cross, axis=0)
    o_intra = _dot(scores.astype(BF16), v)
    return o_intra, q_dec, k_dec, eb


def _hg_prompt_step(q_ref, k_ref, v_ref, g_ref, sg_ref, gn_ref, sums_ref, masks_ref,
                    rows_ref, blocks_ref, gseq_ref, og_ref, s_ref, st_ref, b_ref, fast_ref,
                    *, it, last_it, cpi, chunk, hps, dh):
    @pl.when(it == 0)
    def _():
        st_ref[...] = jnp.zeros(st_ref.shape, F32)
        block_sums = _dot(blocks_ref[...], jnp.abs(gseq_ref[...]).astype(BF16))
        fast_ref[0] = (jnp.max(block_sums) * LOG2E < HG_FAST_LIMIT).astype(jnp.int32)

    def run(fast):
        rows = [pl.ds(cc * chunk, chunk) for cc in range(cpi)]
        heads = [slice(hh * dh, (hh + 1) * dh) for hh in range(hps)]
        pairs = [(hh, cc) for hh in range(hps) for cc in range(cpi)]
        vs = {(hh, cc): v_ref[rows[cc], heads[hh]] for hh, cc in pairs}
        chains = []
        for hh, cc in pairs:
            args = (q_ref[rows[cc], heads[hh]], k_ref[rows[cc], heads[hh]], vs[hh, cc],
                    g_ref[rows[cc], heads[hh]])
            if fast:
                chains.append(_hg_chunk_fast(*args, sums_ref, rows_ref))
            else:
                chains.append(_hg_chunk(*args, sums_ref, masks_ref, rows_ref,
                                        b_ref.at[hh * cpi + cc], chunk))
        intra = dict(zip(pairs, _run_staged(chains)))
        updates = {p: _dot_tn(vs[p], intra[p][2]) for p in pairs}
        states = [st_ref[hh] for hh in range(hps)]
        outs = {}
        for cc in range(cpi):
            inters = [_dot_nt(intra[hh, cc][1], states[hh].astype(BF16)) for hh in range(hps)]
            for hh in range(hps):
                o_intra, _, _, eb = intra[hh, cc]
                states[hh] = states[hh] * eb[chunk - 1:chunk, :] + updates[hh, cc]
                outs[hh, cc] = _head_norm_gate(o_intra + inters[hh], gn_ref[hh],
                                               sg_ref[rows[cc], heads[hh]])
        for hh in range(hps):
            st_ref[hh] = states[hh]
        for hh, cc in pairs:
            og_ref[rows[cc], heads[hh]] = outs[hh, cc]

    fast_ok = fast_ref[0] == 1
    pl.when(fast_ok)(lambda: run(True))
    pl.when(jnp.logical_not(fast_ok))(lambda: run(False))

    @pl.when(it == last_it)
    def _():
        for hh in range(hps):
            s_ref[0, hh] = st_ref[hh].T


def _hg_sample_step(q_ref, k_ref, v_ref, g_ref, sg_ref, gn_ref, sums_ref, masks_ref, rows_ref,
                    s0_ref, og_ref, s_ref, b_ref, *, seqs, dec_len, n_heads, dh):
    rows = seqs * dec_len
    row_seq = lax.broadcasted_iota(jnp.int32, (rows, 1), 0) // dec_len
    col_seq = lax.broadcasted_iota(jnp.int32, (1, rows), 1) // dec_len
    for hh in range(n_heads):
        cols = slice(hh * dh, (hh + 1) * dh)
        vb = v_ref[:, cols]
        (o, q_dec, k_dec, eb), = _run_staged([_hg_chunk(
            q_ref[:, cols], k_ref[:, cols], vb, g_ref[:, cols], sums_ref, masks_ref, rows_ref,
            b_ref, dec_len)])
        eb_t = eb.T
        k_dec_t = k_dec.astype(F32).T
        inter = jnp.zeros(o.shape, F32)
        for i in range(seqs):
            s_old = s0_ref[i, hh]
            inter = jnp.where(row_seq == i, _dot(q_dec, s_old.astype(BF16)), inter)
            k_i = jnp.where(col_seq == i, k_dec_t, 0.0).astype(BF16)
            last = (i + 1) * dec_len - 1
            s_ref[i, hh] = s_old * eb_t[:, last:last + 1] + _dot(k_i, vb)
        og_ref[:, cols] = _head_norm_gate(o + inter, gn_ref[hh], sg_ref[:, cols])


def _hg_core_kernel(*refs, n_prev, steps_per_unit, prompt_kw, sample_kw):
    prompt_in, sample_in = refs[:11], refs[11:21]
    og_p, s_p, og_s, s_s, st_ref, b_ref, bs_ref, fast_ref = refs[21 + n_prev:]
    n_h, n_it = pl.num_programs(1), pl.num_programs(2)
    step = (pl.program_id(0) * n_h + pl.program_id(1)) * n_it + pl.program_id(2)

    @pl.when(step % steps_per_unit == 0)
    def _():
        _hg_sample_step(*sample_in, og_s, s_s, bs_ref, **sample_kw)

    _hg_prompt_step(*prompt_in, og_p, s_p, st_ref, b_ref, fast_ref, it=pl.program_id(2),
                    last_it=n_it - 1, **prompt_kw)


def _hg_core(prompt, sample, gn, s0_all, j, prev_prompt, prev_sample, n_seqs, seq_len, dec_len):
    heads = HG_HEADS
    hps = HG_HEADS_PER_STEP
    n_layers, dec_seqs = s0_all.shape[:2]
    dh = s0_all.shape[3]
    d = heads * dh
    gn4 = gn.reshape(gn.shape[0], heads, 1, dh)

    chunk = math.gcd(seq_len, HG_CHUNK)
    n_chunks = seq_len // chunk
    cpi = math.gcd(n_chunks, HG_CHUNKS_PER_ITER)
    grid = (n_seqs, heads // hps, n_chunks // cpi)
    step_rows = cpi * chunk
    sums, masks, row_tabs = _hg_tables(chunk, chunk)
    block_of_row = np.arange(seq_len) // HG_FAST_BLOCK
    blocks = jnp.asarray(np.arange(seq_len // HG_FAST_BLOCK)[:, None] == block_of_row[None, :],
                         dtype=BF16)
    per_step = pl.BlockSpec((step_rows, hps * dh), lambda b, h, t: (b * grid[2] + t, h))
    per_seq = pl.BlockSpec((seq_len, hps * dh), lambda b, h, t: (b, h))
    sp_spec, sp_shape, prev_p, prev_p_spec = _state_out(
        prev_prompt, (n_layers, n_seqs, heads, dh, dh), j, (1, hps, dh, dh),
        lambda b, h, t: (b, h, 0, 0))
    prompt_specs = [per_step] * 5 + [
        pl.BlockSpec((None, hps, 1, dh), lambda b, h, t: (j, h, 0, 0)),
        _resident(sums.shape), _resident(masks.shape), _resident(row_tabs.shape),
        _resident(blocks.shape), per_seq]
    prompt_args = (*prompt, gn4, sums, masks, row_tabs, blocks, prompt[3])

    rows = math.gcd(dec_seqs * dec_len, HG_CHUNK)
    seqs = rows // dec_len
    n_steps = grid[0] * grid[1] * grid[2]
    seq_blocks = dec_seqs // seqs
    hu = max(1, seq_blocks * heads // n_steps)
    units_per_block = heads // hu
    spu = n_steps // (seq_blocks * units_per_block)
    assert seqs * dec_len == rows and seq_blocks * units_per_block * spu == n_steps
    sums_s, masks_s, row_tabs_s = _hg_tables(rows, dec_len)

    def unit(b, h, t):
        u = ((b * grid[1] + h) * grid[2] + t) // spu
        return u // units_per_block, u % units_per_block

    per_unit = pl.BlockSpec((rows, hu * dh), lambda b, h, t: unit(b, h, t))
    state_in = pl.BlockSpec((None, seqs, hu, dh, dh),
                            lambda b, h, t: (j,) + unit(b, h, t) + (0, 0))
    ss_spec, ss_shape, prev_s, prev_s_spec = _state_out(
        prev_sample, s0_all.shape, j, (seqs, hu, dh, dh),
        lambda b, h, t: unit(b, h, t) + (0, 0))
    sample_specs = [per_unit] * 5 + [
        pl.BlockSpec((None, hu, 1, dh), lambda b, h, t: (j, unit(b, h, t)[1], 0, 0)),
        _resident(sums_s.shape), _resident(masks_s.shape), _resident(row_tabs_s.shape),
        state_in]
    sample_args = (*sample, gn4, sums_s, masks_s, row_tabs_s, s0_all)

    aliases = {}
    if prev_p:
        aliases[21] = 1
    if prev_s:
        aliases[21 + len(prev_p)] = 3
    return pl.pallas_call(
        functools.partial(
            _hg_core_kernel, n_prev=len(prev_p) + len(prev_s), steps_per_unit=spu,
            prompt_kw=dict(cpi=cpi, chunk=chunk, hps=hps, dh=dh),
            sample_kw=dict(seqs=seqs, dec_len=dec_len, n_heads=hu, dh=dh)),
        out_shape=(jax.ShapeDtypeStruct((n_seqs * seq_len, d), BF16), sp_shape,
                   jax.ShapeDtypeStruct((dec_seqs * dec_len, d), BF16), ss_shape),
        grid=grid,
        in_specs=prompt_specs + sample_specs + prev_p_spec + prev_s_spec,
        out_specs=(per_step, sp_spec, per_unit, ss_spec),
        input_output_aliases=aliases,
        scratch_shapes=[pltpu.VMEM((hps, dh, dh), F32),
                        pltpu.VMEM((hps * cpi, chunk, dh), F32),
                        pltpu.VMEM((rows, dh), F32),
                        pltpu.SMEM((1,), jnp.int32)],
        compiler_params=_cparams(3),
        name="hg_core",
    )(*prompt_args, *sample_args, *prev_p, *prev_s)


def kernel(x_prompt, x_sample, state_ret, state_hgrn, norm_gain, ffn_w_up, ffn_w_down,
           ret_w_in, ret_norm, ret_w_out, hg_w_in, hg_lb_logits, hg_norm, hg_w_out, final_norm):
    n_seqs, seq_len, d = x_prompt.shape
    dec_seqs, dec_len, _ = x_sample.shape
    depth = norm_gain.shape[0]
    n_ret, n_hg = state_ret.shape[0], state_hgrn.shape[0]

    xp = x_prompt.reshape(n_seqs * seq_len, d)
    xs = x_sample.reshape(dec_seqs * dec_len, d)
    w_up, w_down = ffn_w_up.astype(BF16), ffn_w_down.astype(BF16)
    ret_in, ret_out = ret_w_in.astype(BF16), ret_w_out.astype(BF16)
    hg_in, hg_out = hg_w_in.astype(BF16), hg_w_out.astype(BF16)

    ret_p = ret_s = hg_p = hg_s = None
    for layer in range(depth):
        xp, xs = _ffn(xp, xs, norm_gain, w_up, w_down, layer, 0)
        j = layer // 2
        if layer % 2 == 0:
            qp, qs, kp, ks, vp, vs, gp, gs = _ret_proj(
                xp, xs, norm_gain, ret_in, layer, j, seq_len, dec_len)
            ap, ret_p, a_s, ret_s = _ret_core(
                (qp, kp, vp), (qs, ks, vs), ret_norm, state_ret, j, ret_p, ret_s,
                n_seqs, seq_len, dec_len)
            mix = (ap, a_s, ret_out, j, (gp, gs))
        else:
            qp, qs, kp, ks, gp, gs, vp, vs, sp, ss = _hg_proj(
                xp, xs, norm_gain, hg_in, hg_lb_logits, layer, j)
            ap, hg_p, a_s, hg_s = _hg_core(
                (qp, kp, vp, gp, sp), (qs, ks, vs, gs, ss), hg_norm, state_hgrn, j, hg_p, hg_s,
                n_seqs, seq_len, dec_len)
            mix = (ap, a_s, hg_out, j, None)
        xp, xs = _ffn(xp, xs, norm_gain, w_up, w_down, layer, 1, mix,
                      final_norm if layer == depth - 1 else None)
    return (xp.reshape(n_seqs, seq_len, d), xs.reshape(dec_seqs, dec_len, d),
            ret_p, ret_s, hg_p, hg_s)
```

```python
import functools
import math

import jax
import jax.numpy as jnp
import numpy as np
from jax import lax
from jax.experimental import pallas as pl
from jax.experimental.pallas import tpu as pltpu

F32 = jnp.float32
BF16 = jnp.bfloat16

EPS = 1e-6
ROPE_BASE = 10000.0
PAST_LEN = 16384
RET_HEADS = 4
RET_CHUNK = 256
RET_HEADS_PER_STEP = 2
RET_CHUNKS_PER_ITER = 2
HG_HEADS = 8
HG_CHUNK = 128
HG_HEADS_PER_STEP = 4
HG_CHUNKS_PER_ITER = 8
HG_FAST_BLOCK = 32
HG_FAST_LIMIT = 120.0
TOKEN_TILE = 1024
MIX_FFN_TILE = 512
FFN_COLS = 256
LANE = 128
SUBLANE = 8
VMEM_LIMIT = 56 * 1024 * 1024
LOG2E = 1.4426950408889634


def _cparams(n_axes):
    return pltpu.CompilerParams(
        dimension_semantics=("arbitrary",) * n_axes,
        vmem_limit_bytes=VMEM_LIMIT)


def _resident(shape, index=None):
    index = (0,) * len(shape) if index is None else index
    return pl.BlockSpec(shape, lambda *_: index, pipeline_mode=pl.Buffered(1))


def _dot(a, b):
    return jnp.dot(a, b, preferred_element_type=F32)


def _dot_nt(a, b):
    return lax.dot_general(a, b, (((1,), (1,)), ((), ())),
                           preferred_element_type=F32)


def _dot_tn(a, b):
    return lax.dot_general(a, b, (((0,), (0,)), ((), ())),
                           preferred_element_type=F32)


def _rmsnorm(x, gain):
    ms = jnp.mean(x * x, axis=-1, keepdims=True)
    return x * lax.rsqrt(ms + EPS) * gain


def _exp_neg(x):
    return jnp.exp2(x * (-LOG2E))


def _silu(x):
    return x * (1.0 / (1.0 + _exp_neg(x)))


def _run_staged(gens):
    results = [None] * len(gens)
    live = list(range(len(gens)))
    while live:
        still = []
        for i in live:
            try:
                next(gens[i])
                still.append(i)
            except StopIteration as stop:
                results[i] = stop.value
        live = still
    return results


def _token_specs(tm, ts, width):
    return (pl.BlockSpec((tm, width), lambda i: (i, 0)),
            pl.BlockSpec((ts, width), lambda i: (i, 0)))


def _token_tiles(xp, xs, tile=None):
    n_prompt, n_sample = xp.shape[0], xs.shape[0]
    tm = math.gcd(n_prompt, TOKEN_TILE if tile is None else tile)
    steps = n_prompt // tm
    ts = n_sample // steps
    assert ts * steps == n_sample and ts % 16 == 0
    return tm, ts, steps


def _load_normed(xp_ref, xs_ref, g_ref, xn_ref):
    tm = xp_ref.shape[0]
    xn_ref[:tm, :] = _rmsnorm(xp_ref[...], g_ref[...]).astype(BF16)
    xn_ref[tm:, :] = _rmsnorm(xs_ref[...], g_ref[...]).astype(BF16)


def _ffn_body(xp, xs, g_ref, wu_ref, wd_ref, op_ref, os_ref, xn_ref, h_ref, fg_ref=None):
    tm = xp.shape[0]
    d_ff = wd_ref.shape[0]
    xn_ref[:tm, :] = _rmsnorm(xp, g_ref[...]).astype(BF16)
    xn_ref[tm:, :] = _rmsnorm(xs, g_ref[...]).astype(BF16)
    xn = xn_ref[...]
    for c in range(d_ff // FFN_COLS):
        lo = c * FFN_COLS
        a = _dot(xn, wu_ref[:, lo:lo + FFN_COLS])
        b = _dot(xn, wu_ref[:, d_ff + lo:d_ff + lo + FFN_COLS])
        h_ref[:, lo:lo + FFN_COLS] = (_silu(a) * b).astype(BF16)
    y = _dot(h_ref[...], wd_ref[...])
    yp = xp + 0.5 * y[:tm]
    ys = xs + 0.5 * y[tm:]
    if fg_ref is not None:
        yp, ys = _rmsnorm(yp, fg_ref[...]), _rmsnorm(ys, fg_ref[...])
    op_ref[...] = yp
    os_ref[...] = ys


def _ffn_kernel(xp_ref, xs_ref, g_ref, wu_ref, wd_ref, op_ref, os_ref, xn_ref, h_ref):
    _ffn_body(xp_ref[...], xs_ref[...], g_ref, wu_ref, wd_ref, op_ref, os_ref, xn_ref, h_ref)


def _mix_ffn_kernel(xp_ref, xs_ref, ap_ref, as_ref, *rest, gated):
    ap, a_s = ap_ref[...], as_ref[...]
    if gated:
        ap, a_s = ap * rest[0][...], a_s * rest[1][...]
        rest = rest[2:]
    wo_ref, g_ref, wu_ref, wd_ref = rest[:4]
    fg_ref = rest[4] if len(rest) == 9 else None
    xp = xp_ref[...] + _dot(ap, wo_ref[...])
    xs = xs_ref[...] + _dot(a_s, wo_ref[...])
    _ffn_body(xp, xs, g_ref, wu_ref, wd_ref, *rest[-4:], fg_ref=fg_ref)


def _ffn(xp, xs, gains, w_up, w_down, layer, which, mix=None, final_gain=None):
    d = xp.shape[1]
    d_ff = w_down.shape[2]
    tm, ts, steps = _token_tiles(xp, xs, None if mix is None else MIX_FFN_TILE)
    assert d_ff % FFN_COLS == 0
    xspec = _token_specs(tm, ts, d)
    gain = _resident((None, None, 1, d), (layer, 2 * which, 0, 0))
    weights = [_resident((None, None, d, 2 * d_ff), (layer, which, 0, 0)),
               _resident((None, None, d_ff, d), (layer, which, 0, 0))]
    gains4 = gains.reshape(gains.shape[0], gains.shape[1], 1, d)
    if mix is None:
        body, in_specs, args = _ffn_kernel, [*xspec, gain, *weights], (xp, xs, gains4, w_up, w_down)
    else:
        ap, a_s, w_out, j, gates = mix
        k = ap.shape[1]
        body = functools.partial(_mix_ffn_kernel, gated=gates is not None)
        in_specs = [*xspec, *_token_specs(tm, ts, k)]
        args = (xp, xs, ap, a_s)
        if gates is not None:
            in_specs += _token_specs(tm, ts, k)
            args += tuple(gates)
        in_specs += [_resident((None, k, d), (j, 0, 0)), gain, *weights]
        args += (w_out, gains4, w_up, w_down)
        if final_gain is not None:
            in_specs.append(_resident((1, d)))
            args += (final_gain.reshape(1, d),)
    return pl.pallas_call(
        body,
        out_shape=(jax.ShapeDtypeStruct(xp.shape, F32), jax.ShapeDtypeStruct(xs.shape, F32)),
        grid=(steps,),
        in_specs=in_specs,
        out_specs=xspec,
        scratch_shapes=[pltpu.VMEM((tm + ts, d), BF16), pltpu.VMEM((tm + ts, d_ff), BF16)],
        compiler_params=_cparams(1),
        name="ffn" if mix is None else "mix_ffn",
    )(*args)


def _ret_proj_kernel(xp_ref, xs_ref, g_ref, w_ref, cp_ref, sp_ref, cs_ref, ss_ref,
                     qp_ref, qs_ref, kp_ref, ks_ref, vp_ref, vs_ref, gp_ref, gs_ref, xn_ref,
                     *, dk, dv, heads):
    tm = xp_ref.shape[0]
    _load_normed(xp_ref, xs_ref, g_ref, xn_ref)
    xn = xn_ref[...]
    cos = jnp.concatenate([cp_ref[...], cs_ref[...]], axis=0)
    sin = jnp.concatenate([sp_ref[...], ss_ref[...]], axis=0)
    half = dk // 2
    qk = heads * dk
    k_scale = dk ** -0.5

    def put(p_ref, s_ref, lo, val):
        val = val.astype(BF16)
        p_ref[:, lo:lo + val.shape[1]] = val[:tm]
        s_ref[:, lo:lo + val.shape[1]] = val[tm:]

    for h in range(heads):
        for base, p_ref, s_ref, scale in ((0, qp_ref, qs_ref, None), (qk, kp_ref, ks_ref, k_scale)):
            t = _dot(xn, w_ref[:, base + h * dk:base + (h + 1) * dk])
            x1, x2 = t[:, :half], t[:, half:]
            r1 = x1 * cos - x2 * sin
            r2 = x1 * sin + x2 * cos
            if scale is not None:
                r1, r2 = r1 * scale, r2 * scale
            put(p_ref, s_ref, h * dk, r1)
            put(p_ref, s_ref, h * dk + half, r2)
    for h in range(heads):
        lo = 2 * qk + h * dv
        put(vp_ref, vs_ref, h * dv, _dot(xn, w_ref[:, lo:lo + dv]))
        lo = 2 * qk + heads * dv + h * dv
        put(gp_ref, gs_ref, h * dv, _silu(_dot(xn, w_ref[:, lo:lo + dv])))


def _rope_tables(pos, half):
    inv_freq = ROPE_BASE ** (-jnp.arange(half, dtype=F32) / half)
    ang = pos[:, None] * inv_freq[None, :]
    return jnp.cos(ang), jnp.sin(ang)


def _ret_proj(xp, xs, gains, w_in, layer, j, seq_len, dec_len):
    d = xp.shape[1]
    heads = RET_HEADS
    dk = d // heads
    dv = 2 * dk
    half = dk // 2
    tm, ts, steps = _token_tiles(xp, xs)
    assert seq_len % tm == 0 and ts % dec_len == 0
    tiles_per_seq = seq_len // tm
    cos_p, sin_p = _rope_tables(jnp.arange(seq_len, dtype=F32), half)
    cos_s, sin_s = _rope_tables(
        jnp.tile(PAST_LEN + jnp.arange(dec_len, dtype=F32), ts // dec_len), half)
    tab_p = pl.BlockSpec((tm, half), lambda i: (i % tiles_per_seq, 0))
    widths = (heads * dk, heads * dk, heads * dv, heads * dv)
    out_shape, out_specs = [], []
    for w in widths:
        out_shape += [jax.ShapeDtypeStruct((xp.shape[0], w), BF16),
                      jax.ShapeDtypeStruct((xs.shape[0], w), BF16)]
        out_specs += list(_token_specs(tm, ts, w))
    return pl.pallas_call(
        functools.partial(_ret_proj_kernel, dk=dk, dv=dv, heads=heads),
        out_shape=tuple(out_shape),
        grid=(steps,),
        in_specs=[*_token_specs(tm, ts, d),
                  _resident((None, None, 1, d), (layer, 1, 0, 0)),
                  _resident((None,) + w_in.shape[1:], (j, 0, 0)),
                  tab_p, tab_p, _resident((ts, half)), _resident((ts, half))],
        out_specs=tuple(out_specs),
        scratch_shapes=[pltpu.VMEM((tm + ts, d), BF16)],
        compiler_params=_cparams(1),
        name="ret_proj",
    )(xp, xs, gains.reshape(gains.shape[0], gains.shape[1], 1, d), w_in,
      cos_p, sin_p, cos_s, sin_s)


def _head_norm(o, gn):
    ms = jnp.mean(o * o, axis=-1, keepdims=True)
    return o * lax.rsqrt(ms + EPS) * gn


def _head_norm_gate(o, gn, sg):
    return (_head_norm(o, gn) * sg.astype(F32)).astype(BF16)


def _ret_prompt_step(q_ref, k_ref, v_ref, gn_ref, dm_ref, qd_ref, kd_ref, cd_ref,
                     on_ref, s_ref, *, it, cpi, chunk, hps, dk, dv):
    h0 = pl.program_id(1) * hps

    @pl.when(it == 0)
    def _():
        for hh in range(hps):
            s_ref[0, hh] = jnp.zeros((dk, dv), F32)

    def head_chain(hh, rows):
        qs = [q_ref[r, hh * dk:(hh + 1) * dk] for r in rows]
        ks = [k_ref[r, hh * dk:(hh + 1) * dk] for r in rows]
        vs = [v_ref[r, hh * dv:(hh + 1) * dv] for r in rows]
        s = s_ref[0, hh]
        cd = cd_ref[h0 + hh]
        scores = [_dot_nt(qc, kc) for qc, kc in zip(qs, ks)]
        updates = [_dot_tn((kc.astype(F32) * kd_ref[hh]).astype(BF16), vc)
                   for kc, vc in zip(ks, vs)]
        yield
        intra = [_dot((sc * dm_ref[hh]).astype(BF16), vc) for sc, vc in zip(scores, vs)]
        inter = []
        for qc, upd in zip(qs, updates):
            inter.append(_dot(qc, s.astype(BF16)))
            s = s * cd + upd
        yield
        return s, [_head_norm(oi + oc * qd_ref[hh], gn_ref[hh]).astype(BF16)
                   for oi, oc in zip(intra, inter)]

    rows = [pl.ds(cc * chunk, chunk) for cc in range(cpi)]
    results = _run_staged([head_chain(hh, rows) for hh in range(hps)])
    for hh, (s_new, ons) in enumerate(results):
        s_ref[0, hh] = s_new
        for r, on in zip(rows, ons):
            on_ref[r, hh * dv:(hh + 1) * dv] = on


def _ret_decay_tables(chunk):
    heads = RET_HEADS
    log_gamma = jnp.log(1.0 - jnp.power(2.0, -5.0 - jnp.arange(heads, dtype=F32)))
    idx = jnp.arange(chunk, dtype=F32)
    diff = idx[:, None] - idx[None, :]
    lg = log_gamma[:, None, None]
    decay_mat = jnp.where(diff[None] >= 0, jnp.exp(diff[None] * lg), 0.0)
    q_decay = jnp.exp((idx[None, :] + 1.0) * log_gamma[:, None])
    k_decay = jnp.exp((chunk - 1.0 - idx[None, :]) * log_gamma[:, None])
    chunk_decay = jnp.exp(chunk * log_gamma)
    return decay_mat, q_decay, k_decay, chunk_decay


def _state_out(prev, shape, j, block, index_map):
    spec = pl.BlockSpec((None,) + block, lambda *ids: (j,) + index_map(*ids))
    extra_in = [] if prev is None else [prev]
    extra_spec = [] if prev is None else [pl.BlockSpec(memory_space=pl.ANY)]
    return spec, jax.ShapeDtypeStruct(shape, F32), extra_in, extra_spec


def _ret_sample_step(q_ref, k_ref, v_ref, gn_ref, dm_ref, qd_ref, kd_ref, cd_ref,
                     s0_ref, on_ref, s_ref, *, h, seqs, dec_len):
    rows = seqs * dec_len
    qb = q_ref[...]
    vb = v_ref[...]
    kdec = k_ref[...].astype(F32) * kd_ref[0]
    cd = cd_ref[h]
    row_seq = lax.broadcasted_iota(jnp.int32, (rows, 1), 0) // dec_len
    scores = _dot_nt(qb, k_ref[...]) * dm_ref[0]
    o = _dot(scores.astype(BF16), vb)
    inter = jnp.zeros(o.shape, F32)
    for i in range(seqs):
        s_old = s0_ref[i, 0]
        mine = row_seq == i
        inter = jnp.where(mine, _dot(qb, s_old.astype(BF16)), inter)
        k_i = jnp.where(mine, kdec, 0.0).astype(BF16)
        s_ref[i, 0] = s_old * cd + _dot_tn(k_i, vb)
    o = o + inter * qd_ref[0]
    on_ref[...] = _head_norm(o, gn_ref[0]).astype(BF16)


def _ret_core_kernel(*refs, n_prev, heads, cpi, chunk, hps, dk, dv, seqs, dec_len):
    prompt_in, sample_in = refs[:8], refs[8:17]
    og_p, s_p, og_s, s_s = refs[17 + n_prev:]
    n_h, n_it = pl.num_programs(1), pl.num_programs(2)
    unit = (pl.program_id(0) * n_h + pl.program_id(1)) * n_it + pl.program_id(2)
    _ret_sample_step(*sample_in, og_s, s_s, h=unit % heads, seqs=seqs, dec_len=dec_len)
    _ret_prompt_step(*prompt_in, og_p, s_p, it=pl.program_id(2), cpi=cpi, chunk=chunk,
                     hps=hps, dk=dk, dv=dv)


def _ret_core(prompt, sample, gn, s0_all, j, prev_prompt, prev_sample, n_seqs, seq_len, dec_len):
    qp, kp, vp = prompt
    qs, ks, vs = sample
    heads = RET_HEADS
    hps = RET_HEADS_PER_STEP
    n_layers, dec_seqs = s0_all.shape[:2]
    dk, dv = s0_all.shape[3], s0_all.shape[4]
    gn4 = gn.reshape(gn.shape[0], heads, 1, dv)

    chunk = math.gcd(seq_len, RET_CHUNK)
    n_chunks = seq_len // chunk
    cpi = math.gcd(n_chunks, RET_CHUNKS_PER_ITER)
    grid = (n_seqs, heads // hps, n_chunks // cpi)
    dm, qd, kd, cd = _ret_decay_tables(chunk)
    per_step = lambda w: pl.BlockSpec((cpi * chunk, hps * w),
                                      lambda b, h, t: (b * grid[2] + t, h))
    tab = lambda shape: pl.BlockSpec((hps,) + shape, lambda b, h, t: (h, 0, 0))
    sp_spec, sp_shape, prev_p, prev_p_spec = _state_out(
        prev_prompt, (n_layers, n_seqs, heads, dk, dv), j, (1, hps, dk, dv),
        lambda b, h, t: (b, h, 0, 0))
    prompt_specs = [per_step(dk), per_step(dk), per_step(dv),
                    pl.BlockSpec((None, hps, 1, dv), lambda b, h, t: (j, h, 0, 0)),
                    tab((chunk, chunk)), tab((chunk, 1)), tab((chunk, 1)),
                    pl.BlockSpec(memory_space=pltpu.SMEM)]
    prompt_args = (qp, kp, vp, gn4, dm, qd[:, :, None], kd[:, :, None], cd)

    n_units = grid[0] * grid[1] * grid[2]
    seqs = dec_seqs * heads // n_units
    rows = seqs * dec_len
    assert seqs * n_units == dec_seqs * heads and rows % 16 == 0
    assert math.gcd(dec_len, RET_CHUNK) == dec_len
    dm_s, qd_s, kd_s, cd_s = _ret_decay_tables(dec_len)
    seq_id = np.arange(rows) // dec_len
    same = jnp.asarray(seq_id[:, None] == seq_id[None, :])
    dm_blk = jnp.where(same[None], jnp.tile(dm_s, (1, seqs, seqs)), 0.0)
    qd_blk = jnp.tile(qd_s, (1, seqs))[:, :, None]
    kd_blk = jnp.tile(kd_s, (1, seqs))[:, :, None]

    def unit(b, h, t):
        u = (b * grid[1] + h) * grid[2] + t
        return u // heads, u % heads

    per_unit = lambda w: pl.BlockSpec((rows, w), lambda b, h, t: unit(b, h, t))
    tab_s = lambda shape: pl.BlockSpec((1,) + shape, lambda b, h, t: (unit(b, h, t)[1], 0, 0))
    state_in = pl.BlockSpec((None, seqs, 1, dk, dv),
                            lambda b, h, t: (j,) + unit(b, h, t) + (0, 0))
    ss_spec, ss_shape, prev_s, prev_s_spec = _state_out(
        prev_sample, s0_all.shape, j, (seqs, 1, dk, dv),
        lambda b, h, t: unit(b, h, t) + (0, 0))
    sample_specs = [per_unit(dk), per_unit(dk), per_unit(dv),
                    pl.BlockSpec((None, 1, 1, dv),
                                 lambda b, h, t: (j, unit(b, h, t)[1], 0, 0)),
                    tab_s((rows, rows)), tab_s((rows, 1)), tab_s((rows, 1)),
                    pl.BlockSpec(memory_space=pltpu.SMEM), state_in]
    sample_args = (qs, ks, vs, gn4, dm_blk, qd_blk, kd_blk, cd_s, s0_all)

    n_prev = len(prev_p) + len(prev_s)
    aliases = {}
    if prev_p:
        aliases[17] = 1
    if prev_s:
        aliases[17 + len(prev_p)] = 3
    return pl.pallas_call(
        functools.partial(_ret_core_kernel, n_prev=n_prev, heads=heads, cpi=cpi, chunk=chunk,
                          hps=hps, dk=dk, dv=dv, seqs=seqs, dec_len=dec_len),
        out_shape=(jax.ShapeDtypeStruct((qp.shape[0], heads * dv), BF16), sp_shape,
                   jax.ShapeDtypeStruct((qs.shape[0], heads * dv), BF16), ss_shape),
        grid=grid,
        in_specs=prompt_specs + sample_specs + prev_p_spec + prev_s_spec,
        out_specs=(per_step(dv), sp_spec, per_unit(dv), ss_spec),
        input_output_aliases=aliases,
        compiler_params=_cparams(3),
        name="ret_core",
    )(*prompt_args, *sample_args, *prev_p, *prev_s)


def _hg_proj_kernel(xp_ref, xs_ref, g_ref, w_ref, lbl_ref,
                    qp_ref, qs_ref, kp_ref, ks_ref, gp_ref, gs_ref, vp_ref, vs_ref,
                    sp_ref, ss_ref, xn_ref, *, layer, cols):
    tm, d = xp_ref.shape
    _load_normed(xp_ref, xs_ref, g_ref, xn_ref)
    xn = xn_ref[...]
    logits = lbl_ref[...]
    e = jnp.exp(logits - jnp.max(logits, axis=0, keepdims=True))
    sm = e / jnp.sum(e, axis=0, keepdims=True)
    cum = sm[0:1]
    for i in range(1, layer + 1):
        cum = cum + sm[i:i + 1]
    lb = cum - sm[0:1]
    log_lb = jnp.log(lb)
    log_1m_lb = jnp.log1p(-lb)

    def put(p_ref, s_ref, lo, val):
        p_ref[:, lo:lo + cols] = val[:tm].astype(p_ref.dtype)
        s_ref[:, lo:lo + cols] = val[tm:].astype(s_ref.dtype)

    for c in range(d // cols):
        lo = c * cols
        put(qp_ref, qs_ref, lo, _silu(_dot(xn, w_ref[:, lo:lo + cols])))
        z = _dot(xn, w_ref[:, d + lo:d + lo + cols])
        ez = _exp_neg(jnp.abs(z))
        ez1 = 1.0 + ez
        log_sig = jnp.minimum(z, 0.0) - jnp.log(ez1)
        a = log_lb[:, lo:lo + cols]
        b = log_1m_lb[:, lo:lo + cols] + log_sig
        delta = a - b
        g = jnp.where(delta != delta, a + b,
                      jnp.maximum(a, b) + jnp.log(1.0 + _exp_neg(jnp.abs(delta))))
        put(gp_ref, gs_ref, lo, g)
        sig_neg = jnp.where(z >= 0.0, ez, 1.0) * (1.0 / ez1)
        put(kp_ref, ks_ref, lo, (1.0 - lb[:, lo:lo + cols]) * sig_neg)
        put(vp_ref, vs_ref, lo, _dot(xn, w_ref[:, 2 * d + lo:2 * d + lo + cols]))
        put(sp_ref, ss_ref, lo, _silu(_dot(xn, w_ref[:, 3 * d + lo:3 * d + lo + cols])))


def _hg_proj(xp, xs, gains, w_in, lb_logits, layer, j):
    d = xp.shape[1]
    tm, ts, steps = _token_tiles(xp, xs)
    out_shape, out_specs = [], []
    for dt in (BF16, BF16, F32, BF16, BF16):
        out_shape += [jax.ShapeDtypeStruct(xp.shape, dt), jax.ShapeDtypeStruct(xs.shape, dt)]
        out_specs += list(_token_specs(tm, ts, d))
    return pl.pallas_call(
        functools.partial(_hg_proj_kernel, layer=j, cols=2 * LANE),
        out_shape=tuple(out_shape),
        grid=(steps,),
        in_specs=[*_token_specs(tm, ts, d),
                  _resident((None, None, 1, d), (layer, 1, 0, 0)),
                  _resident((None,) + w_in.shape[1:], (j, 0, 0)),
                  _resident(lb_logits.shape)],
        out_specs=tuple(out_specs),
        scratch_shapes=[pltpu.VMEM((tm + ts, d), BF16)],
        compiler_params=_cparams(1),
        name="hg_proj",
    )(xp, xs, gains.reshape(gains.shape[0], gains.shape[1], 1, d), w_in, lb_logits)


def _hg_tables(chunk, block):
    n_levels = int(math.log2(block))
    assert 2 ** n_levels == block and chunk % block == 0 and chunk == LANE
    t = np.arange(chunk)[:, None]
    s = np.arange(chunk)[None, :]
    same_run = (t // block) == (s // block)
    sums = np.concatenate([same_run & (s <= t), same_run & (s > t)], axis=0)
    masks = [t == s]
    for l in range(n_levels):
        m = 2 ** l
        mid = (t // (2 * m)) * (2 * m) + m
        masks.append(((t // (2 * m)) == (s // (2 * m))) & (t >= mid) & (s < mid))
    r = np.arange(chunk)
    rows = np.stack([r % 2 == 1, r % 4 == 0, r % 4 >= 2, r % 4 == 3,
                     np.where(r % 8 >= 4, 1.0, -1.0)]).astype(np.float32)
    rows = np.broadcast_to(rows[:, :, None], rows.shape + (LANE,))
    block_causal = ((t // HG_FAST_BLOCK) == (s // HG_FAST_BLOCK)) & (s <= t)
    rows = np.concatenate([rows, block_causal[None].astype(np.float32)])
    sums = jnp.asarray(sums.astype(np.float32), dtype=BF16)
    masks = jnp.asarray(np.stack(masks).astype(np.float32), dtype=BF16)
    return sums, masks, jnp.asarray(rows)


def _level_exponents(b, g2, b_ref, rows_ref, block):
    c = b.shape[0]
    out = []
    n_levels = int(math.log2(block))
    for l in range(n_levels):
        m = 2 ** l
        if m == 1:
            out.append(g2 * rows_ref[0])
        elif m == 2:
            nxt = pltpu.roll(g2, c - 1, 0)
            prv = pltpu.roll(g2, 1, 0)
            out.append(nxt * rows_ref[1] + g2 * rows_ref[2] + prv * rows_ref[3])
        elif 2 * m == SUBLANE:
            pieces = [b[lo:lo + SUBLANE] - b_ref[pl.ds(lo + m - 1, 1), :]
                      for lo in range(0, c, SUBLANE)]
            out.append(jnp.concatenate(pieces, axis=0) * rows_ref[4])
        else:
            pieces = []
            for lo in range(0, c, 2 * m):
                ref_row = b_ref[pl.ds(lo + m - 1, 1), :]
                pieces.append(ref_row - b[lo:lo + m])
                pieces.append(b[lo + m:lo + 2 * m] - ref_row)
            out.append(jnp.concatenate(pieces, axis=0))
    return out


def _hg_chunk(q, k, v, g, sums_ref, masks_ref, rows_ref, b_ref, block):
    c = q.shape[0]
    g2 = g * LOG2E
    g_hi = g2.astype(BF16)
    g_lo = (g2 - g_hi.astype(F32)).astype(BF16)
    g_split = jnp.concatenate([g_hi, g_lo], axis=1)
    both = _dot(sums_ref[:c, :] if block == c else sums_ref[...], g_split)
    yield
    both = both[:, :LANE] + both[:, LANE:]
    if block == c:
        b = both
        b_rest = b[c - 1:c, :] - b
    else:
        b, b_rest = both[:c], both[c:]
    b_ref[...] = b
    level_scores = [_dot_nt(q, k)]
    for nd in _level_exponents(b, g2, b_ref, rows_ref, block):
        e = jnp.exp2(nd).astype(BF16)
        level_scores.append(_dot_nt(q * e, k * e))
    eb = jnp.exp2(b)
    q_dec = q * eb.astype(BF16)
    k_dec = k * jnp.exp2(b_rest).astype(BF16)
    yield
    scores = level_scores[0].astype(BF16) * masks_ref[0]
    for l, s in enumerate(level_scores[1:]):
        scores = scores + s.astype(BF16) * masks_ref[1 + l]
    o_intra = _dot(scores, v)
    return o_intra, q_dec, k_dec, eb


def _hg_chunk_fast(q, k, v, g, sums_ref, rows_ref):
    c = q.shape[0]
    blk = HG_FAST_BLOCK
    g2 = g * LOG2E
    g_hi = g2.astype(BF16)
    g_lo = (g2 - g_hi.astype(F32)).astype(BF16)
    b = _dot(sums_ref[:c, :], jnp.concatenate([g_hi, g_lo], axis=1))
    yield
    b = b[:, :LANE] + b[:, LANE:]
    qf = q.astype(F32)
    kf = k.astype(F32)
    q_blocks, k_blocks, cross = [], [], [jnp.zeros((blk, c), F32)]
    for i in range(c // blk):
        rows = slice(i * blk, (i + 1) * blk)
        if i == 0:
            d = b[rows]
        else:
            r = b[i * blk - 1:i * blk, :]
            d = b[rows] - r
        q_i = (qf[rows] * jnp.exp2(d)).astype(BF16)
        q_blocks.append(q_i)
        k_blocks.append((kf[rows] * jnp.exp2(-d)).astype(BF16))
        if i > 0:
            k_before = (kf[:i * blk] * jnp.exp2(r - b[:i * blk])).astype(BF16)
            k_before = jnp.concatenate(
                [k_before, jnp.zeros((c - i * blk, LANE), BF16)], axis=0)
            cross.append(_dot_nt(q_i, k_before))
    within = _dot_nt(jnp.concatenate(q_blocks, axis=0), jnp.concatenate(k_blocks, axis=0))
    eb = jnp.exp2(b)
    q_dec = (qf * eb).astype(BF16)
    k_dec = (kf * jnp.exp2(b[c - 1:c, :] - b)).astype(BF16)
    yield
    scores = within * rows_ref[5] + jnp.concatenate(cross, axis=0)
    o_intra = _dot(scores.astype(BF16), v)
    return o_intra, q_dec, k_dec, eb


def _hg_prompt_step(q_ref, k_ref, v_ref, g_ref, sg_ref, gn_ref, sums_ref, masks_ref,
                    rows_ref, blocks_ref, og_ref, s_ref, st_ref, b_ref,
                    *, it, last_it, cpi, chunk, hps, dh):
    @pl.when(it == 0)
    def _():
        st_ref[...] = jnp.zeros(st_ref.shape, F32)

    def run(fast):
        rows = [pl.ds(cc * chunk, chunk) for cc in range(cpi)]
        heads = [slice(hh * dh, (hh + 1) * dh) for hh in range(hps)]
        pairs = [(hh, cc) for hh in range(hps) for cc in range(cpi)]
        vs = {(hh, cc): v_ref[rows[cc], heads[hh]] for hh, cc in pairs}
        chains = []
        for hh, cc in pairs:
            args = (q_ref[rows[cc], heads[hh]], k_ref[rows[cc], heads[hh]], vs[hh, cc],
                    g_ref[rows[cc], heads[hh]])
            if fast:
                chains.append(_hg_chunk_fast(*args, sums_ref, rows_ref))
            else:
                chains.append(_hg_chunk(*args, sums_ref, masks_ref, rows_ref,
                                        b_ref.at[hh * cpi + cc], chunk))
        intra = dict(zip(pairs, _run_staged(chains)))
        updates = {p: _dot_tn(vs[p], intra[p][2]) for p in pairs}
        states = [st_ref[hh] for hh in range(hps)]
        outs = {}
        for cc in range(cpi):
            inters = [_dot_nt(intra[hh, cc][1], states[hh].astype(BF16)) for hh in range(hps)]
            for hh in range(hps):
                o_intra, _, _, eb = intra[hh, cc]
                states[hh] = states[hh] * eb[chunk - 1:chunk, :] + updates[hh, cc]
                outs[hh, cc] = _head_norm_gate(o_intra + inters[hh], gn_ref[hh],
                                               sg_ref[rows[cc], heads[hh]])
        for hh in range(hps):
            st_ref[hh] = states[hh]
        for hh, cc in pairs:
            og_ref[rows[cc], heads[hh]] = outs[hh, cc]

    block_sums = _dot(blocks_ref[...], jnp.abs(g_ref[...]).astype(BF16))
    fast_ok = jnp.max(block_sums) * LOG2E < HG_FAST_LIMIT
    pl.when(fast_ok)(lambda: run(True))
    pl.when(jnp.logical_not(fast_ok))(lambda: run(False))

    @pl.when(it == last_it)
    def _():
        for hh in range(hps):
            s_ref[0, hh] = st_ref[hh].T


def _hg_sample_step(q_ref, k_ref, v_ref, g_ref, sg_ref, gn_ref, sums_ref, masks_ref, rows_ref,
                    s0_ref, og_ref, s_ref, b_ref, *, seqs, dec_len, n_heads, dh):
    rows = seqs * dec_len
    row_seq = lax.broadcasted_iota(jnp.int32, (rows, 1), 0) // dec_len
    col_seq = lax.broadcasted_iota(jnp.int32, (1, rows), 1) // dec_len
    for hh in range(n_heads):
        cols = slice(hh * dh, (hh + 1) * dh)
        vb = v_ref[:, cols]
        (o, q_dec, k_dec, eb), = _run_staged([_hg_chunk(
            q_ref[:, cols], k_ref[:, cols], vb, g_ref[:, cols], sums_ref, masks_ref, rows_ref,
            b_ref, dec_len)])
        eb_t = eb.T
        k_dec_t = k_dec.astype(F32).T
        inter = jnp.zeros(o.shape, F32)
        for i in range(seqs):
            s_old = s0_ref[i, hh]
            inter = jnp.where(row_seq == i, _dot(q_dec, s_old.astype(BF16)), inter)
            k_i = jnp.where(col_seq == i, k_dec_t, 0.0).astype(BF16)
            last = (i + 1) * dec_len - 1
            s_ref[i, hh] = s_old * eb_t[:, last:last + 1] + _dot(k_i, vb)
        og_ref[:, cols] = _head_norm_gate(o + inter, gn_ref[hh], sg_ref[:, cols])


def _hg_core_kernel(*refs, n_prev, steps_per_unit, prompt_kw, sample_kw):
    prompt_in, sample_in = refs[:10], refs[10:20]
    og_p, s_p, og_s, s_s, st_ref, b_ref, bs_ref = refs[20 + n_prev:]
    n_h, n_it = pl.num_programs(1), pl.num_programs(2)
    step = (pl.program_id(0) * n_h + pl.program_id(1)) * n_it + pl.program_id(2)

    @pl.when(step % steps_per_unit == 0)
    def _():
        _hg_sample_step(*sample_in, og_s, s_s, bs_ref, **sample_kw)

    _hg_prompt_step(*prompt_in, og_p, s_p, st_ref, b_ref, it=pl.program_id(2),
                    last_it=n_it - 1, **prompt_kw)


def _hg_core(prompt, sample, gn, s0_all, j, prev_prompt, prev_sample, n_seqs, seq_len, dec_len):
    heads = HG_HEADS
    hps = HG_HEADS_PER_STEP
    n_layers, dec_seqs = s0_all.shape[:2]
    dh = s0_all.shape[3]
    d = heads * dh
    gn4 = gn.reshape(gn.shape[0], heads, 1, dh)

    chunk = math.gcd(seq_len, HG_CHUNK)
    n_chunks = seq_len // chunk
    cpi = math.gcd(n_chunks, HG_CHUNKS_PER_ITER)
    grid = (n_seqs, heads // hps, n_chunks // cpi)
    step_rows = cpi * chunk
    sums, masks, row_tabs = _hg_tables(chunk, chunk)
    block_of_row = np.arange(step_rows) // HG_FAST_BLOCK
    blocks = jnp.asarray(np.arange(step_rows // HG_FAST_BLOCK)[:, None] == block_of_row[None, :],
                         dtype=BF16)
    per_step = pl.BlockSpec((step_rows, hps * dh), lambda b, h, t: (b * grid[2] + t, h))
    sp_spec, sp_shape, prev_p, prev_p_spec = _state_out(
        prev_prompt, (n_layers, n_seqs, heads, dh, dh), j, (1, hps, dh, dh),
        lambda b, h, t: (b, h, 0, 0))
    prompt_specs = [per_step] * 5 + [
        pl.BlockSpec((None, hps, 1, dh), lambda b, h, t: (j, h, 0, 0)),
        _resident(sums.shape), _resident(masks.shape), _resident(row_tabs.shape),
        _resident(blocks.shape)]
    prompt_args = (*prompt, gn4, sums, masks, row_tabs, blocks)

    rows = math.gcd(dec_seqs * dec_len, HG_CHUNK)
    seqs = rows // dec_len
    n_steps = grid[0] * grid[1] * grid[2]
    seq_blocks = dec_seqs // seqs
    hu = max(1, seq_blocks * heads // n_steps)
    units_per_block = heads // hu
    spu = n_steps // (seq_blocks * units_per_block)
    assert seqs * dec_len == rows and seq_blocks * units_per_block * spu == n_steps
    sums_s, masks_s, row_tabs_s = _hg_tables(rows, dec_len)

    def unit(b, h, t):
        u = ((b * grid[1] + h) * grid[2] + t) // spu
        return u // units_per_block, u % units_per_block

    per_unit = pl.BlockSpec((rows, hu * dh), lambda b, h, t: unit(b, h, t))
    state_in = pl.BlockSpec((None, seqs, hu, dh, dh),
                            lambda b, h, t: (j,) + unit(b, h, t) + (0, 0))
    ss_spec, ss_shape, prev_s, prev_s_spec = _state_out(
        prev_sample, s0_all.shape, j, (seqs, hu, dh, dh),
        lambda b, h, t: unit(b, h, t) + (0, 0))
    sample_specs = [per_unit] * 5 + [
        pl.BlockSpec((None, hu, 1, dh), lambda b, h, t: (j, unit(b, h, t)[1], 0, 0)),
        _resident(sums_s.shape), _resident(masks_s.shape), _resident(row_tabs_s.shape),
        state_in]
    sample_args = (*sample, gn4, sums_s, masks_s, row_tabs_s, s0_all)

    aliases = {}
    if prev_p:
        aliases[20] = 1
    if prev_s:
        aliases[20 + len(prev_p)] = 3
    return pl.pallas_call(
        functools.partial(
            _hg_core_kernel, n_prev=len(prev_p) + len(prev_s), steps_per_unit=spu,
            prompt_kw=dict(cpi=cpi, chunk=chunk, hps=hps, dh=dh),
            sample_kw=dict(seqs=seqs, dec_len=dec_len, n_heads=hu, dh=dh)),
        out_shape=(jax.ShapeDtypeStruct((n_seqs * seq_len, d), BF16), sp_shape,
                   jax.ShapeDtypeStruct((dec_seqs * dec_len, d), BF16), ss_shape),
        grid=grid,
        in_specs=prompt_specs + sample_specs + prev_p_spec + prev_s_spec,
        out_specs=(per_step, sp_spec, per_unit, ss_spec),
        input_output_aliases=aliases,
        scratch_shapes=[pltpu.VMEM((hps, dh, dh), F32),
                        pltpu.VMEM((hps * cpi, chunk, dh), F32),
                        pltpu.VMEM((rows, dh), F32)],
        compiler_params=_cparams(3),
        name="hg_core",
    )(*prompt_args, *sample_args, *prev_p, *prev_s)


def kernel(x_prompt, x_sample, state_ret, state_hgrn, norm_gain, ffn_w_up, ffn_w_down,
           ret_w_in, ret_norm, ret_w_out, hg_w_in, hg_lb_logits, hg_norm, hg_w_out, final_norm):
    n_seqs, seq_len, d = x_prompt.shape
    dec_seqs, dec_len, _ = x_sample.shape
    depth = norm_gain.shape[0]
    n_ret, n_hg = state_ret.shape[0], state_hgrn.shape[0]

    xp = x_prompt.reshape(n_seqs * seq_len, d)
    xs = x_sample.reshape(dec_seqs * dec_len, d)
    w_up, w_down = ffn_w_up.astype(BF16), ffn_w_down.astype(BF16)
    ret_in, ret_out = ret_w_in.astype(BF16), ret_w_out.astype(BF16)
    hg_in, hg_out = hg_w_in.astype(BF16), hg_w_out.astype(BF16)

    ret_p = ret_s = hg_p = hg_s = None
    for layer in range(depth):
        xp, xs = _ffn(xp, xs, norm_gain, w_up, w_down, layer, 0)
        j = layer // 2
        if layer % 2 == 0:
            qp, qs, kp, ks, vp, vs, gp, gs = _ret_proj(
                xp, xs, norm_gain, ret_in, layer, j, seq_len, dec_len)
            ap, ret_p, a_s, ret_s = _ret_core(
                (qp, kp, vp), (qs, ks, vs), ret_norm, state_ret, j, ret_p, ret_s,
                n_seqs, seq_len, dec_len)
            mix = (ap, a_s, ret_out, j, (gp, gs))
        else:
            qp, qs, kp, ks, gp, gs, vp, vs, sp, ss = _hg_proj(
                xp, xs, norm_gain, hg_in, hg_lb_logits, layer, j)
            ap, hg_p, a_s, hg_s = _hg_core(
                (qp, kp, vp, gp, sp), (qs, ks, vs, gs, ss), hg_norm, state_hgrn, j, hg_p, hg_s,
                n_seqs, seq_len, dec_len)
            mix = (ap, a_s, hg_out, j, None)
        xp, xs = _ffn(xp, xs, norm_gain, w_up, w_down, layer, 1, mix,
                      final_norm if layer == depth - 1 else None)
    return (xp.reshape(n_seqs, seq_len, d), xs.reshape(dec_seqs, dec_len, d),
            ret_p, ret_s, hg_p, hg_s)
```

```python
import functools
import math

import jax
import jax.numpy as jnp
import numpy as np
from jax import lax
from jax.experimental import pallas as pl
from jax.experimental.pallas import tpu as pltpu

F32 = jnp.float32
BF16 = jnp.bfloat16

EPS = 1e-6
ROPE_BASE = 10000.0
PAST_LEN = 16384
RET_HEADS = 4
RET_CHUNK = 256
RET_HEADS_PER_STEP = 2
RET_CHUNKS_PER_ITER = 4
HG_HEADS = 8
HG_CHUNK = 128
HG_HEADS_PER_STEP = 4
HG_CHUNKS_PER_ITER = 8
HG_FAST_BLOCK = 32
HG_FAST_LIMIT = 120.0
TOKEN_TILE = 1024
MIX_FFN_TILE = 512
FFN_COLS = 256
LANE = 128
SUBLANE = 8
VMEM_LIMIT = 56 * 1024 * 1024
LOG2E = 1.4426950408889634


def _cparams(n_axes):
    return pltpu.CompilerParams(
        dimension_semantics=("arbitrary",) * n_axes,
        vmem_limit_bytes=VMEM_LIMIT)


def _resident(shape, index=None):
    index = (0,) * len(shape) if index is None else index
    return pl.BlockSpec(shape, lambda *_: index, pipeline_mode=pl.Buffered(1))


def _dot(a, b):
    return jnp.dot(a, b, preferred_element_type=F32)


def _dot_nt(a, b):
    return lax.dot_general(a, b, (((1,), (1,)), ((), ())),
                           preferred_element_type=F32)


def _dot_tn(a, b):
    return lax.dot_general(a, b, (((0,), (0,)), ((), ())),
                           preferred_element_type=F32)


def _rmsnorm(x, gain):
    ms = jnp.mean(x * x, axis=-1, keepdims=True)
    return x * lax.rsqrt(ms + EPS) * gain


def _exp_neg(x):
    return jnp.exp2(x * (-LOG2E))


def _silu(x):
    return x * (1.0 / (1.0 + _exp_neg(x)))


def _run_staged(gens):
    results = [None] * len(gens)
    live = list(range(len(gens)))
    while live:
        still = []
        for i in live:
            try:
                next(gens[i])
                still.append(i)
            except StopIteration as stop:
                results[i] = stop.value
        live = still
    return results


def _token_specs(tm, ts, width):
    return (pl.BlockSpec((tm, width), lambda i: (i, 0)),
            pl.BlockSpec((ts, width), lambda i: (i, 0)))


def _token_tiles(xp, xs, tile=None):
    n_prompt, n_sample = xp.shape[0], xs.shape[0]
    tm = math.gcd(n_prompt, TOKEN_TILE if tile is None else tile)
    steps = n_prompt // tm
    ts = n_sample // steps
    assert ts * steps == n_sample and ts % 16 == 0
    return tm, ts, steps


def _load_normed(xp_ref, xs_ref, g_ref, xn_ref):
    tm = xp_ref.shape[0]
    xn_ref[:tm, :] = _rmsnorm(xp_ref[...], g_ref[...]).astype(BF16)
    xn_ref[tm:, :] = _rmsnorm(xs_ref[...], g_ref[...]).astype(BF16)


def _ffn_body(xp, xs, g_ref, wu_ref, wd_ref, op_ref, os_ref, xn_ref, h_ref, fg_ref=None):
    tm = xp.shape[0]
    d_ff = wd_ref.shape[0]
    xn_ref[:tm, :] = _rmsnorm(xp, g_ref[...]).astype(BF16)
    xn_ref[tm:, :] = _rmsnorm(xs, g_ref[...]).astype(BF16)
    xn = xn_ref[...]
    for c in range(d_ff // FFN_COLS):
        lo = c * FFN_COLS
        a = _dot(xn, wu_ref[:, lo:lo + FFN_COLS])
        b = _dot(xn, wu_ref[:, d_ff + lo:d_ff + lo + FFN_COLS])
        h_ref[:, lo:lo + FFN_COLS] = (_silu(a) * b).astype(BF16)
    y = _dot(h_ref[...], wd_ref[...])
    yp = xp + 0.5 * y[:tm]
    ys = xs + 0.5 * y[tm:]
    if fg_ref is not None:
        yp, ys = _rmsnorm(yp, fg_ref[...]), _rmsnorm(ys, fg_ref[...])
    op_ref[...] = yp
    os_ref[...] = ys


def _ffn_kernel(xp_ref, xs_ref, g_ref, wu_ref, wd_ref, op_ref, os_ref, xn_ref, h_ref):
    _ffn_body(xp_ref[...], xs_ref[...], g_ref, wu_ref, wd_ref, op_ref, os_ref, xn_ref, h_ref)


def _mix_ffn_kernel(xp_ref, xs_ref, ap_ref, as_ref, *rest, gated):
    ap, a_s = ap_ref[...], as_ref[...]
    if gated:
        ap, a_s = ap * rest[0][...], a_s * rest[1][...]
        rest = rest[2:]
    wo_ref, g_ref, wu_ref, wd_ref = rest[:4]
    fg_ref = rest[4] if len(rest) == 9 else None
    xp = xp_ref[...] + _dot(ap, wo_ref[...])
    xs = xs_ref[...] + _dot(a_s, wo_ref[...])
    _ffn_body(xp, xs, g_ref, wu_ref, wd_ref, *rest[-4:], fg_ref=fg_ref)


def _ffn(xp, xs, gains, w_up, w_down, layer, which, mix=None, final_gain=None):
    d = xp.shape[1]
    d_ff = w_down.shape[2]
    tm, ts, steps = _token_tiles(xp, xs, None if mix is None else MIX_FFN_TILE)
    assert d_ff % FFN_COLS == 0
    xspec = _token_specs(tm, ts, d)
    gain = _resident((None, None, 1, d), (layer, 2 * which, 0, 0))
    weights = [_resident((None, None, d, 2 * d_ff), (layer, which, 0, 0)),
               _resident((None, None, d_ff, d), (layer, which, 0, 0))]
    gains4 = gains.reshape(gains.shape[0], gains.shape[1], 1, d)
    if mix is None:
        body, in_specs, args = _ffn_kernel, [*xspec, gain, *weights], (xp, xs, gains4, w_up, w_down)
    else:
        ap, a_s, w_out, j, gates = mix
        k = ap.shape[1]
        body = functools.partial(_mix_ffn_kernel, gated=gates is not None)
        in_specs = [*xspec, *_token_specs(tm, ts, k)]
        args = (xp, xs, ap, a_s)
        if gates is not None:
            in_specs += _token_specs(tm, ts, k)
            args += tuple(gates)
        in_specs += [_resident((None, k, d), (j, 0, 0)), gain, *weights]
        args += (w_out, gains4, w_up, w_down)
        if final_gain is not None:
            in_specs.append(_resident((1, d)))
            args += (final_gain.reshape(1, d),)
    return pl.pallas_call(
        body,
        out_shape=(jax.ShapeDtypeStruct(xp.shape, F32), jax.ShapeDtypeStruct(xs.shape, F32)),
        grid=(steps,),
        in_specs=in_specs,
        out_specs=xspec,
        scratch_shapes=[pltpu.VMEM((tm + ts, d), BF16), pltpu.VMEM((tm + ts, d_ff), BF16)],
        compiler_params=_cparams(1),
        name="ffn" if mix is None else "mix_ffn",
    )(*args)


def _ret_proj_kernel(xp_ref, xs_ref, g_ref, w_ref, cp_ref, sp_ref, cs_ref, ss_ref,
                     qp_ref, qs_ref, kp_ref, ks_ref, vp_ref, vs_ref, gp_ref, gs_ref, xn_ref,
                     *, dk, dv, heads):
    tm = xp_ref.shape[0]
    _load_normed(xp_ref, xs_ref, g_ref, xn_ref)
    xn = xn_ref[...]
    cos = jnp.concatenate([cp_ref[...], cs_ref[...]], axis=0)
    sin = jnp.concatenate([sp_ref[...], ss_ref[...]], axis=0)
    half = dk // 2
    qk = heads * dk
    k_scale = dk ** -0.5

    def put(p_ref, s_ref, lo, val):
        val = val.astype(BF16)
        p_ref[:, lo:lo + val.shape[1]] = val[:tm]
        s_ref[:, lo:lo + val.shape[1]] = val[tm:]

    for h in range(heads):
        for base, p_ref, s_ref, scale in ((0, qp_ref, qs_ref, None), (qk, kp_ref, ks_ref, k_scale)):
            t = _dot(xn, w_ref[:, base + h * dk:base + (h + 1) * dk])
            x1, x2 = t[:, :half], t[:, half:]
            r1 = x1 * cos - x2 * sin
            r2 = x1 * sin + x2 * cos
            if scale is not None:
                r1, r2 = r1 * scale, r2 * scale
            put(p_ref, s_ref, h * dk, r1)
            put(p_ref, s_ref, h * dk + half, r2)
    for h in range(heads):
        lo = 2 * qk + h * dv
        put(vp_ref, vs_ref, h * dv, _dot(xn, w_ref[:, lo:lo + dv]))
        lo = 2 * qk + heads * dv + h * dv
        put(gp_ref, gs_ref, h * dv, _silu(_dot(xn, w_ref[:, lo:lo + dv])))


def _rope_tables(pos, half):
    inv_freq = ROPE_BASE ** (-jnp.arange(half, dtype=F32) / half)
    ang = pos[:, None] * inv_freq[None, :]
    return jnp.cos(ang), jnp.sin(ang)


def _ret_proj(xp, xs, gains, w_in, layer, j, seq_len, dec_len):
    d = xp.shape[1]
    heads = RET_HEADS
    dk = d // heads
    dv = 2 * dk
    half = dk // 2
    tm, ts, steps = _token_tiles(xp, xs)
    assert seq_len % tm == 0 and ts % dec_len == 0
    tiles_per_seq = seq_len // tm
    cos_p, sin_p = _rope_tables(jnp.arange(seq_len, dtype=F32), half)
    cos_s, sin_s = _rope_tables(
        jnp.tile(PAST_LEN + jnp.arange(dec_len, dtype=F32), ts // dec_len), half)
    tab_p = pl.BlockSpec((tm, half), lambda i: (i % tiles_per_seq, 0))
    widths = (heads * dk, heads * dk, heads * dv, heads * dv)
    out_shape, out_specs = [], []
    for w in widths:
        out_shape += [jax.ShapeDtypeStruct((xp.shape[0], w), BF16),
                      jax.ShapeDtypeStruct((xs.shape[0], w), BF16)]
        out_specs += list(_token_specs(tm, ts, w))
    return pl.pallas_call(
        functools.partial(_ret_proj_kernel, dk=dk, dv=dv, heads=heads),
        out_shape=tuple(out_shape),
        grid=(steps,),
        in_specs=[*_token_specs(tm, ts, d),
                  _resident((None, None, 1, d), (layer, 1, 0, 0)),
                  _resident((None,) + w_in.shape[1:], (j, 0, 0)),
                  tab_p, tab_p, _resident((ts, half)), _resident((ts, half))],
        out_specs=tuple(out_specs),
        scratch_shapes=[pltpu.VMEM((tm + ts, d), BF16)],
        compiler_params=_cparams(1),
        name="ret_proj",
    )(xp, xs, gains.reshape(gains.shape[0], gains.shape[1], 1, d), w_in,
      cos_p, sin_p, cos_s, sin_s)


def _head_norm(o, gn):
    ms = jnp.mean(o * o, axis=-1, keepdims=True)
    return o * lax.rsqrt(ms + EPS) * gn


def _head_norm_gate(o, gn, sg):
    return (_head_norm(o, gn) * sg.astype(F32)).astype(BF16)


def _ret_prompt_step(q_ref, k_ref, v_ref, gn_ref, dm_ref, qd_ref, kd_ref, cd_ref,
                     on_ref, s_ref, *, it, cpi, chunk, hps, dk, dv):
    h0 = pl.program_id(1) * hps

    @pl.when(it == 0)
    def _():
        for hh in range(hps):
            s_ref[0, hh] = jnp.zeros((dk, dv), F32)

    def head_chain(hh, rows):
        qs = [q_ref[r, hh * dk:(hh + 1) * dk] for r in rows]
        ks = [k_ref[r, hh * dk:(hh + 1) * dk] for r in rows]
        vs = [v_ref[r, hh * dv:(hh + 1) * dv] for r in rows]
        s = s_ref[0, hh]
        cd = cd_ref[h0 + hh]
        scores = [_dot_nt(qc, kc) for qc, kc in zip(qs, ks)]
        updates = [_dot_tn((kc.astype(F32) * kd_ref[hh]).astype(BF16), vc)
                   for kc, vc in zip(ks, vs)]
        yield
        intra = [_dot((sc * dm_ref[hh]).astype(BF16), vc) for sc, vc in zip(scores, vs)]
        inter = []
        for qc, upd in zip(qs, updates):
            inter.append(_dot(qc, s.astype(BF16)))
            s = s * cd + upd
        yield
        return s, [_head_norm(oi + oc * qd_ref[hh], gn_ref[hh]).astype(BF16)
                   for oi, oc in zip(intra, inter)]

    rows = [pl.ds(cc * chunk, chunk) for cc in range(cpi)]
    results = _run_staged([head_chain(hh, rows) for hh in range(hps)])
    for hh, (s_new, ons) in enumerate(results):
        s_ref[0, hh] = s_new
        for r, on in zip(rows, ons):
            on_ref[r, hh * dv:(hh + 1) * dv] = on


def _ret_decay_tables(chunk):
    heads = RET_HEADS
    log_gamma = jnp.log(1.0 - jnp.power(2.0, -5.0 - jnp.arange(heads, dtype=F32)))
    idx = jnp.arange(chunk, dtype=F32)
    diff = idx[:, None] - idx[None, :]
    lg = log_gamma[:, None, None]
    decay_mat = jnp.where(diff[None] >= 0, jnp.exp(diff[None] * lg), 0.0)
    q_decay = jnp.exp((idx[None, :] + 1.0) * log_gamma[:, None])
    k_decay = jnp.exp((chunk - 1.0 - idx[None, :]) * log_gamma[:, None])
    chunk_decay = jnp.exp(chunk * log_gamma)
    return decay_mat, q_decay, k_decay, chunk_decay


def _state_out(prev, shape, j, block, index_map):
    spec = pl.BlockSpec((None,) + block, lambda *ids: (j,) + index_map(*ids))
    extra_in = [] if prev is None else [prev]
    extra_spec = [] if prev is None else [pl.BlockSpec(memory_space=pl.ANY)]
    return spec, jax.ShapeDtypeStruct(shape, F32), extra_in, extra_spec


def _ret_sample_step(q_ref, k_ref, v_ref, gn_ref, dm_ref, qd_ref, kd_ref, cd_ref,
                     s0_ref, on_ref, s_ref, *, h, seqs, dec_len):
    rows = seqs * dec_len
    qb = q_ref[...]
    vb = v_ref[...]
    kdec = k_ref[...].astype(F32) * kd_ref[0]
    cd = cd_ref[h]
    row_seq = lax.broadcasted_iota(jnp.int32, (rows, 1), 0) // dec_len
    scores = _dot_nt(qb, k_ref[...]) * dm_ref[0]
    o = _dot(scores.astype(BF16), vb)
    inter = jnp.zeros(o.shape, F32)
    for i in range(seqs):
        s_old = s0_ref[i, 0]
        mine = row_seq == i
        inter = jnp.where(mine, _dot(qb, s_old.astype(BF16)), inter)
        k_i = jnp.where(mine, kdec, 0.0).astype(BF16)
        s_ref[i, 0] = s_old * cd + _dot_tn(k_i, vb)
    o = o + inter * qd_ref[0]
    on_ref[...] = _head_norm(o, gn_ref[0]).astype(BF16)


def _ret_core_kernel(*refs, n_prev, heads, cpi, chunk, hps, dk, dv, seqs, dec_len):
    prompt_in, sample_in = refs[:8], refs[8:17]
    og_p, s_p, og_s, s_s = refs[17 + n_prev:]
    n_h, n_it = pl.num_programs(1), pl.num_programs(2)
    unit = (pl.program_id(0) * n_h + pl.program_id(1)) * n_it + pl.program_id(2)
    _ret_sample_step(*sample_in, og_s, s_s, h=unit % heads, seqs=seqs, dec_len=dec_len)
    _ret_prompt_step(*prompt_in, og_p, s_p, it=pl.program_id(2), cpi=cpi, chunk=chunk,
                     hps=hps, dk=dk, dv=dv)


def _ret_core(prompt, sample, gn, s0_all, j, prev_prompt, prev_sample, n_seqs, seq_len, dec_len):
    qp, kp, vp = prompt
    qs, ks, vs = sample
    heads = RET_HEADS
    hps = RET_HEADS_PER_STEP
    n_layers, dec_seqs = s0_all.shape[:2]
    dk, dv = s0_all.shape[3], s0_all.shape[4]
    gn4 = gn.reshape(gn.shape[0], heads, 1, dv)

    chunk = math.gcd(seq_len, RET_CHUNK)
    n_chunks = seq_len // chunk
    cpi = math.gcd(n_chunks, RET_CHUNKS_PER_ITER)
    grid = (n_seqs, heads // hps, n_chunks // cpi)
    dm, qd, kd, cd = _ret_decay_tables(chunk)
    per_step = lambda w: pl.BlockSpec((cpi * chunk, hps * w),
                                      lambda b, h, t: (b * grid[2] + t, h))
    tab = lambda shape: pl.BlockSpec((hps,) + shape, lambda b, h, t: (h, 0, 0))
    sp_spec, sp_shape, prev_p, prev_p_spec = _state_out(
        prev_prompt, (n_layers, n_seqs, heads, dk, dv), j, (1, hps, dk, dv),
        lambda b, h, t: (b, h, 0, 0))
    prompt_specs = [per_step(dk), per_step(dk), per_step(dv),
                    pl.BlockSpec((None, hps, 1, dv), lambda b, h, t: (j, h, 0, 0)),
                    tab((chunk, chunk)), tab((chunk, 1)), tab((chunk, 1)),
                    pl.BlockSpec(memory_space=pltpu.SMEM)]
    prompt_args = (qp, kp, vp, gn4, dm, qd[:, :, None], kd[:, :, None], cd)

    n_units = grid[0] * grid[1] * grid[2]
    seqs = dec_seqs * heads // n_units
    rows = seqs * dec_len
    assert seqs * n_units == dec_seqs * heads and rows % 16 == 0
    assert math.gcd(dec_len, RET_CHUNK) == dec_len
    dm_s, qd_s, kd_s, cd_s = _ret_decay_tables(dec_len)
    seq_id = np.arange(rows) // dec_len
    same = jnp.asarray(seq_id[:, None] == seq_id[None, :])
    dm_blk = jnp.where(same[None], jnp.tile(dm_s, (1, seqs, seqs)), 0.0)
    qd_blk = jnp.tile(qd_s, (1, seqs))[:, :, None]
    kd_blk = jnp.tile(kd_s, (1, seqs))[:, :, None]

    def unit(b, h, t):
        u = (b * grid[1] + h) * grid[2] + t
        return u // heads, u % heads

    per_unit = lambda w: pl.BlockSpec((rows, w), lambda b, h, t: unit(b, h, t))
    tab_s = lambda shape: pl.BlockSpec((1,) + shape, lambda b, h, t: (unit(b, h, t)[1], 0, 0))
    state_in = pl.BlockSpec((None, seqs, 1, dk, dv),
                            lambda b, h, t: (j,) + unit(b, h, t) + (0, 0))
    ss_spec, ss_shape, prev_s, prev_s_spec = _state_out(
        prev_sample, s0_all.shape, j, (seqs, 1, dk, dv),
        lambda b, h, t: unit(b, h, t) + (0, 0))
    sample_specs = [per_unit(dk), per_unit(dk), per_unit(dv),
                    pl.BlockSpec((None, 1, 1, dv),
                                 lambda b, h, t: (j, unit(b, h, t)[1], 0, 0)),
                    tab_s((rows, rows)), tab_s((rows, 1)), tab_s((rows, 1)),
                    pl.BlockSpec(memory_space=pltpu.SMEM), state_in]
    sample_args = (qs, ks, vs, gn4, dm_blk, qd_blk, kd_blk, cd_s, s0_all)

    n_prev = len(prev_p) + len(prev_s)
    aliases = {}
    if prev_p:
        aliases[17] = 1
    if prev_s:
        aliases[17 + len(prev_p)] = 3
    return pl.pallas_call(
        functools.partial(_ret_core_kernel, n_prev=n_prev, heads=heads, cpi=cpi, chunk=chunk,
                          hps=hps, dk=dk, dv=dv, seqs=seqs, dec_len=dec_len),
        out_shape=(jax.ShapeDtypeStruct((qp.shape[0], heads * dv), BF16), sp_shape,
                   jax.ShapeDtypeStruct((qs.shape[0], heads * dv), BF16), ss_shape),
        grid=grid,
        in_specs=prompt_specs + sample_specs + prev_p_spec + prev_s_spec,
        out_specs=(per_step(dv), sp_spec, per_unit(dv), ss_spec),
        input_output_aliases=aliases,
        compiler_params=_cparams(3),
        name="ret_core",
    )(*prompt_args, *sample_args, *prev_p, *prev_s)


def _hg_proj_kernel(xp_ref, xs_ref, g_ref, w_ref, lbl_ref,
                    qp_ref, qs_ref, kp_ref, ks_ref, gp_ref, gs_ref, vp_ref, vs_ref,
                    sp_ref, ss_ref, xn_ref, *, layer, cols):
    tm, d = xp_ref.shape
    _load_normed(xp_ref, xs_ref, g_ref, xn_ref)
    xn = xn_ref[...]
    logits = lbl_ref[...]
    e = jnp.exp(logits - jnp.max(logits, axis=0, keepdims=True))
    sm = e / jnp.sum(e, axis=0, keepdims=True)
    cum = sm[0:1]
    for i in range(1, layer + 1):
        cum = cum + sm[i:i + 1]
    lb = cum - sm[0:1]
    log_lb = jnp.log(lb)
    log_1m_lb = jnp.log1p(-lb)

    def put(p_ref, s_ref, lo, val):
        p_ref[:, lo:lo + cols] = val[:tm].astype(p_ref.dtype)
        s_ref[:, lo:lo + cols] = val[tm:].astype(s_ref.dtype)

    for c in range(d // cols):
        lo = c * cols
        put(qp_ref, qs_ref, lo, _silu(_dot(xn, w_ref[:, lo:lo + cols])))
        z = _dot(xn, w_ref[:, d + lo:d + lo + cols])
        ez = _exp_neg(jnp.abs(z))
        ez1 = 1.0 + ez
        log_sig = jnp.minimum(z, 0.0) - jnp.log(ez1)
        a = log_lb[:, lo:lo + cols]
        b = log_1m_lb[:, lo:lo + cols] + log_sig
        delta = a - b
        g = jnp.where(delta != delta, a + b,
                      jnp.maximum(a, b) + jnp.log(1.0 + _exp_neg(jnp.abs(delta))))
        put(gp_ref, gs_ref, lo, g)
        sig_neg = jnp.where(z >= 0.0, ez, 1.0) * (1.0 / ez1)
        put(kp_ref, ks_ref, lo, (1.0 - lb[:, lo:lo + cols]) * sig_neg)
        put(vp_ref, vs_ref, lo, _dot(xn, w_ref[:, 2 * d + lo:2 * d + lo + cols]))
        put(sp_ref, ss_ref, lo, _silu(_dot(xn, w_ref[:, 3 * d + lo:3 * d + lo + cols])))


def _hg_proj(xp, xs, gains, w_in, lb_logits, layer, j):
    d = xp.shape[1]
    tm, ts, steps = _token_tiles(xp, xs)
    out_shape, out_specs = [], []
    for dt in (BF16, BF16, F32, BF16, BF16):
        out_shape += [jax.ShapeDtypeStruct(xp.shape, dt), jax.ShapeDtypeStruct(xs.shape, dt)]
        out_specs += list(_token_specs(tm, ts, d))
    return pl.pallas_call(
        functools.partial(_hg_proj_kernel, layer=j, cols=2 * LANE),
        out_shape=tuple(out_shape),
        grid=(steps,),
        in_specs=[*_token_specs(tm, ts, d),
                  _resident((None, None, 1, d), (layer, 1, 0, 0)),
                  _resident((None,) + w_in.shape[1:], (j, 0, 0)),
                  _resident(lb_logits.shape)],
        out_specs=tuple(out_specs),
        scratch_shapes=[pltpu.VMEM((tm + ts, d), BF16)],
        compiler_params=_cparams(1),
        name="hg_proj",
    )(xp, xs, gains.reshape(gains.shape[0], gains.shape[1], 1, d), w_in, lb_logits)


def _hg_tables(chunk, block):
    n_levels = int(math.log2(block))
    assert 2 ** n_levels == block and chunk % block == 0 and chunk == LANE
    t = np.arange(chunk)[:, None]
    s = np.arange(chunk)[None, :]
    same_run = (t // block) == (s // block)
    sums = np.concatenate([same_run & (s <= t), same_run & (s > t)], axis=0)
    masks = [t == s]
    for l in range(n_levels):
        m = 2 ** l
        mid = (t // (2 * m)) * (2 * m) + m
        masks.append(((t // (2 * m)) == (s // (2 * m))) & (t >= mid) & (s < mid))
    r = np.arange(chunk)
    rows = np.stack([r % 2 == 1, r % 4 == 0, r % 4 >= 2, r % 4 == 3,
                     np.where(r % 8 >= 4, 1.0, -1.0)]).astype(np.float32)
    rows = np.broadcast_to(rows[:, :, None], rows.shape + (LANE,))
    block_causal = ((t // HG_FAST_BLOCK) == (s // HG_FAST_BLOCK)) & (s <= t)
    rows = np.concatenate([rows, block_causal[None].astype(np.float32)])
    sums = jnp.asarray(sums.astype(np.float32), dtype=BF16)
    masks = jnp.asarray(np.stack(masks).astype(np.float32), dtype=BF16)
    return sums, masks, jnp.asarray(rows)


def _level_exponents(b, g2, b_ref, rows_ref, block):
    c = b.shape[0]
    out = []
    n_levels = int(math.log2(block))
    for l in range(n_levels):
        m = 2 ** l
        if m == 1:
            out.append(g2 * rows_ref[0])
        elif m == 2:
            nxt = pltpu.roll(g2, c - 1, 0)
            prv = pltpu.roll(g2, 1, 0)
            out.append(nxt * rows_ref[1] + g2 * rows_ref[2] + prv * rows_ref[3])
        elif 2 * m == SUBLANE:
            pieces = [b[lo:lo + SUBLANE] - b_ref[pl.ds(lo + m - 1, 1), :]
                      for lo in range(0, c, SUBLANE)]
            out.append(jnp.concatenate(pieces, axis=0) * rows_ref[4])
        else:
            pieces = []
            for lo in range(0, c, 2 * m):
                ref_row = b_ref[pl.ds(lo + m - 1, 1), :]
                pieces.append(ref_row - b[lo:lo + m])
                pieces.append(b[lo + m:lo + 2 * m] - ref_row)
            out.append(jnp.concatenate(pieces, axis=0))
    return out


def _hg_chunk(q, k, v, g, sums_ref, masks_ref, rows_ref, b_ref, block):
    c = q.shape[0]
    g2 = g * LOG2E
    g_hi = g2.astype(BF16)
    g_lo = (g2 - g_hi.astype(F32)).astype(BF16)
    g_split = jnp.concatenate([g_hi, g_lo], axis=1)
    both = _dot(sums_ref[:c, :] if block == c else sums_ref[...], g_split)
    yield
    both = both[:, :LANE] + both[:, LANE:]
    if block == c:
        b = both
        b_rest = b[c - 1:c, :] - b
    else:
        b, b_rest = both[:c], both[c:]
    b_ref[...] = b
    level_scores = [_dot_nt(q, k)]
    for nd in _level_exponents(b, g2, b_ref, rows_ref, block):
        e = jnp.exp2(nd).astype(BF16)
        level_scores.append(_dot_nt(q * e, k * e))
    eb = jnp.exp2(b)
    q_dec = q * eb.astype(BF16)
    k_dec = k * jnp.exp2(b_rest).astype(BF16)
    yield
    scores = level_scores[0].astype(BF16) * masks_ref[0]
    for l, s in enumerate(level_scores[1:]):
        scores = scores + s.astype(BF16) * masks_ref[1 + l]
    o_intra = _dot(scores, v)
    return o_intra, q_dec, k_dec, eb


def _hg_chunk_fast(q, k, v, g, sums_ref, rows_ref):
    c = q.shape[0]
    blk = HG_FAST_BLOCK
    g2 = g * LOG2E
    g_hi = g2.astype(BF16)
    g_lo = (g2 - g_hi.astype(F32)).astype(BF16)
    b = _dot(sums_ref[:c, :], jnp.concatenate([g_hi, g_lo], axis=1))
    yield
    b = b[:, :LANE] + b[:, LANE:]
    qf = q.astype(F32)
    kf = k.astype(F32)
    q_blocks, k_blocks, cross = [], [], [jnp.zeros((blk, c), F32)]
    for i in range(c // blk):
        rows = slice(i * blk, (i + 1) * blk)
        if i == 0:
            d = b[rows]
        else:
            r = b[i * blk - 1:i * blk, :]
            d = b[rows] - r
        q_i = (qf[rows] * jnp.exp2(d)).astype(BF16)
        q_blocks.append(q_i)
        k_blocks.append((kf[rows] * jnp.exp2(-d)).astype(BF16))
        if i > 0:
            k_before = (kf[:i * blk] * jnp.exp2(r - b[:i * blk])).astype(BF16)
            k_before = jnp.concatenate(
                [k_before, jnp.zeros((c - i * blk, LANE), BF16)], axis=0)
            cross.append(_dot_nt(q_i, k_before))
    within = _dot_nt(jnp.concatenate(q_blocks, axis=0), jnp.concatenate(k_blocks, axis=0))
    eb = jnp.exp2(b)
    q_dec = (qf * eb).astype(BF16)
    k_dec = (kf * jnp.exp2(b[c - 1:c, :] - b)).astype(BF16)
    yield
    scores = within * rows_ref[5] + jnp.concatenate(cross, axis=0)
    o_intra = _dot(scores.astype(BF16), v)
    return o_intra, q_dec, k_dec, eb


def _hg_prompt_step(q_ref, k_ref, v_ref, g_ref, sg_ref, gn_ref, sums_ref, masks_ref,
                    rows_ref, blocks_ref, og_ref, s_ref, st_ref, b_ref,
                    *, it, last_it, cpi, chunk, hps, dh):
    @pl.when(it == 0)
    def _():
        st_ref[...] = jnp.zeros(st_ref.shape, F32)

    def run(fast):
        rows = [pl.ds(cc * chunk, chunk) for cc in range(cpi)]
        heads = [slice(hh * dh, (hh + 1) * dh) for hh in range(hps)]
        pairs = [(hh, cc) for hh in range(hps) for cc in range(cpi)]
        vs = {(hh, cc): v_ref[rows[cc], heads[hh]] for hh, cc in pairs}
        chains = []
        for hh, cc in pairs:
            args = (q_ref[rows[cc], heads[hh]], k_ref[rows[cc], heads[hh]], vs[hh, cc],
                    g_ref[rows[cc], heads[hh]])
            if fast:
                chains.append(_hg_chunk_fast(*args, sums_ref, rows_ref))
            else:
                chains.append(_hg_chunk(*args, sums_ref, masks_ref, rows_ref,
                                        b_ref.at[hh * cpi + cc], chunk))
        intra = dict(zip(pairs, _run_staged(chains)))
        updates = {p: _dot_tn(vs[p], intra[p][2]) for p in pairs}
        states = [st_ref[hh] for hh in range(hps)]
        outs = {}
        for cc in range(cpi):
            inters = [_dot_nt(intra[hh, cc][1], states[hh].astype(BF16)) for hh in range(hps)]
            for hh in range(hps):
                o_intra, _, _, eb = intra[hh, cc]
                states[hh] = states[hh] * eb[chunk - 1:chunk, :] + updates[hh, cc]
                outs[hh, cc] = _head_norm_gate(o_intra + inters[hh], gn_ref[hh],
                                               sg_ref[rows[cc], heads[hh]])
        for hh in range(hps):
            st_ref[hh] = states[hh]
        for hh, cc in pairs:
            og_ref[rows[cc], heads[hh]] = outs[hh, cc]

    block_sums = _dot(blocks_ref[...], jnp.abs(g_ref[...]).astype(BF16))
    fast_ok = jnp.max(block_sums) * LOG2E < HG_FAST_LIMIT
    pl.when(fast_ok)(lambda: run(True))
    pl.when(jnp.logical_not(fast_ok))(lambda: run(False))

    @pl.when(it == last_it)
    def _():
        for hh in range(hps):
            s_ref[0, hh] = st_ref[hh].T


def _hg_sample_step(q_ref, k_ref, v_ref, g_ref, sg_ref, gn_ref, sums_ref, masks_ref, rows_ref,
                    s0_ref, og_ref, s_ref, b_ref, *, seqs, dec_len, n_heads, dh):
    rows = seqs * dec_len
    row_seq = lax.broadcasted_iota(jnp.int32, (rows, 1), 0) // dec_len
    col_seq = lax.broadcasted_iota(jnp.int32, (1, rows), 1) // dec_len
    for hh in range(n_heads):
        cols = slice(hh * dh, (hh + 1) * dh)
        vb = v_ref[:, cols]
        (o, q_dec, k_dec, eb), = _run_staged([_hg_chunk(
            q_ref[:, cols], k_ref[:, cols], vb, g_ref[:, cols], sums_ref, masks_ref, rows_ref,
            b_ref, dec_len)])
        eb_t = eb.T
        k_dec_t = k_dec.astype(F32).T
        inter = jnp.zeros(o.shape, F32)
        for i in range(seqs):
            s_old = s0_ref[i, hh]
            inter = jnp.where(row_seq == i, _dot(q_dec, s_old.astype(BF16)), inter)
            k_i = jnp.where(col_seq == i, k_dec_t, 0.0).astype(BF16)
            last = (i + 1) * dec_len - 1
            s_ref[i, hh] = s_old * eb_t[:, last:last + 1] + _dot(k_i, vb)
        og_ref[:, cols] = _head_norm_gate(o + inter, gn_ref[hh], sg_ref[:, cols])


def _hg_core_kernel(*refs, n_prev, steps_per_unit, prompt_kw, sample_kw):
    prompt_in, sample_in = refs[:10], refs[10:20]
    og_p, s_p, og_s, s_s, st_ref, b_ref, bs_ref = refs[20 + n_prev:]
    n_h, n_it = pl.num_programs(1), pl.num_programs(2)
    step = (pl.program_id(0) * n_h + pl.program_id(1)) * n_it + pl.program_id(2)

    @pl.when(step % steps_per_unit == 0)
    def _():
        _hg_sample_step(*sample_in, og_s, s_s, bs_ref, **sample_kw)

    _hg_prompt_step(*prompt_in, og_p, s_p, st_ref, b_ref, it=pl.program_id(2),
                    last_it=n_it - 1, **prompt_kw)


def _hg_core(prompt, sample, gn, s0_all, j, prev_prompt, prev_sample, n_seqs, seq_len, dec_len):
    heads = HG_HEADS
    hps = HG_HEADS_PER_STEP
    n_layers, dec_seqs = s0_all.shape[:2]
    dh = s0_all.shape[3]
    d = heads * dh
    gn4 = gn.reshape(gn.shape[0], heads, 1, dh)

    chunk = math.gcd(seq_len, HG_CHUNK)
    n_chunks = seq_len // chunk
    cpi = math.gcd(n_chunks, HG_CHUNKS_PER_ITER)
    grid = (n_seqs, heads // hps, n_chunks // cpi)
    step_rows = cpi * chunk
    sums, masks, row_tabs = _hg_tables(chunk, chunk)
    block_of_row = np.arange(step_rows) // HG_FAST_BLOCK
    blocks = jnp.asarray(np.arange(step_rows // HG_FAST_BLOCK)[:, None] == block_of_row[None, :],
                         dtype=BF16)
    per_step = pl.BlockSpec((step_rows, hps * dh), lambda b, h, t: (b * grid[2] + t, h))
    sp_spec, sp_shape, prev_p, prev_p_spec = _state_out(
        prev_prompt, (n_layers, n_seqs, heads, dh, dh), j, (1, hps, dh, dh),
        lambda b, h, t: (b, h, 0, 0))
    prompt_specs = [per_step] * 5 + [
        pl.BlockSpec((None, hps, 1, dh), lambda b, h, t: (j, h, 0, 0)),
        _resident(sums.shape), _resident(masks.shape), _resident(row_tabs.shape),
        _resident(blocks.shape)]
    prompt_args = (*prompt, gn4, sums, masks, row_tabs, blocks)

    rows = math.gcd(dec_seqs * dec_len, HG_CHUNK)
    seqs = rows // dec_len
    n_steps = grid[0] * grid[1] * grid[2]
    seq_blocks = dec_seqs // seqs
    hu = max(1, seq_blocks * heads // n_steps)
    units_per_block = heads // hu
    spu = n_steps // (seq_blocks * units_per_block)
    assert seqs * dec_len == rows and seq_blocks * units_per_block * spu == n_steps
    sums_s, masks_s, row_tabs_s = _hg_tables(rows, dec_len)

    def unit(b, h, t):
        u = ((b * grid[1] + h) * grid[2] + t) // spu
        return u // units_per_block, u % units_per_block

    per_unit = pl.BlockSpec((rows, hu * dh), lambda b, h, t: unit(b, h, t))
    state_in = pl.BlockSpec((None, seqs, hu, dh, dh),
                            lambda b, h, t: (j,) + unit(b, h, t) + (0, 0))
    ss_spec, ss_shape, prev_s, prev_s_spec = _state_out(
        prev_sample, s0_all.shape, j, (seqs, hu, dh, dh),
        lambda b, h, t: unit(b, h, t) + (0, 0))
    sample_specs = [per_unit] * 5 + [
        pl.BlockSpec((None, hu, 1, dh), lambda b, h, t: (j, unit(b, h, t)[1], 0, 0)),
        _resident(sums_s.shape), _resident(masks_s.shape), _resident(row_tabs_s.shape),
        state_in]
    sample_args = (*sample, gn4, sums_s, masks_s, row_tabs_s, s0_all)

    aliases = {}
    if prev_p:
        aliases[20] = 1
    if prev_s:
        aliases[20 + len(prev_p)] = 3
    return pl.pallas_call(
        functools.partial(
            _hg_core_kernel, n_prev=len(prev_p) + len(prev_s), steps_per_unit=spu,
            prompt_kw=dict(cpi=cpi, chunk=chunk, hps=hps, dh=dh),
            sample_kw=dict(seqs=seqs, dec_len=dec_len, n_heads=hu, dh=dh)),
        out_shape=(jax.ShapeDtypeStruct((n_seqs * seq_len, d), BF16), sp_shape,
                   jax.ShapeDtypeStruct((dec_seqs * dec_len, d), BF16), ss_shape),
        grid=grid,
        in_specs=prompt_specs + sample_specs + prev_p_spec + prev_s_spec,
        out_specs=(per_step, sp_spec, per_unit, ss_spec),
        input_output_aliases=aliases,
        scratch_shapes=[pltpu.VMEM((hps, dh, dh), F32),
                        pltpu.VMEM((hps * cpi, chunk, dh), F32),
                        pltpu.VMEM((rows, dh), F32)],
        compiler_params=_cparams(3),
        name="hg_core",
    )(*prompt_args, *sample_args, *prev_p, *prev_s)


def kernel(x_prompt, x_sample, state_ret, state_hgrn, norm_gain, ffn_w_up, ffn_w_down,
           ret_w_in, ret_norm, ret_w_out, hg_w_in, hg_lb_logits, hg_norm, hg_w_out, final_norm):
    n_seqs, seq_len, d = x_prompt.shape
    dec_seqs, dec_len, _ = x_sample.shape
    depth = norm_gain.shape[0]
    n_ret, n_hg = state_ret.shape[0], state_hgrn.shape[0]

    xp = x_prompt.reshape(n_seqs * seq_len, d)
    xs = x_sample.reshape(dec_seqs * dec_len, d)
    w_up, w_down = ffn_w_up.astype(BF16), ffn_w_down.astype(BF16)
    ret_in, ret_out = ret_w_in.astype(BF16), ret_w_out.astype(BF16)
    hg_in, hg_out = hg_w_in.astype(BF16), hg_w_out.astype(BF16)

    ret_p = ret_s = hg_p = hg_s = None
    for layer in range(depth):
        xp, xs = _ffn(xp, xs, norm_gain, w_up, w_down, layer, 0)
        j = layer // 2
        if layer % 2 == 0:
            qp, qs, kp, ks, vp, vs, gp, gs = _ret_proj(
                xp, xs, norm_gain, ret_in, layer, j, seq_len, dec_len)
            ap, ret_p, a_s, ret_s = _ret_core(
                (qp, kp, vp), (qs, ks, vs), ret_norm, state_ret, j, ret_p, ret_s,
                n_seqs, seq_len, dec_len)
            mix = (ap, a_s, ret_out, j, (gp, gs))
        else:
            qp, qs, kp, ks, gp, gs, vp, vs, sp, ss = _hg_proj(
                xp, xs, norm_gain, hg_in, hg_lb_logits, layer, j)
            ap, hg_p, a_s, hg_s = _hg_core(
                (qp, kp, vp, gp, sp), (qs, ks, vs, gs, ss), hg_norm, state_hgrn, j, hg_p, hg_s,
                n_seqs, seq_len, dec_len)
            mix = (ap, a_s, hg_out, j, None)
        xp, xs = _ffn(xp, xs, norm_gain, w_up, w_down, layer, 1, mix,
                      final_norm if layer == depth - 1 else None)
    return (xp.reshape(n_seqs, seq_len, d), xs.reshape(dec_seqs, dec_len, d),
            ret_p, ret_s, hg_p, hg_s)
```
